```python
import math
import jax
import jax.numpy as jnp
from jax import lax
import numpy as np

D_MODEL = 2048
BATCH = 32
SEQ = 256
DEPTH = 4
DEC_BATCH = 2
DEC_SEQ = 4096
PAST_LEN = 256

GRID_W = 64
EPS = 1e-6

GLA_HEADS = 4
GLA_DK = 128
GLA_DV = 256
GLA_RANK = 16
GLA_TAU = 16.0
GLA_CHUNK = 64
GLA_WIDTH = GLA_HEADS * GLA_DV

POOL_WINDOWS = (2, 4, 8, 16)
POOL_GROUPS = 4
POOL_GROUP_DIM = 256
POOL_WIDTH = POOL_GROUPS * POOL_GROUP_DIM

MLA_HEADS = 8
MLA_Q_LORA = 768
MLA_KV_LORA = 512
MLA_NOPE = 128
MLA_ROPE = 64
MLA_V = 128
MLA_WIDTH = MLA_HEADS * MLA_V
MLA_SCALE = 1.0 / math.sqrt(MLA_NOPE + MLA_ROPE)
ROPE_THETA = 10000.0
QBLK = 128

N_BRANCH = 3

N_GROUPS = 4
EXPERTS_PER_GROUP = 4
N_EXPERTS = N_GROUPS * EXPERTS_PER_GROUP
TOP_K = 2
EXPERT_FF = 512

SPLIT_SIZES = (GLA_HEADS * GLA_DK, GLA_HEADS * GLA_DK, GLA_WIDTH, GLA_WIDTH, 2 * GLA_RANK,
               POOL_WIDTH, MLA_Q_LORA, MLA_KV_LORA, MLA_ROPE, N_BRANCH * D_MODEL)
D_IN = int(sum(SPLIT_SIZES))
SPLIT_IDX = tuple(int(i) for i in np.cumsum(SPLIT_SIZES)[:-1])

kernel_name = 'hybrid_gla_pool_mla_hmoe_diffusion_step'

F32 = jnp.float32


def rms_norm(x, g):
    xf = x.astype(F32)
    y = xf * lax.rsqrt(jnp.mean(xf * xf, axis=-1, keepdims=True) + EPS)
    return (y * g.astype(F32)).astype(x.dtype)


def axial_rope_tables(L):
    rows = L // GRID_W
    row = jnp.broadcast_to(jnp.arange(rows)[:, None], (rows, GRID_W)).reshape(-1).astype(F32)
    col = jnp.broadcast_to(jnp.arange(GRID_W)[None, :], (rows, GRID_W)).reshape(-1).astype(F32)
    n_freq = MLA_ROPE // 4
    inv = ROPE_THETA ** (-jnp.arange(n_freq, dtype=F32) / n_freq)
    ang = jnp.stack([row[:, None] * inv, col[:, None] * inv], axis=1)
    return jnp.cos(ang), jnp.sin(ang)


def rope2d(x, cos, sin):
    xr = x.reshape(x.shape[:-1] + (2, 2, MLA_ROPE // 4))
    x1, x2 = xr[..., 0, :], xr[..., 1, :]
    cos = cos.astype(x.dtype)
    sin = sin.astype(x.dtype)
    out = jnp.stack([x1 * cos - x2 * sin, x2 * cos + x1 * sin], axis=-2)
    return out.reshape(x.shape)


def gla_chunked(q, k, v, log_a, s0):
    B, L, H, K = q.shape
    V = v.shape[-1]
    n = L // GLA_CHUNK

    def chunks(t):
        return t.astype(F32).reshape(B, n, GLA_CHUNK, H, t.shape[-1]).transpose(1, 0, 3, 2, 4)

    qc, kc, vc, ac = chunks(q), chunks(k), chunks(v), chunks(log_a)
    b = jnp.cumsum(ac, axis=3)
    b_last = b[:, :, :, -1:, :]
    q_t = qc * jnp.exp(b)
    k_t = kc * jnp.exp(-b)
    k_e = kc * jnp.exp(b_last - b)
    lower = jnp.tril(jnp.ones((GLA_CHUNK, GLA_CHUNK), dtype=bool))
    a = jnp.where(lower, jnp.einsum('nbhck,nbhsk->nbhcs', q_t, k_t), 0.0)
    o_intra = jnp.einsum('nbhcs,nbhsv->nbhcv', a, vc)
    ds = jnp.einsum('nbhck,nbhcv->nbhkv', k_e, vc)
    decay = jnp.exp(b_last[:, :, :, 0, :])

    def step(S, inp):
        q_i, dec_i, ds_i = inp
        o_i = jnp.einsum('bhck,bhkv->bhcv', q_i, S)
        return dec_i[..., None] * S + ds_i, o_i

    s_fin, o_inter = lax.scan(step, s0.astype(F32), (q_t, decay, ds))
    o = (o_intra + o_inter).transpose(1, 0, 3, 2, 4).reshape(B, L, H, V)
    return o, s_fin


def multiscale_pool(u):
    B, L, C = u.shape
    uf = u.astype(F32)
    cs = jnp.concatenate([jnp.zeros((B, 1, C), F32), jnp.cumsum(uf, axis=1)], axis=1)
    t = jnp.arange(L)
    outs = []
    for g, w in enumerate(POOL_WINDOWS):
        lo = jnp.clip(t - w // 2, 0, L)
        hi = jnp.clip(t + w - w // 2, 0, L)
        csg = cs[:, :, g * POOL_GROUP_DIM:(g + 1) * POOL_GROUP_DIM]
        outs.append((csg[:, hi] - csg[:, lo]) / (hi - lo).astype(F32)[:, None])
    pooled = jnp.concatenate(outs, axis=-1)
    return (pooled - uf).astype(u.dtype)


def mla_decompress(ckv, p):
    B, Lk, _ = ckv.shape
    kv = (ckv @ p['w_mla_kv_up']).reshape(B, Lk, MLA_HEADS, MLA_NOPE + MLA_V)
    return rms_norm(kv[..., :MLA_NOPE], p['g_k_nope']), kv[..., MLA_NOPE:]


def block_attention(q_nope, q_ropes, k_nopes, k_ropes, vs):
    B, L, H, _ = q_nope.shape
    nb = L // QBLK
    v_all = jnp.concatenate(vs, axis=1)

    def blocks(t):
        return t.reshape((B, nb, QBLK) + t.shape[2:]).swapaxes(0, 1)

    def one_block(qb):
        qn, qrs = qb
        logits = jnp.concatenate(
            [jnp.einsum('bqhd,bkhd->bhqk', qn, kn) + jnp.einsum('bqhr,bkr->bhqk', qr, kr)
             for qr, kn, kr in zip(qrs, k_nopes, k_ropes)], axis=-1)
        probs = jax.nn.softmax(logits.astype(F32) * MLA_SCALE, axis=-1).astype(v_all.dtype)
        return jnp.einsum('bhqk,bkhv->bqhv', probs, v_all)

    out = lax.map(one_block, (blocks(q_nope), tuple(blocks(qr) for qr in q_ropes)))
    return out.swapaxes(0, 1).reshape(B, L, H * MLA_V)


def token_mixer(h, p, ctx):
    B, L, D = h.shape
    gq, gk, gv, gr, glow, pin, qa, kva, kr, gl = jnp.split(h @ p['w_in'], SPLIT_IDX, axis=-1)

    q = gq.reshape(B, L, GLA_HEADS, GLA_DK) * (GLA_DK ** -0.5)
    k = gk.reshape(B, L, GLA_HEADS, GLA_DK)
    v = gv.reshape(B, L, GLA_HEADS, GLA_DV)
    dec = jnp.einsum('bldr,drk->bldk', glow.reshape(B, L, 2, GLA_RANK), p['w_gla_dec']) + p['b_gla_dec']
    log_a = (jax.nn.log_sigmoid(dec.astype(F32)) / GLA_TAU).reshape(B, L, 2, GLA_HEADS, GLA_DK)
    if ctx is None:
        s0_f = jnp.zeros((B, GLA_HEADS, GLA_DK, GLA_DV), F32)
        s0_b = s0_f
    else:
        s0_f, s0_b = ctx[2], ctx[3]
    rev = lambda t: jnp.flip(t, axis=1)
    o_f, s_f = gla_chunked(q, k, v, log_a[:, :, 0], s0_f)
    o_b, s_b = gla_chunked(rev(q), rev(k), rev(v), rev(log_a[:, :, 1]), s0_b)
    o = rms_norm((o_f + rev(o_b)).astype(h.dtype), p['g_gla']) * jax.nn.silu(gr.reshape(B, L, GLA_HEADS, GLA_DV))
    gla_out = o.reshape(B, L, GLA_WIDTH)

    pooled = multiscale_pool(pin).reshape(B, L, POOL_GROUPS, POOL_GROUP_DIM)
    pool_out = jnp.einsum('blgc,gcd->blgd', pooled, p['w_pool']).reshape(B, L, POOL_WIDTH) * p['pool_scale']

    qh = (rms_norm(qa, p['g_mla_qa']) @ p['w_mla_q_up']).reshape(B, L, MLA_HEADS, MLA_NOPE + MLA_ROPE)
    q_nope = rms_norm(qh[..., :MLA_NOPE], p['g_q_nope'])
    q_rope = rms_norm(qh[..., MLA_NOPE:], p['g_q_rope'])
    ckv = rms_norm(kva, p['g_mla_kva'])
    k_nope, v_m = mla_decompress(ckv, p)
    k_rope = rms_norm(kr, p['g_k_rope'])
    if ctx is None:
        attn = block_attention(q_nope, (q_rope,), (k_nope,), (k_rope,), (v_m,))
        new_ctx = (ckv, k_rope, s_f, s_b)
    else:
        cos, sin = axial_rope_tables(L)
        q_rope_rot = rope2d(q_rope, cos[:, None], sin[:, None])
        k_rope_rot = rope2d(k_rope, cos, sin)
        k_nope_c, v_c = mla_decompress(ctx[0], p)
        attn = block_attention(q_nope, (q_rope_rot, q_rope), (k_nope, k_nope_c),
                               (k_rope_rot, ctx[1]), (v_m, v_c))
        new_ctx = None

    gates = jax.nn.sigmoid(gl.reshape(B, L, N_BRANCH, D))
    merged = (gates[:, :, 0] * (gla_out @ p['w_br_gla'])
              + gates[:, :, 1] * (pool_out @ p['w_br_pool'])
              + gates[:, :, 2] * (attn @ p['w_br_mla']))
    return merged @ p['w_o'], new_ctx


def hier_moe(h, p):
    B, L, D = h.shape
    t = h.reshape(-1, D)
    pg = jax.nn.softmax((t @ p['w_group_router'] + p['b_group_router']).astype(F32), axis=-1)
    grp = jnp.argmax(pg, axis=-1)
    grp_oh = jax.nn.one_hot(grp, N_GROUPS, dtype=F32)
    pg_top = jnp.max(pg, axis=-1, keepdims=True)
    el = (t @ p['w_expert_router'] + p['b_expert_router']).astype(F32).reshape(-1, N_GROUPS, EXPERTS_PER_GROUP)
    el_sel = jnp.einsum('tge,tg->te', el, grp_oh)
    pe = jax.nn.softmax(el_sel, axis=-1)
    top_v, top_i = lax.top_k(pe, TOP_K)
    top_v = top_v / jnp.sum(top_v, axis=-1, keepdims=True)
    w_grp = jnp.sum(jax.nn.one_hot(top_i, EXPERTS_PER_GROUP, dtype=F32) * top_v[..., None], axis=1)
    comb = (grp_oh[:, :, None] * (pg_top * w_grp)[:, None, :]).reshape(-1, N_EXPERTS).astype(h.dtype)
    hid = jax.nn.silu(jnp.einsum('td,edf->tef', t, p['w_exp_gate'])) * jnp.einsum('td,edf->tef', t, p['w_exp_up'])
    y = jnp.einsum('tef,efd->td', hid * comb[:, :, None], p['w_exp_down'])
    return y.reshape(B, L, D)


def trunk_layer(x, mod, p, ctx):
    mod = mod.astype(x.dtype)
    sh1, sc1, gt1, sh2, sc2, gt2 = (mod[:, None, i] for i in range(6))
    h = rms_norm(x, p['g_norm1']) * (1.0 + sc1) + sh1
    mix, new_ctx = token_mixer(h, p, ctx)
    x = x + gt1 * mix
    h2 = rms_norm(x, p['g_norm2']) * (1.0 + sc2) + sh2
    x = x + gt2 * hier_moe(h2, p)
    return x, new_ctx


def setup_inputs(seed: int = 0) -> dict:
    key = jax.random.key(seed)
    ks = jax.random.split(key, 40)
    D = D_MODEL

    def nrm(k, shape, s):
        return jax.random.normal(k, shape, F32) * s

    return {
        'x_prompt': nrm(ks[0], (BATCH, SEQ, D), 1.0),
        'x_sample': nrm(ks[1], (DEC_BATCH, DEC_SEQ, D), 1.0),
        'c': nrm(ks[2], (DEC_BATCH, D), 1.0),
        'cache_mla_ckv': nrm(ks[3], (DEC_BATCH, DEPTH, PAST_LEN, MLA_KV_LORA), 1.0),
        'cache_mla_krope': nrm(ks[4], (DEC_BATCH, DEPTH, PAST_LEN, MLA_ROPE), 1.0),
        'state_gla': nrm(ks[5], (DEC_BATCH, DEPTH, 2, GLA_HEADS, GLA_DK, GLA_DV), 0.5),
        'c_ctx': nrm(ks[6], (D,), 1.0),
        'w_ada': nrm(ks[7], (DEPTH, D, 6 * D), 0.5 * D ** -0.5),
        'b_ada': nrm(ks[8], (DEPTH, 6 * D), 0.02),
        'g_norm1': 1.0 + nrm(ks[9], (DEPTH, D), 0.02),
        'g_norm2': 1.0 + nrm(ks[10], (DEPTH, D), 0.02),
        'w_in': nrm(ks[11], (DEPTH, D, D_IN), D ** -0.5),
        'w_gla_dec': nrm(ks[12], (DEPTH, 2, GLA_RANK, GLA_HEADS * GLA_DK), GLA_RANK ** -0.5),
        'b_gla_dec': nrm(ks[13], (DEPTH, 2, GLA_HEADS * GLA_DK), 0.1),
        'g_gla': 1.0 + nrm(ks[14], (DEPTH, GLA_DV), 0.02),
        'w_pool': nrm(ks[15], (DEPTH, POOL_GROUPS, POOL_GROUP_DIM, POOL_GROUP_DIM), POOL_GROUP_DIM ** -0.5),
        'pool_scale': 1.0 + nrm(ks[16], (DEPTH, POOL_WIDTH), 0.1),
        'g_mla_qa': 1.0 + nrm(ks[17], (DEPTH, MLA_Q_LORA), 0.02),
        'w_mla_q_up': nrm(ks[18], (DEPTH, MLA_Q_LORA, MLA_HEADS * (MLA_NOPE + MLA_ROPE)), MLA_Q_LORA ** -0.5),
        'g_mla_kva': 1.0 + nrm(ks[19], (DEPTH, MLA_KV_LORA), 0.02),
        'w_mla_kv_up': nrm(ks[20], (DEPTH, MLA_KV_LORA, MLA_HEADS * (MLA_NOPE + MLA_V)), MLA_KV_LORA ** -0.5),
        'g_q_nope': 1.0 + nrm(ks[21], (DEPTH, MLA_NOPE), 0.02),
        'g_q_rope': 1.0 + nrm(ks[22], (DEPTH, MLA_ROPE), 0.02),
        'g_k_nope': 1.0 + nrm(ks[23], (DEPTH, MLA_NOPE), 0.02),
        'g_k_rope': 1.0 + nrm(ks[24], (DEPTH, MLA_ROPE), 0.02),
        'w_br_gla': nrm(ks[25], (DEPTH, GLA_WIDTH, D), GLA_WIDTH ** -0.5),
        'w_br_pool': nrm(ks[26], (DEPTH, POOL_WIDTH, D), POOL_WIDTH ** -0.5),
        'w_br_mla': nrm(ks[27], (DEPTH, MLA_WIDTH, D), MLA_WIDTH ** -0.5),
        'w_o': nrm(ks[28], (DEPTH, D, D), D ** -0.5),
        'w_group_router': nrm(ks[29], (DEPTH, D, N_GROUPS), D ** -0.5),
        'b_group_router': nrm(ks[30], (DEPTH, N_GROUPS), 0.01),
        'w_expert_router': nrm(ks[31], (DEPTH, D, N_EXPERTS), D ** -0.5),
        'b_expert_router': nrm(ks[32], (DEPTH, N_EXPERTS), 0.01),
        'w_exp_gate': nrm(ks[33], (DEPTH, N_EXPERTS, D, EXPERT_FF), D ** -0.5),
        'w_exp_up': nrm(ks[34], (DEPTH, N_EXPERTS, D, EXPERT_FF), D ** -0.5),
        'w_exp_down': nrm(ks[35], (DEPTH, N_EXPERTS, EXPERT_FF, D), EXPERT_FF ** -0.5),
    }


def reference(x_prompt, x_sample, c, cache_mla_ckv, cache_mla_krope, state_gla, c_ctx,
              w_ada, b_ada, g_norm1, g_norm2, w_in, w_gla_dec, b_gla_dec, g_gla,
              w_pool, pool_scale, g_mla_qa, w_mla_q_up, g_mla_kva, w_mla_kv_up,
              g_q_nope, g_q_rope, g_k_nope, g_k_rope, w_br_gla, w_br_pool, w_br_mla, w_o,
              w_group_router, b_group_router, w_expert_router, b_expert_router,
              w_exp_gate, w_exp_up, w_exp_down):
    def layer_params(l):
        return {
            'g_norm1': g_norm1[l], 'g_norm2': g_norm2[l], 'w_in': w_in[l],
            'w_gla_dec': w_gla_dec[l], 'b_gla_dec': b_gla_dec[l], 'g_gla': g_gla[l],
            'w_pool': w_pool[l], 'pool_scale': pool_scale[l],
            'g_mla_qa': g_mla_qa[l], 'w_mla_q_up': w_mla_q_up[l],
            'g_mla_kva': g_mla_kva[l], 'w_mla_kv_up': w_mla_kv_up[l],
            'g_q_nope': g_q_nope[l], 'g_q_rope': g_q_rope[l],
            'g_k_nope': g_k_nope[l], 'g_k_rope': g_k_rope[l],
            'w_br_gla': w_br_gla[l], 'w_br_pool': w_br_pool[l], 'w_br_mla': w_br_mla[l],
            'w_o': w_o[l],
            'w_group_router': w_group_router[l], 'b_group_router': b_group_router[l],
            'w_expert_router': w_expert_router[l], 'b_expert_router': b_expert_router[l],
            'w_exp_gate': w_exp_gate[l], 'w_exp_up': w_exp_up[l], 'w_exp_down': w_exp_down[l],
        }

    def modulation(cond, l):
        return (jax.nn.silu(cond) @ w_ada[l] + b_ada[l]).reshape(cond.shape[0], 6, D_MODEL)

    x = x_prompt
    ckv_list, krope_list, gla_list = [], [], []
    for l in range(DEPTH):
        x, (ckv, krope, s_f, s_b) = trunk_layer(x, modulation(c_ctx[None, :], l), layer_params(l), None)
        ckv_list.append(ckv)
        krope_list.append(krope)
        gla_list.append(jnp.stack([s_f, s_b], axis=1))
    y_prompt = x
    new_cache_mla_ckv = jnp.stack(ckv_list, axis=1)
    new_cache_mla_krope = jnp.stack(krope_list, axis=1)
    new_state_gla = jnp.stack(gla_list, axis=1).astype(x_prompt.dtype)

    xs = x_sample
    for l in range(DEPTH):
        ctx = (cache_mla_ckv[:, l], cache_mla_krope[:, l], state_gla[:, l, 0], state_gla[:, l, 1])
        xs, _ = trunk_layer(xs, modulation(c, l), layer_params(l), ctx)
    y_sample = xs

    return (y_prompt, y_sample, new_cache_mla_ckv, new_cache_mla_krope, new_state_gla)
```

```python
import functools
import math

import numpy as np
import jax
import jax.numpy as jnp
from jax import lax
from jax.experimental import pallas as pl
from jax.experimental.pallas import tpu as pltpu

F32 = jnp.float32
BF16 = jnp.bfloat16
I32 = jnp.int32

D = 2048
DEPTH = 4
EPS = 1e-6
GRID_W = 64
GLA_HEADS, GLA_DK, GLA_DV, GLA_RANK, GLA_TAU, GLA_CHUNK = 4, 128, 256, 16, 16.0, 64
POOL_WINDOWS = (2, 4, 8, 16)
POOL_GROUP_DIM = 256
MLA_HEADS, MLA_Q_LORA, MLA_KV_LORA, MLA_NOPE, MLA_ROPE, MLA_V = 8, 768, 512, 128, 64, 128
MLA_SCALE = 1.0 / math.sqrt(MLA_NOPE + MLA_ROPE)
ROPE_THETA = 10000.0
N_GROUPS, EXPERTS_PER_GROUP, EXPERT_FF = 4, 4, 512
SPLIT_SIZES = (512, 512, 1024, 1024, 32, 1024, 768, 512, 64, 6144)

LANES = 128
SLAB = 256
VMEM_LIMIT_BYTES = 60000 * 1024

C_GL = 0
C_Q = 6144
C_K = 6656
C_V = 7168
C_GR = 8192
C_PIN = 9216
C_KVA = 10240
C_QA = 10752
C_GLOW = 11520
C_KR = 11648
NP = 11776


def _dot(a, b):
    return jnp.dot(a, b, preferred_element_type=F32)


def _dot_nt(a, b):
    return lax.dot_general(a, b, (((1,), (1,)), ((), ())), preferred_element_type=F32)


def _dot_tn(a, b):
    return lax.dot_general(a, b, (((0,), (0,)), ((), ())), preferred_element_type=F32)


def _split3(x):
    hi = x.astype(BF16)
    r1 = x - hi.astype(F32)
    mid = r1.astype(BF16)
    lo = (r1 - mid.astype(F32)).astype(BF16)
    return hi, mid, lo


def _silu(x):
    return x * jax.nn.sigmoid(x)


def _rms(x, n):
    ms = jnp.sum(x * x, axis=-1, keepdims=True) * (1.0 / n)
    return x * lax.rsqrt(ms + EPS)


def _cparams(sem, vmem=VMEM_LIMIT_BYTES):
    return pltpu.CompilerParams(dimension_semantics=sem, vmem_limit_bytes=vmem)


def _pick_tile(cands, *extents):
    for c in cands:
        if all(e % c == 0 for e in extents):
            return c
    raise ValueError(f"no tile in {cands} divides {extents}")


def _mod_kernel(c_ref, w_ref, b_ref, o_ref):
    a = _silu(c_ref[...]).astype(BF16)
    o_ref[0] = _dot(a, w_ref[0].astype(BF16)) + b_ref[0]


def _modulation(cond, w_ada, b_ada):
    tn = 1024
    out = pl.pallas_call(
        _mod_kernel,
        grid=(DEPTH, 6 * D // tn),
        in_specs=[
            pl.BlockSpec((8, D), lambda l, j: (0, 0)),
            pl.BlockSpec((1, D, tn), lambda l, j: (l, 0, j)),
            pl.BlockSpec((1, 1, tn), lambda l, j: (l, 0, j)),
        ],
        out_specs=pl.BlockSpec((1, 8, tn), lambda l, j: (l, 0, j)),
        out_shape=jax.ShapeDtypeStruct((DEPTH, 8, 6 * D), F32),
        compiler_params=_cparams(("parallel", "parallel"), 40 * 1024 * 1024),
        name="adaln_modulation",
    )(cond, w_ada, b_ada.reshape(DEPTH, 1, 6 * D))
    return out.reshape(DEPTH, 8, 6, D)


def _proj_in_kernel(x_ref, mod_ref, g_ref, w_ref, o_ref, h_scr, *, tm):
    @pl.when(pl.program_id(1) == 0)
    def _():
        m = mod_ref[0, 0]
        g = g_ref[0]

        def body(r, carry):
            rows = pl.ds(pl.multiple_of(r * LANES, LANES), LANES)
            y = _rms(x_ref[rows, :], D) * g
            h_scr[rows, :] = (y * (1.0 + m[1:2]) + m[0:1]).astype(BF16)
            return carry

        lax.fori_loop(0, tm // LANES, body, 0)

    o_ref[...] = _dot(h_scr[...], w_ref[0]).astype(BF16)


def _proj_in(x, mods, g1, wp, l, dims):
    T = x.shape[0]
    tm = _pick_tile((1024, 512, 256), dims["t_ctx"], dims["l_lat"])
    tn = 512
    gmap = _group_map(dims, tm)
    return pl.pallas_call(
        functools.partial(_proj_in_kernel, tm=tm),
        grid=(T // tm, NP // tn),
        in_specs=[
            pl.BlockSpec((tm, D), lambda i, j: (i, 0)),
            pl.BlockSpec((1, 1, 6, D), lambda i, j: (l, gmap(i), 0, 0)),
            pl.BlockSpec((1, 1, D), lambda i, j: (l, 0, 0)),
            pl.BlockSpec((1, D, tn), lambda i, j: (l, 0, j)),
        ],
        out_specs=pl.BlockSpec((tm, tn), lambda i, j: (i, j)),
        out_shape=jax.ShapeDtypeStruct((T, NP), BF16),
        scratch_shapes=[pltpu.VMEM((tm, D), BF16)],
        compiler_params=_cparams(("parallel", "arbitrary")),
        name="norm1_proj_in",
    )(x, mods, g1, wp)


def _group_map(dims, tm):
    n_ctx_tiles = dims["t_ctx"] // tm
    per_lat = dims["l_lat"] // tm

    def gmap(i):
        return jnp.where(i < n_ctx_tiles, 0, 1 + (i - n_ctx_tiles) // per_lat)

    return gmap


def _gla_kernel(fblk, bblk, first, last, sidx, unit,
                qf, kf, vf, gf, qb, kb, vb, gb, wdec, bdec, s0,
                of, ob, sout, s_scr):
    del fblk, bblk, sidx, unit
    s = pl.program_id(1)

    @pl.when(first[s] == 1)
    def _():
        s_scr[...] = s0[0, :, 0]

    n_chunks = SLAB // GLA_CHUNK
    r = lax.broadcasted_iota(I32, (SLAB, SLAB), 0)
    c = lax.broadcasted_iota(I32, (SLAB, SLAB), 1)
    same = (r // GLA_CHUNK) == (c // GLA_CHUNK)
    ones_b = same.astype(BF16)

    dirs = ((qf, kf, vf, gf, of), (qb, kb, vb, gb, ob))
    for d, (q_ref, k_ref, v_ref, g_ref, o_ref) in enumerate(dirs):
        tri = jnp.logical_and(same, (c <= r) if d == 0 else (c >= r))
        tri_b = tri.astype(BF16)
        dec = _dot(g_ref[...], wdec[d, 0]) + bdec[d, 0]
        la = (jnp.minimum(dec, 0.0) - jnp.log1p(jnp.exp(-jnp.abs(dec)))) * (1.0 / GLA_TAU)
        hi, mid, lo = _split3(la)
        b = _dot(tri_b, hi) + _dot(tri_b, mid) + _dot(tri_b, lo)
        bl = _dot(ones_b, hi) + _dot(ones_b, mid) + _dot(ones_b, lo)
        q = q_ref[...].astype(F32) * (GLA_DK ** -0.5)
        k = k_ref[...].astype(F32)
        v = v_ref[...]
        qt = (q * jnp.exp(b)).astype(BF16)
        kt = (k * jnp.exp(-b)).astype(BF16)
        ke = (k * jnp.exp(bl - b)).astype(BF16)
        a = jnp.where(tri, _dot_nt(qt, kt), 0.0).astype(BF16)
        o_intra = _dot(a, v)

        def rows(x, ci):
            return x[ci * GLA_CHUNK:(ci + 1) * GLA_CHUNK]

        ds_t = [_dot_tn(rows(v, ci), rows(ke, ci)) for ci in range(n_chunks)]
        decay = [jnp.exp(bl[ci * GLA_CHUNK:ci * GLA_CHUNK + 1]) for ci in range(n_chunks)]
        sd = s_scr[d]
        o_inter = [None] * n_chunks
        order = range(n_chunks) if d == 0 else range(n_chunks - 1, -1, -1)
        for ci in order:
            o_inter[ci] = _dot_nt(rows(qt, ci), sd.astype(BF16))
            sd = sd * decay[ci] + ds_t[ci]
        o_ref[...] = (o_intra + jnp.concatenate(o_inter, axis=0)).astype(BF16)
        s_scr[d] = sd

        @pl.when(last[s] == 1)
        def _(sd=sd, d=d):
            sout[0, d, 0] = sd


def _gla_tables(dims):
    ctx_slabs = dims["l_ctx"] // SLAB
    lat_slabs = dims["l_lat"] // SLAB
    fblk, bblk, first, last, sidx, unit = [], [], [], [], [], []
    base = 0
    for u in range(dims["b_ctx"] + dims["b_lat"]):
        is_ctx = u < dims["b_ctx"]
        n = ctx_slabs if is_ctx else lat_slabs
        for j in range(n):
            fblk.append(base + j)
            bblk.append(base + n - 1 - j)
            first.append(int(j == 0))
            last.append(int(j == n - 1))
            sidx.append(0 if is_ctx else 1 + u - dims["b_ctx"])
            unit.append(u)
        base += n
    return [jnp.asarray(np.asarray(t, np.int32)) for t in (fblk, bblk, first, last, sidx, unit)]


def _gla(p, wdec, bdec, s0, tables, dims):
    T = p.shape[0]
    n_steps = T // SLAB
    n_units = dims["b_ctx"] + dims["b_lat"]

    def pspec(width, col0, which):
        cb = col0 // width
        if which == 0:
            return pl.BlockSpec((SLAB, width), lambda h, s, fb, bb, fi, la, si, un: (fb[s], cb + h))
        return pl.BlockSpec((SLAB, width), lambda h, s, fb, bb, fi, la, si, un: (bb[s], cb + h))

    def glow_spec(which):
        cb = C_GLOW // LANES
        if which == 0:
            return pl.BlockSpec((SLAB, LANES), lambda h, s, fb, bb, fi, la, si, un: (fb[s], cb))
        return pl.BlockSpec((SLAB, LANES), lambda h, s, fb, bb, fi, la, si, un: (bb[s], cb))

    in_specs = []
    for which in (0, 1):
        in_specs += [pspec(GLA_DK, C_Q, which), pspec(GLA_DK, C_K, which),
                     pspec(GLA_DV, C_V, which), glow_spec(which)]
    in_specs += [
        pl.BlockSpec((2, 1, LANES, GLA_DK), lambda h, s, *_: (0, h, 0, 0)),
        pl.BlockSpec((2, 1, 1, GLA_DK), lambda h, s, *_: (0, h, 0, 0)),
        pl.BlockSpec((1, 2, 1, GLA_DV, GLA_DK), lambda h, s, fb, bb, fi, la, si, un: (si[s], 0, h, 0, 0)),
    ]
    out_specs = [
        pl.BlockSpec((SLAB, GLA_DV), lambda h, s, fb, bb, fi, la, si, un: (fb[s], h)),
        pl.BlockSpec((SLAB, GLA_DV), lambda h, s, fb, bb, fi, la, si, un: (bb[s], h)),
        pl.BlockSpec((1, 2, 1, GLA_DV, GLA_DK), lambda h, s, fb, bb, fi, la, si, un: (un[s], 0, h, 0, 0)),
    ]
    grid_spec = pltpu.PrefetchScalarGridSpec(
        num_scalar_prefetch=6, grid=(GLA_HEADS, n_steps),
        in_specs=in_specs, out_specs=out_specs,
        scratch_shapes=[pltpu.VMEM((2, GLA_DV, GLA_DK), F32)])
    return pl.pallas_call(
        _gla_kernel,
        grid_spec=grid_spec,
        out_shape=[jax.ShapeDtypeStruct((T, GLA_HEADS * GLA_DV), BF16),
                   jax.ShapeDtypeStruct((T, GLA_HEADS * GLA_DV), BF16),
                   jax.ShapeDtypeStruct((n_units, 2, GLA_HEADS, GLA_DV, GLA_DK), F32)],
        compiler_params=_cparams(("parallel", "arbitrary"), 32 * 1024 * 1024),
        name="gla_bidirectional",
    )(*tables, p, p, p, p, p, p, p, p, wdec, bdec, s0)


def _pool_kernel(cur, prv, nxt, wp, sc, o_ref, *, n_ctx_slabs, ctx_slabs, lat_slabs):
    i = pl.program_id(0)
    is_ctx = i < n_ctx_slabs
    seq_slabs = jnp.where(is_ctx, ctx_slabs, lat_slabs)
    j = jnp.where(is_ctx, i % ctx_slabs, (i - n_ctx_slabs) % lat_slabs)
    has_prev = j > 0
    has_next = j < seq_slabs - 1
    seq_len = seq_slabs * SLAB
    r = lax.broadcasted_iota(I32, (SLAB, SLAB), 0)
    c = lax.broadcasted_iota(I32, (SLAB, SLAB), 1)
    t = j * SLAB + lax.broadcasted_iota(I32, (SLAB, 1), 0)
    gd = POOL_GROUP_DIM
    for g, w in enumerate(POOL_WINDOWS):
        lo_off, hi_off = w // 2, w - w // 2
        cols = slice(g * gd, (g + 1) * gd)
        u = cur[:, cols]
        b_cur = jnp.logical_and(c >= r - lo_off, c < r + hi_off)
        b_prv = jnp.logical_and(c - SLAB >= r - lo_off, has_prev)
        b_nxt = jnp.logical_and(c + SLAB < r + hi_off, has_next)
        ssum = (_dot(b_cur.astype(BF16), u) + _dot(b_prv.astype(BF16), prv[:, cols])
                + _dot(b_nxt.astype(BF16), nxt[:, cols]))
        cnt = (jnp.minimum(t + hi_off, seq_len) - jnp.maximum(t - lo_off, 0)).astype(F32)
        pooled = ssum / cnt - u.astype(F32)
        o_ref[:, cols] = (_dot(pooled.astype(BF16), wp[0, g]) * sc[0, :, cols]).astype(BF16)


def _pool(p, w_pool, pool_scale, l, dims):
    T = p.shape[0]
    n = T // SLAB
    cb = C_PIN // 1024
    kern = functools.partial(_pool_kernel, n_ctx_slabs=dims["t_ctx"] // SLAB,
                             ctx_slabs=dims["l_ctx"] // SLAB, lat_slabs=dims["l_lat"] // SLAB)
    return pl.pallas_call(
        kern,
        grid=(n,),
        in_specs=[
            pl.BlockSpec((SLAB, 1024), lambda i: (i, cb)),
            pl.BlockSpec((SLAB, 1024), lambda i: (jnp.maximum(i - 1, 0), cb)),
            pl.BlockSpec((SLAB, 1024), lambda i: (jnp.minimum(i + 1, n - 1), cb)),
            pl.BlockSpec((1, 4, 256, 256), lambda i: (l, 0, 0, 0)),
            pl.BlockSpec((1, 1, 1024), lambda i: (l, 0, 0)),
        ],
        out_specs=pl.BlockSpec((SLAB, 1024), lambda i: (i, 0)),
        out_shape=jax.ShapeDtypeStruct((T, 1024), BF16),
        compiler_params=_cparams(("parallel",), 32 * 1024 * 1024),
        name="pool_mixer",
    )(p, p, p, w_pool, pool_scale)


def _swap16(x):
    lane = lax.broadcasted_iota(I32, x.shape, x.ndim - 1)
    n = x.shape[-1]
    fwd = pltpu.roll(x, n - 16, x.ndim - 1)
    bwd = pltpu.roll(x, 16, x.ndim - 1)
    return jnp.where((lane % 32) < 16, fwd, bwd)


def _mla_prep_kernel(kva_ref, qa_ref, kr_ref, cos_ref, sin_ref, wq, wkv,
                     g_qa, g_kva, g_qn, g_qr, g_kn, g_kr,
                     qr_ref, qu_ref, k_ref, v_ref, ckv_ref, kro_ref):
    ckv = _rms(kva_ref[...].astype(F32), MLA_KV_LORA) * g_kva[0]
    ckv_ref[...] = ckv
    kv = _dot(ckv.astype(BF16), wkv[0])
    qn = _rms(qa_ref[...].astype(F32), MLA_Q_LORA) * g_qa[0]
    q = _dot(qn.astype(BF16), wq[0])
    cos = cos_ref[...]
    sin = sin_ref[...]
    kr = _rms(kr_ref[...].astype(F32), MLA_ROPE) * g_kr[0]
    kro_ref[...] = kr
    kr_rot = (kr * cos + _swap16(kr) * sin).astype(BF16)
    for h in range(MLA_HEADS):
        c0 = 2 * LANES * h
        q_nope = _rms(q[:, c0:c0 + LANES], MLA_NOPE) * g_qn[0] * MLA_SCALE
        q_rope = _rms(q[:, c0 + LANES:c0 + 2 * LANES], MLA_ROPE) * g_qr[0]
        q_rot = q_rope * cos + _swap16(q_rope) * sin
        qr_ref[:, c0:c0 + LANES] = q_nope.astype(BF16)
        qu_ref[:, c0:c0 + LANES] = q_nope.astype(BF16)
        qr_ref[:, c0 + LANES:c0 + 2 * LANES] = (q_rot * MLA_SCALE).astype(BF16)
        qu_ref[:, c0 + LANES:c0 + 2 * LANES] = (q_rope * MLA_SCALE).astype(BF16)
        k_nope = _rms(kv[:, c0:c0 + LANES], MLA_NOPE) * g_kn[0]
        k_ref[:, c0:c0 + LANES] = k_nope.astype(BF16)
        k_ref[:, c0 + LANES:c0 + 2 * LANES] = kr_rot
        v_ref[:, LANES * h:LANES * (h + 1)] = kv[:, c0 + LANES:c0 + 2 * LANES].astype(BF16)


def _mla_prep(p, cos, sin, wq, wkv, gains, l, dims):
    T = p.shape[0]
    tm = _pick_tile((512, 256), dims["t_ctx"], dims["l_lat"])
    hw = MLA_HEADS * 2 * LANES

    def gspec(n):
        return pl.BlockSpec((1, 1, n), lambda i: (l, 0, 0))

    return pl.pallas_call(
        _mla_prep_kernel,
        grid=(T // tm,),
        in_specs=[
            pl.BlockSpec((tm, MLA_KV_LORA), lambda i: (i, C_KVA // MLA_KV_LORA)),
            pl.BlockSpec((tm, MLA_Q_LORA), lambda i: (i, C_QA // MLA_Q_LORA)),
            pl.BlockSpec((tm, LANES), lambda i: (i, C_KR // LANES)),
            pl.BlockSpec((tm, LANES), lambda i: (i, 0)),
            pl.BlockSpec((tm, LANES), lambda i: (i, 0)),
            pl.BlockSpec((1, MLA_Q_LORA, hw), lambda i: (l, 0, 0)),
            pl.BlockSpec((1, MLA_KV_LORA, hw), lambda i: (l, 0, 0)),
            gspec(MLA_Q_LORA), gspec(MLA_KV_LORA), gspec(LANES), gspec(LANES), gspec(LANES), gspec(LANES),
        ],
        out_specs=[
            pl.BlockSpec((tm, hw), lambda i: (i, 0)),
            pl.BlockSpec((tm, hw), lambda i: (i, 0)),
            pl.BlockSpec((tm, hw), lambda i: (i, 0)),
            pl.BlockSpec((tm, MLA_HEADS * MLA_V), lambda i: (i, 0)),
            pl.BlockSpec((tm, MLA_KV_LORA), lambda i: (i, 0)),
            pl.BlockSpec((tm, LANES), lambda i: (i, 0)),
        ],
        out_shape=[
            jax.ShapeDtypeStruct((T, hw), BF16),
            jax.ShapeDtypeStruct((T, hw), BF16),
            jax.ShapeDtypeStruct((T, hw), BF16),
            jax.ShapeDtypeStruct((T, MLA_HEADS * MLA_V), BF16),
            jax.ShapeDtypeStruct((T, MLA_KV_LORA), F32),
            jax.ShapeDtypeStruct((T, LANES), F32),
        ],
        compiler_params=_cparams(("parallel",), 48 * 1024 * 1024),
        name="mla_prep",
    )(p, p, p, cos, sin, wq, wkv, *gains)


def _cache_kv_kernel(ckv_ref, kr_ref, wkv, g_kn, k_ref, v_ref):
    kv = _dot(ckv_ref[0, 0].astype(BF16), wkv[0])
    kr = kr_ref[0, 0].astype(BF16)
    for h in range(MLA_HEADS):
        c0 = 2 * LANES * h
        k_nope = _rms(kv[:, c0:c0 + LANES], MLA_NOPE) * g_kn[0]
        k_ref[0, 0, :, c0:c0 + LANES] = k_nope.astype(BF16)
        k_ref[0, 0, :, c0 + LANES:c0 + 2 * LANES] = kr
        v_ref[0, 0, :, LANES * h:LANES * (h + 1)] = kv[:, c0 + LANES:c0 + 2 * LANES].astype(BF16)


def _cache_kv(cache_ckv, cache_kr, wkv, g_kn):
    b_lat, _, past, _ = cache_ckv.shape
    hw = MLA_HEADS * 2 * LANES
    return pl.pallas_call(
        _cache_kv_kernel,
        grid=(DEPTH, b_lat),
        in_specs=[
            pl.BlockSpec((1, 1, past, MLA_KV_LORA), lambda l, b: (b, l, 0, 0)),
            pl.BlockSpec((1, 1, past, LANES), lambda l, b: (b, l, 0, 0)),
            pl.BlockSpec((1, MLA_KV_LORA, hw), lambda l, b: (l, 0, 0)),
            pl.BlockSpec((1, 1, LANES), lambda l, b: (l, 0, 0)),
        ],
        out_specs=[
            pl.BlockSpec((1, 1, past, hw), lambda l, b: (l, b, 0, 0)),
            pl.BlockSpec((1, 1, past, MLA_HEADS * MLA_V), lambda l, b: (l, b, 0, 0)),
        ],
        out_shape=[
            jax.ShapeDtypeStruct((DEPTH, b_lat, past, hw), BF16),
            jax.ShapeDtypeStruct((DEPTH, b_lat, past, MLA_HEADS * MLA_V), BF16),
        ],
        compiler_params=_cparams(("parallel", "parallel"), 32 * 1024 * 1024),
        name="mla_cache_decompress",
    )(cache_ckv, cache_kr, wkv, g_kn)


def _attn_ctx_kernel(q_ref, k_ref, v_ref, o_ref):
    s = _dot_nt(q_ref[...], k_ref[...])
    m = jnp.max(s, axis=-1, keepdims=True)
    pr = jnp.exp(s - m)
    den = jnp.sum(pr, axis=-1, keepdims=True)
    o_ref[...] = (_dot(pr.astype(BF16), v_ref[...]) / den).astype(BF16)


def _attn_ctx(q, k, v, dims):
    lc = dims["l_ctx"]
    t_ctx = dims["t_ctx"]
    return pl.pallas_call(
        _attn_ctx_kernel,
        grid=(dims["b_ctx"], MLA_HEADS),
        in_specs=[
            pl.BlockSpec((lc, 2 * LANES), lambda s, h: (s, h)),
            pl.BlockSpec((lc, 2 * LANES), lambda s, h: (s, h)),
            pl.BlockSpec((lc, MLA_V), lambda s, h: (s, h)),
        ],
        out_specs=pl.BlockSpec((lc, MLA_V), lambda s, h: (s, h)),
        out_shape=jax.ShapeDtypeStruct((t_ctx, MLA_HEADS * MLA_V), BF16),
        compiler_params=_cparams(("parallel", "parallel"), 32 * 1024 * 1024),
        name="mla_attention_context",
    )(q, k, v)


def _attn_lat_kernel(qr_ref, qu_ref, k_ref, v_ref, kc_ref, vc_ref, o_ref):
    s1 = _dot_nt(qr_ref[...], k_ref[...])
    s2 = _dot_nt(qu_ref[...], kc_ref[0, 0])
    m = jnp.maximum(jnp.max(s1, axis=-1, keepdims=True), jnp.max(s2, axis=-1, keepdims=True))
    p1 = jnp.exp(s1 - m)
    p2 = jnp.exp(s2 - m)
    den = jnp.sum(p1, axis=-1, keepdims=True) + jnp.sum(p2, axis=-1, keepdims=True)
    o = _dot(p1.astype(BF16), v_ref[...]) + _dot(p2.astype(BF16), vc_ref[0, 0])
    o_ref[...] = (o / den).astype(BF16)


def _attn_lat(qr, qu, k, v, kc, vc, l, dims):
    ll = dims["l_lat"]
    past = kc.shape[2]
    tq = _pick_tile((256,), ll)
    q0 = dims["t_ctx"] // tq
    k0 = dims["t_ctx"] // ll
    nq = ll // tq
    return pl.pallas_call(
        _attn_lat_kernel,
        grid=(dims["b_lat"], MLA_HEADS, nq),
        in_specs=[
            pl.BlockSpec((tq, 2 * LANES), lambda b, h, i: (q0 + b * nq + i, h)),
            pl.BlockSpec((tq, 2 * LANES), lambda b, h, i: (q0 + b * nq + i, h)),
            pl.BlockSpec((ll, 2 * LANES), lambda b, h, i: (k0 + b, h)),
            pl.BlockSpec((ll, MLA_V), lambda b, h, i: (k0 + b, h)),
            pl.BlockSpec((1, 1, past, 2 * LANES), lambda b, h, i: (l, b, 0, h)),
            pl.BlockSpec((1, 1, past, MLA_V), lambda b, h, i: (l, b, 0, h)),
        ],
        out_specs=pl.BlockSpec((tq, MLA_V), lambda b, h, i: (b * nq + i, h)),
        out_shape=jax.ShapeDtypeStruct((dims["b_lat"] * ll, MLA_HEADS * MLA_V), BF16),
        compiler_params=_cparams(("parallel", "parallel", "arbitrary"), 48 * 1024 * 1024),
        name="mla_attention_latent",
    )(qr, qu, k, v, kc, vc)


def _route(logits_t):
    ng, ne = N_GROUPS, EXPERTS_PER_GROUP
    lg = [logits_t[g:g + 1] for g in range(ng)]
    mg = functools.reduce(jnp.maximum, lg)
    zg = functools.reduce(lambda a, b: a + b, [jnp.exp(x - mg) for x in lg])
    pg_top = 1.0 / zg
    grp = jnp.full_like(mg, float(ng))
    for g in range(ng - 1, -1, -1):
        grp = jnp.where(lg[g] == mg, float(g), grp)
    el = []
    for j in range(ne):
        acc = jnp.zeros_like(mg)
        for g in range(ng):
            row = ng + g * ne + j
            acc = jnp.where(grp == float(g), logits_t[row:row + 1], acc)
        el.append(acc)
    m1 = functools.reduce(jnp.maximum, el)
    i1 = jnp.full_like(mg, float(ne))
    for j in range(ne - 1, -1, -1):
        i1 = jnp.where(el[j] == m1, float(j), i1)
    neg = jnp.full_like(mg, -jnp.inf)
    rest = [jnp.where(i1 == float(j), neg, el[j]) for j in range(ne)]
    m2 = functools.reduce(jnp.maximum, rest)
    i2 = jnp.full_like(mg, float(ne))
    for j in range(ne - 1, -1, -1):
        i2 = jnp.where(rest[j] == m2, float(j), i2)
    e2 = jnp.exp(m2 - m1)
    w1 = 1.0 / (1.0 + e2)
    w2 = e2 / (1.0 + e2)
    rows = [pg_top * (jnp.where(i1 == float(j), w1, 0.0) + jnp.where(i2 == float(j), w2, 0.0))
            for j in range(ne)]
    rows += [grp, jnp.zeros_like(mg), jnp.zeros_like(mg), jnp.zeros_like(mg)]
    return jnp.concatenate(rows, axis=0)


def _merge_kernel(x_ref, mod_ref, of_ref, ob_ref, gr_ref, pool_ref, attn_ref, gl0, gl1, gl2,
                  wbg, wbp, wbm, wo, g_gla, g2_ref, wr, br,
                  xo_ref, r_ref, a_scr, m_scr, *, tm):
    m = mod_ref[0, 0]
    gg = g_gla[0]
    for h in range(GLA_HEADS):
        cols = slice(h * GLA_DV, (h + 1) * GLA_DV)
        o = of_ref[:, cols].astype(F32) + ob_ref[:, cols].astype(F32)
        a_scr[:, cols] = (_rms(o, GLA_DV) * gg * _silu(gr_ref[:, cols].astype(F32))).astype(BF16)
    nb = 512
    for n in range(D // nb):
        cols = slice(n * nb, (n + 1) * nb)
        acc = jax.nn.sigmoid(gl0[:, cols].astype(F32)) * _dot(a_scr[...], wbg[0, :, cols])
        acc += jax.nn.sigmoid(gl1[:, cols].astype(F32)) * _dot(pool_ref[...], wbp[0, :, cols])
        acc += jax.nn.sigmoid(gl2[:, cols].astype(F32)) * _dot(attn_ref[...], wbm[0, :, cols])
        m_scr[:, cols] = acc.astype(BF16)
    xm = x_ref[...] + m[2:3] * _dot(m_scr[...], wo[0])
    xo_ref[...] = xm
    g2 = g2_ref[0]
    for ci in range(tm // LANES):
        rows = slice(ci * LANES, (ci + 1) * LANES)
        h2 = _rms(xm[rows], D) * g2 * (1.0 + m[4:5]) + m[3:4]
        hs = _split3(h2)
        lt = br[0]
        for ia, ib in ((0, 0), (0, 1), (1, 0), (0, 2), (2, 0), (1, 1)):
            lt = lt + _dot_nt(wr[0, ia], hs[ib])
        r_ref[:, rows] = _route(lt)


def _merge(x, mods, o_f, o_b, p, pool_out, attn, w, l, dims):
    T = x.shape[0]
    tm = _pick_tile((256,), dims["t_ctx"], dims["l_lat"])
    gmap = _group_map(dims, tm)

    def wspec(shape):
        return pl.BlockSpec((1,) + shape, lambda i: (l,) + (0,) * len(shape), pipeline_mode=pl.Buffered(1))

    br_tiled = w["router_b"]
    return pl.pallas_call(
        functools.partial(_merge_kernel, tm=tm),
        grid=(T // tm,),
        in_specs=[
            pl.BlockSpec((tm, D), lambda i: (i, 0)),
            pl.BlockSpec((1, 1, 6, D), lambda i: (l, gmap(i), 0, 0)),
            pl.BlockSpec((tm, 1024), lambda i: (i, 0)),
            pl.BlockSpec((tm, 1024), lambda i: (i, 0)),
            pl.BlockSpec((tm, 1024), lambda i: (i, C_GR // 1024)),
            pl.BlockSpec((tm, 1024), lambda i: (i, 0)),
            pl.BlockSpec((tm, 1024), lambda i: (i, 0)),
            pl.BlockSpec((tm, D), lambda i: (i, 0)),
            pl.BlockSpec((tm, D), lambda i: (i, 1)),
            pl.BlockSpec((tm, D), lambda i: (i, 2)),
            wspec((1024, D)), wspec((1024, D)), wspec((1024, D)), wspec((D, D)),
            pl.BlockSpec((1, 1, GLA_DV), lambda i: (l, 0, 0)),
            pl.BlockSpec((1, 1, D), lambda i: (l, 0, 0)),
            pl.BlockSpec((1, 3, 32, D), lambda i: (l, 0, 0, 0)),
            pl.BlockSpec((1, 32, LANES), lambda i: (l, 0, 0)),
        ],
        out_specs=[
            pl.BlockSpec((tm, D), lambda i: (i, 0)),
            pl.BlockSpec((8, tm), lambda i: (0, i)),
        ],
        out_shape=[jax.ShapeDtypeStruct((T, D), F32), jax.ShapeDtypeStruct((8, T), F32)],
        scratch_shapes=[pltpu.VMEM((tm, 1024), BF16), pltpu.VMEM((tm, D), BF16)],
        compiler_params=_cparams(("parallel",)),
        name="merge_out_route",
    )(x, mods, o_f, o_b, p, pool_out, attn, p, p, p,
      w["w_br_gla"], w["w_br_pool"], w["w_br_mla"], w["w_o"], w["g_gla"], w["g_norm2"],
      w["router_w"], br_tiled)


def _moe_kernel(tg, nvalid, src_ref, x_hbm, cm_ref, mod_ref, g2_ref, wg, wu, wd,
                o_hbm, xbuf, obuf, h_scr, gsem, ssem, *, tm, n_cond):
    del tg
    t = pl.program_id(0)
    n = nvalid[t]

    def gather_copy(r, row):
        return pltpu.make_async_copy(x_hbm.at[pl.ds(row, 1)], xbuf.at[pl.ds(r, 1)], gsem)

    def scatter_copy(r, row):
        return pltpu.make_async_copy(obuf.at[pl.ds(r, 1)], o_hbm.at[pl.ds(row, 1)], ssem)

    @pl.when(t == 0)
    def _():
        xbuf[...] = jnp.zeros_like(xbuf)

    @pl.when(n > 0)
    def _():
        def start_g(r, carry):
            gather_copy(r, src_ref[0, 0, r]).start()
            return carry

        lax.fori_loop(0, n, start_g, 0)

        def wait_g(r, carry):
            gather_copy(r, 0).wait()
            return carry

        lax.fori_loop(0, n, wait_g, 0)

        mods = mod_ref[0]
        g2 = g2_ref[0]

        def cond_rows(cg, k):
            out = mods[0, k:k + 1]
            for c in range(1, n_cond):
                out = jnp.where(cg == float(c), mods[c, k:k + 1], out)
            return out

        def norm_body(r, carry):
            rows = pl.ds(pl.multiple_of(r * LANES, LANES), LANES)
            cg = cm_ref[rows, 4:5]
            h2 = _rms(xbuf[rows, :], D) * g2 * (1.0 + cond_rows(cg, 4)) + cond_rows(cg, 3)
            h_scr[rows, :] = h2.astype(BF16)
            return carry

        lax.fori_loop(0, tm // LANES, norm_body, 0)

        hb = h_scr[...]
        y = None
        for e in range(EXPERTS_PER_GROUP):
            hid = _silu(_dot(hb, wg[0, e])) * _dot(hb, wu[0, e]) * cm_ref[:, e:e + 1]
            part = _dot(hid.astype(BF16), wd[0, e])
            y = part if y is None else y + part
        obuf[...] = y

        def out_body(r, carry):
            rows = pl.ds(pl.multiple_of(r * LANES, LANES), LANES)
            cg = cm_ref[rows, 4:5]
            obuf[rows, :] = xbuf[rows, :] + cond_rows(cg, 5) * obuf[rows, :]
            return carry

        lax.fori_loop(0, tm // LANES, out_body, 0)

        def start_s(r, carry):
            scatter_copy(r, src_ref[0, 0, r]).start()
            return carry

        lax.fori_loop(0, n, start_s, 0)

        def wait_s(r, carry):
            scatter_copy(r, 0).wait()
            return carry

        lax.fori_loop(0, n, wait_s, 0)


def _moe_plan(route, dims, tm):
    T = route.shape[1]
    nt = T // tm + N_GROUPS
    grp = route[4].astype(I32)
    order = jnp.argsort(grp, stable=True).astype(I32)
    counts = jnp.sum(grp[None, :] == jnp.arange(N_GROUPS, dtype=I32)[:, None], axis=1).astype(I32)
    tiles = (counts + tm - 1) // tm
    tile_end = jnp.cumsum(tiles)
    tile_start = tile_end - tiles
    tok_start = jnp.cumsum(counts) - counts
    tidx = jnp.arange(nt, dtype=I32)
    tg = jnp.minimum(jnp.sum(tidx[:, None] >= tile_end[None, :], axis=1), N_GROUPS - 1).astype(I32)
    used = tidx < tile_end[-1]
    in_group = (tidx - tile_start[tg]) * tm
    nvalid = jnp.where(used, jnp.clip(counts[tg] - in_group, 0, tm), 0).astype(I32)
    slot = jnp.arange(tm, dtype=I32)[None, :]
    valid = slot < nvalid[:, None]
    pos = jnp.clip(tok_start[tg][:, None] + in_group[:, None] + slot, 0, T - 1)
    src = jnp.where(valid, order[pos], 0).astype(I32)
    comb = jnp.where(valid[None], route[:4][:, src], 0.0)
    cond = jnp.where(src < dims["t_ctx"], 0, 1 + (src - dims["t_ctx"]) // dims["l_lat"]).astype(F32)
    cm = jnp.concatenate([comb, cond[None], jnp.zeros((3,) + cond.shape, F32)], axis=0)
    cm = cm.transpose(1, 2, 0).reshape(nt * tm, 8)
    return tg, nvalid, src.reshape(nt, 1, tm), cm


def _moe(x, route, mods, w, l, dims):
    T = x.shape[0]
    tm = 512 if T % 512 == 0 else 256
    nt = T // tm + N_GROUPS
    tg, nvalid, src, cm = _moe_plan(route, dims, tm)
    ff = EXPERT_FF

    def wspec(shape):
        return pl.BlockSpec((1, EXPERTS_PER_GROUP) + shape, lambda t, tg_, nv: (l, tg_[t], 0, 0),
                            pipeline_mode=pl.Buffered(1))

    grid_spec = pltpu.PrefetchScalarGridSpec(
        num_scalar_prefetch=2, grid=(nt,),
        in_specs=[
            pl.BlockSpec((1, 1, tm), lambda t, tg_, nv: (t, 0, 0), memory_space=pltpu.SMEM),
            pl.BlockSpec(memory_space=pl.ANY),
            pl.BlockSpec((tm, 8), lambda t, tg_, nv: (t, 0)),
            pl.BlockSpec((1, 8, 6, D), lambda t, tg_, nv: (l, 0, 0, 0)),
            pl.BlockSpec((1, 1, D), lambda t, tg_, nv: (l, 0, 0)),
            wspec((D, ff)), wspec((D, ff)), wspec((ff, D)),
        ],
        out_specs=pl.BlockSpec(memory_space=pl.ANY),
        scratch_shapes=[pltpu.VMEM((tm, D), F32), pltpu.VMEM((tm, D), F32), pltpu.VMEM((tm, D), BF16),
                        pltpu.SemaphoreType.DMA, pltpu.SemaphoreType.DMA])
    return pl.pallas_call(
        functools.partial(_moe_kernel, tm=tm, n_cond=1 + dims["b_lat"]),
        grid_spec=grid_spec,
        out_shape=jax.ShapeDtypeStruct((T, D), F32),
        compiler_params=_cparams(("arbitrary",)),
        name="moe_group_experts",
    )(tg, nvalid, src, x, cm, mods, w["g_norm2"], w["w_exp_gate"], w["w_exp_up"], w["w_exp_down"])


def _pack_params(w_in, w_gla_dec, b_gla_dec, w_mla_q_up, g_q_rope, g_k_rope,
                 w_group_router, b_group_router, w_expert_router, b_expert_router):
    gq, gk, gv, gr, glow, pin, qa, kva, kr, gl = jnp.split(
        w_in, [int(v) for v in np.cumsum(SPLIT_SIZES)[:-1]], axis=-1)

    def padc(a, n):
        return jnp.pad(a, ((0, 0), (0, 0), (0, n - a.shape[-1])))

    wp = jnp.concatenate([gl, gq, gk, gv, gr, pin, kva, qa, padc(glow, LANES), padc(kr, LANES)],
                         axis=-1).astype(BF16)
    wd = w_gla_dec.reshape(DEPTH, 2, GLA_RANK, GLA_HEADS, GLA_DK).transpose(0, 1, 3, 2, 4)
    wdec = jnp.zeros((DEPTH, 2, GLA_HEADS, LANES, GLA_DK), F32)
    for d in range(2):
        wdec = wdec.at[:, d, :, d * GLA_RANK:(d + 1) * GLA_RANK, :].set(wd[:, d])
    wdec = wdec.astype(BF16)
    bdec = b_gla_dec.reshape(DEPTH, 2, GLA_HEADS, 1, GLA_DK)
    wq = w_mla_q_up.reshape(DEPTH, MLA_Q_LORA, MLA_HEADS, MLA_NOPE + MLA_ROPE)
    wq = jnp.pad(wq, ((0, 0), (0, 0), (0, 0), (0, 2 * LANES - MLA_NOPE - MLA_ROPE)))
    wq = wq.reshape(DEPTH, MLA_Q_LORA, MLA_HEADS * 2 * LANES).astype(BF16)
    g_qr = jnp.pad(g_q_rope, ((0, 0), (0, LANES - MLA_ROPE))).reshape(DEPTH, 1, LANES)
    g_kr = jnp.pad(g_k_rope, ((0, 0), (0, LANES - MLA_ROPE))).reshape(DEPTH, 1, LANES)
    wr = jnp.concatenate([w_group_router, w_expert_router], axis=-1).transpose(0, 2, 1)
    wr = jnp.pad(wr, ((0, 0), (0, 32 - wr.shape[1]), (0, 0)))
    hi = wr.astype(BF16)
    r1 = wr - hi.astype(F32)
    mid = r1.astype(BF16)
    lo = (r1 - mid.astype(F32)).astype(BF16)
    router_w = jnp.stack([hi, mid, lo], axis=1)
    rb = jnp.concatenate([b_group_router, b_expert_router], axis=-1)
    rb = jnp.pad(rb, ((0, 0), (0, 32 - rb.shape[1])))
    router_b = jnp.broadcast_to(rb[:, :, None], (DEPTH, 32, LANES))
    return wp, wdec, bdec, wq, g_qr, g_kr, router_w, router_b


def _rope_tables(dims):
    ll = dims["l_lat"]
    t = jnp.arange(ll)
    row = (t // GRID_W).astype(F32)
    col = (t % GRID_W).astype(F32)
    n_freq = MLA_ROPE // 4
    inv = ROPE_THETA ** (-jnp.arange(n_freq, dtype=F32) / n_freq)
    ang = jnp.stack([row[:, None] * inv, col[:, None] * inv], axis=1)
    cos, sin = jnp.cos(ang), jnp.sin(ang)
    cos64 = jnp.concatenate([cos, cos], axis=-1).reshape(ll, MLA_ROPE)
    sin64 = jnp.concatenate([-sin, sin], axis=-1).reshape(ll, MLA_ROPE)
    pad = jnp.zeros((ll, LANES - MLA_ROPE), F32)
    cos_l = jnp.tile(jnp.concatenate([cos64, pad], axis=-1), (dims["b_lat"], 1))
    sin_l = jnp.tile(jnp.concatenate([sin64, pad], axis=-1), (dims["b_lat"], 1))
    cos_c = jnp.concatenate([jnp.ones((dims["t_ctx"], MLA_ROPE), F32),
                             jnp.zeros((dims["t_ctx"], LANES - MLA_ROPE), F32)], axis=-1)
    sin_c = jnp.zeros((dims["t_ctx"], LANES), F32)
    return jnp.concatenate([cos_c, cos_l], axis=0), jnp.concatenate([sin_c, sin_l], axis=0)


def kernel(x_prompt, x_sample, c, cache_mla_ckv, cache_mla_krope, state_gla, c_ctx, w_ada, b_ada, g_norm1, g_norm2, w_in, w_gla_dec, b_gla_dec, g_gla, w_pool, pool_scale, g_mla_qa, w_mla_q_up, g_mla_kva, w_mla_kv_up, g_q_nope, g_q_rope, g_k_nope, g_k_rope, w_br_gla, w_br_pool, w_br_mla, w_o, w_group_router, b_group_router, w_expert_router, b_expert_router, w_exp_gate, w_exp_up, w_exp_down):
    b_ctx, l_ctx, _ = x_prompt.shape
    b_lat, l_lat, _ = x_sample.shape
    dims = dict(b_ctx=b_ctx, l_ctx=l_ctx, b_lat=b_lat, l_lat=l_lat,
                t_ctx=b_ctx * l_ctx, t_lat=b_lat * l_lat)
    t_ctx = dims["t_ctx"]
    assert l_ctx % SLAB == 0 and l_lat % SLAB == 0 and t_ctx % l_lat == 0 and 1 + b_lat <= 8

    wp, wdec, bdec, wq, g_qr, g_kr, router_w, router_b = _pack_params(
        w_in, w_gla_dec, b_gla_dec, w_mla_q_up, g_q_rope, g_k_rope,
        w_group_router, b_group_router, w_expert_router, b_expert_router)
    wkv = w_mla_kv_up.astype(BF16)
    r3 = lambda a: a.reshape(DEPTH, 1, a.shape[-1])
    gains = (r3(g_mla_qa), r3(g_mla_kva), r3(g_q_nope), g_qr, r3(g_k_nope), g_kr)
    w = dict(w_br_gla=w_br_gla.astype(BF16), w_br_pool=w_br_pool.astype(BF16),
             w_br_mla=w_br_mla.astype(BF16), w_o=w_o.astype(BF16), g_gla=r3(g_gla),
             g_norm2=r3(g_norm2), router_w=router_w, router_b=router_b,
             w_exp_gate=w_exp_gate.astype(BF16), w_exp_up=w_exp_up.astype(BF16),
             w_exp_down=w_exp_down.astype(BF16))
    w_pool_b = w_pool.astype(BF16)
    pool_scale3 = r3(pool_scale)
    g1 = r3(g_norm1)

    cond = jnp.concatenate([c_ctx[None, :], c, jnp.zeros((8 - 1 - b_lat, D), F32)], axis=0)
    mods = _modulation(cond, w_ada, b_ada)

    cos, sin = _rope_tables(dims)
    cache_kr = jnp.pad(cache_mla_krope, ((0, 0), (0, 0), (0, 0), (0, LANES - MLA_ROPE)))
    kc, vc = _cache_kv(cache_mla_ckv, cache_kr, wkv, r3(g_k_nope))
    gla_tables = _gla_tables(dims)

    x = jnp.concatenate([x_prompt.reshape(t_ctx, D), x_sample.reshape(dims["t_lat"], D)], axis=0)
    ckv_l, kr_l, st_l = [], [], []
    for l in range(DEPTH):
        p = _proj_in(x, mods, g1, wp, l, dims)
        s0 = jnp.concatenate([jnp.zeros((1, 2, GLA_HEADS, GLA_DV, GLA_DK), F32),
                              jnp.swapaxes(state_gla[:, l], -1, -2)], axis=0)
        o_f, o_b, s_fin = _gla(p, wdec[l], bdec[l], s0, gla_tables, dims)
        pool_out = _pool(p, w_pool_b, pool_scale3, l, dims)
        qr, qu, k, v, ckv, kro = _mla_prep(p, cos, sin, wq, wkv, gains, l, dims)
        attn = jnp.concatenate([_attn_ctx(qr, k, v, dims),
                                _attn_lat(qr, qu, k, v, kc, vc, l, dims)], axis=0)
        x_mid, route = _merge(x, mods, o_f, o_b, p, pool_out, attn, w, l, dims)
        x = _moe(x_mid, route, mods, w, l, dims)
        ckv_l.append(ckv[:t_ctx].reshape(b_ctx, l_ctx, MLA_KV_LORA))
        kr_l.append(kro[:t_ctx, :MLA_ROPE].reshape(b_ctx, l_ctx, MLA_ROPE))
        st_l.append(jnp.swapaxes(s_fin[:b_ctx], -1, -2))

    y_prompt = x[:t_ctx].reshape(b_ctx, l_ctx, D)
    y_sample = x[t_ctx:].reshape(b_lat, l_lat, D)
    return (y_prompt, y_sample, jnp.stack(ckv_l, axis=1), jnp.stack(kr_l, axis=1),
            jnp.stack(st_l, axis=1))
```

```python
import functools
import math

import numpy as np
import jax
import jax.numpy as jnp
from jax import lax
from jax.experimental import pallas as pl
from jax.experimental.pallas import tpu as pltpu

F32 = jnp.float32
BF16 = jnp.bfloat16
I32 = jnp.int32

D = 2048
DEPTH = 4
EPS = 1e-6
GRID_W = 64
GLA_HEADS, GLA_DK, GLA_DV, GLA_RANK, GLA_TAU, GLA_CHUNK = 4, 128, 256, 16, 16.0, 64
POOL_WINDOWS = (2, 4, 8, 16)
POOL_GROUP_DIM = 256
MLA_HEADS, MLA_Q_LORA, MLA_KV_LORA, MLA_NOPE, MLA_ROPE, MLA_V = 8, 768, 512, 128, 64, 128
MLA_SCALE = 1.0 / math.sqrt(MLA_NOPE + MLA_ROPE)
ROPE_THETA = 10000.0
N_GROUPS, EXPERTS_PER_GROUP, EXPERT_FF = 4, 4, 512
SPLIT_SIZES = (512, 512, 1024, 1024, 32, 1024, 768, 512, 64, 6144)

LANES = 128
SLAB = 256
VMEM_LIMIT_BYTES = 60000 * 1024

C_GL = 0
C_Q = 6144
C_K = 6656
C_V = 7168
C_GR = 8192
C_PIN = 9216
C_KVA = 10240
C_QA = 10752
C_GLOW = 11520
C_KR = 11648
NP = 11776


def _dot(a, b):
    return jnp.dot(a, b, preferred_element_type=F32)


def _dot_nt(a, b):
    return lax.dot_general(a, b, (((1,), (1,)), ((), ())), preferred_element_type=F32)


def _dot_tn(a, b):
    return lax.dot_general(a, b, (((0,), (0,)), ((), ())), preferred_element_type=F32)


def _split3(x):
    hi = x.astype(BF16)
    r1 = x - hi.astype(F32)
    mid = r1.astype(BF16)
    lo = (r1 - mid.astype(F32)).astype(BF16)
    return hi, mid, lo


def _silu(x):
    return x * jax.nn.sigmoid(x)


def _rms(x, n):
    ms = jnp.sum(x * x, axis=-1, keepdims=True) * (1.0 / n)
    return x * lax.rsqrt(ms + EPS)


def _cparams(sem, vmem=VMEM_LIMIT_BYTES):
    return pltpu.CompilerParams(dimension_semantics=sem, vmem_limit_bytes=vmem)


def _pick_tile(cands, *extents):
    for c in cands:
        if all(e % c == 0 for e in extents):
            return c
    raise ValueError(f"no tile in {cands} divides {extents}")


def _mod_kernel(c_ref, w_ref, b_ref, o_ref):
    a = _silu(c_ref[...]).astype(BF16)
    o_ref[0] = _dot(a, w_ref[0].astype(BF16)) + b_ref[0]


def _modulation(cond, w_ada, b_ada):
    tn = 1024
    out = pl.pallas_call(
        _mod_kernel,
        grid=(DEPTH, 6 * D // tn),
        in_specs=[
            pl.BlockSpec((8, D), lambda l, j: (0, 0)),
            pl.BlockSpec((1, D, tn), lambda l, j: (l, 0, j)),
            pl.BlockSpec((1, 1, tn), lambda l, j: (l, 0, j)),
        ],
        out_specs=pl.BlockSpec((1, 8, tn), lambda l, j: (l, 0, j)),
        out_shape=jax.ShapeDtypeStruct((DEPTH, 8, 6 * D), F32),
        compiler_params=_cparams(("parallel", "parallel"), 40 * 1024 * 1024),
        name="adaln_modulation",
    )(cond, w_ada, b_ada.reshape(DEPTH, 1, 6 * D))
    return out.reshape(DEPTH, 8, 6, D)


def _proj_in_kernel(x_ref, mod_ref, g_ref, w_ref, o_ref, h_scr, *, tm):
    @pl.when(pl.program_id(1) == 0)
    def _():
        m = mod_ref[0, 0]
        g = g_ref[0]

        def body(r, carry):
            rows = pl.ds(pl.multiple_of(r * LANES, LANES), LANES)
            y = _rms(x_ref[rows, :], D) * g
            h_scr[rows, :] = (y * (1.0 + m[1:2]) + m[0:1]).astype(BF16)
            return carry

        lax.fori_loop(0, tm // LANES, body, 0)

    o_ref[...] = _dot(h_scr[...], w_ref[0]).astype(BF16)


def _proj_in(x, mods, g1, wp, l, dims):
    T = x.shape[0]
    tm = _pick_tile((1024, 512, 256), dims["t_ctx"], dims["l_lat"])
    tn = 512
    gmap = _group_map(dims, tm)
    return pl.pallas_call(
        functools.partial(_proj_in_kernel, tm=tm),
        grid=(T // tm, NP // tn),
        in_specs=[
            pl.BlockSpec((tm, D), lambda i, j: (i, 0)),
            pl.BlockSpec((1, 1, 6, D), lambda i, j: (l, gmap(i), 0, 0)),
            pl.BlockSpec((1, 1, D), lambda i, j: (l, 0, 0)),
            pl.BlockSpec((1, D, tn), lambda i, j: (l, 0, j)),
        ],
        out_specs=pl.BlockSpec((tm, tn), lambda i, j: (i, j)),
        out_shape=jax.ShapeDtypeStruct((T, NP), BF16),
        scratch_shapes=[pltpu.VMEM((tm, D), BF16)],
        compiler_params=_cparams(("parallel", "arbitrary")),
        name="norm1_proj_in",
    )(x, mods, g1, wp)


def _group_map(dims, tm):
    n_ctx_tiles = dims["t_ctx"] // tm
    per_lat = dims["l_lat"] // tm

    def gmap(i):
        return jnp.where(i < n_ctx_tiles, 0, 1 + (i - n_ctx_tiles) // per_lat)

    return gmap


def _gla_kernel(fblk, bblk, first, last, sidx, unit,
                qf, kf, vf, gf, qb, kb, vb, gb, wdec, bdec, s0,
                of, ob, sout, s_scr):
    del fblk, bblk, sidx, unit
    s = pl.program_id(0)

    @pl.when(first[s] == 1)
    def _():
        s_scr[...] = s0[0]

    n_chunks = SLAB // GLA_CHUNK
    r = lax.broadcasted_iota(I32, (SLAB, SLAB), 0)
    c = lax.broadcasted_iota(I32, (SLAB, SLAB), 1)
    same = (r // GLA_CHUNK) == (c // GLA_CHUNK)

    def rows(x, ci):
        return x[ci * GLA_CHUNK:(ci + 1) * GLA_CHUNK]

    dirs = ((qf, kf, vf, gf, of), (qb, kb, vb, gb, ob))
    for d, (q_ref, k_ref, v_ref, g_ref, o_ref) in enumerate(dirs):
        tri = jnp.logical_and(same, (c <= r) if d == 0 else (c >= r))
        tri_b = tri.astype(BF16)
        dec = _dot(g_ref[...], wdec[d]) + bdec[d]
        la = (jnp.minimum(dec, 0.0) - jnp.log1p(jnp.exp(-jnp.abs(dec)))) * (1.0 / GLA_TAU)
        hi, mid, lo = _split3(la)
        b_all = _dot(tri_b, hi) + _dot(tri_b, mid) + _dot(tri_b, lo)
        edge = GLA_CHUNK - 1 if d == 0 else 0
        tot = [b_all[ci * GLA_CHUNK + edge:ci * GLA_CHUNK + edge + 1] for ci in range(n_chunks)]
        bl_all = jnp.concatenate([jnp.broadcast_to(t, (GLA_CHUNK, t.shape[1])) for t in tot], axis=0)
        order = range(n_chunks) if d == 0 else range(n_chunks - 1, -1, -1)
        for h in range(GLA_HEADS):
            kc = slice(h * GLA_DK, (h + 1) * GLA_DK)
            vc = slice(h * GLA_DV, (h + 1) * GLA_DV)
            b = b_all[:, kc]
            bl = bl_all[:, kc]
            q = q_ref[:, kc].astype(F32) * (GLA_DK ** -0.5)
            k = k_ref[:, kc].astype(F32)
            v = v_ref[:, vc]
            qt = (q * jnp.exp(b)).astype(BF16)
            kt = (k * jnp.exp(-b)).astype(BF16)
            ke = (k * jnp.exp(bl - b)).astype(BF16)
            a = jnp.where(tri, _dot_nt(qt, kt), 0.0).astype(BF16)
            o_intra = _dot(a, v)
            ds_t = [_dot_tn(rows(v, ci), rows(ke, ci)) for ci in range(n_chunks)]
            decay = [jnp.exp(tot[ci][:, kc]) for ci in range(n_chunks)]
            sd = s_scr[d, h]
            o_inter = [None] * n_chunks
            for ci in order:
                o_inter[ci] = _dot_nt(rows(qt, ci), sd.astype(BF16))
                sd = sd * decay[ci] + ds_t[ci]
            o_ref[:, vc] = (o_intra + jnp.concatenate(o_inter, axis=0)).astype(BF16)
            s_scr[d, h] = sd

            @pl.when(last[s] == 1)
            def _(sd=sd, d=d, h=h):
                sout[0, d, h] = sd


def _gla_tables(dims):
    ctx_slabs = dims["l_ctx"] // SLAB
    lat_slabs = dims["l_lat"] // SLAB
    fblk, bblk, first, last, sidx, unit = [], [], [], [], [], []
    base = 0
    for u in range(dims["b_ctx"] + dims["b_lat"]):
        is_ctx = u < dims["b_ctx"]
        n = ctx_slabs if is_ctx else lat_slabs
        for j in range(n):
            fblk.append(base + j)
            bblk.append(base + n - 1 - j)
            first.append(int(j == 0))
            last.append(int(j == n - 1))
            sidx.append(0 if is_ctx else 1 + u - dims["b_ctx"])
            unit.append(u)
        base += n
    return [jnp.asarray(np.asarray(t, np.int32)) for t in (fblk, bblk, first, last, sidx, unit)]


def _gla(p, wdec, bdec, s0, tables, dims):
    T = p.shape[0]
    n_steps = T // SLAB
    n_units = dims["b_ctx"] + dims["b_lat"]

    hk, hv = GLA_HEADS * GLA_DK, GLA_HEADS * GLA_DV
    state = (2, GLA_HEADS, GLA_DV, GLA_DK)

    def pspec(width, col0, which):
        cb = col0 // width
        if which == 0:
            return pl.BlockSpec((SLAB, width), lambda s, fb, bb, fi, la, si, un: (fb[s], cb))
        return pl.BlockSpec((SLAB, width), lambda s, fb, bb, fi, la, si, un: (bb[s], cb))

    in_specs = []
    for which in (0, 1):
        in_specs += [pspec(hk, C_Q, which), pspec(hk, C_K, which), pspec(hv, C_V, which),
                     pspec(LANES, C_GLOW, which)]
    in_specs += [
        pl.BlockSpec((2, LANES, hk), lambda s, *_: (0, 0, 0)),
        pl.BlockSpec((2, 1, hk), lambda s, *_: (0, 0, 0)),
        pl.BlockSpec((1,) + state, lambda s, fb, bb, fi, la, si, un: (si[s], 0, 0, 0, 0)),
    ]
    out_specs = [
        pl.BlockSpec((SLAB, hv), lambda s, fb, bb, fi, la, si, un: (fb[s], 0)),
        pl.BlockSpec((SLAB, hv), lambda s, fb, bb, fi, la, si, un: (bb[s], 0)),
        pl.BlockSpec((1,) + state, lambda s, fb, bb, fi, la, si, un: (un[s], 0, 0, 0, 0)),
    ]
    grid_spec = pltpu.PrefetchScalarGridSpec(
        num_scalar_prefetch=6, grid=(n_steps,),
        in_specs=in_specs, out_specs=out_specs,
        scratch_shapes=[pltpu.VMEM(state, F32)])
    return pl.pallas_call(
        _gla_kernel,
        grid_spec=grid_spec,
        out_shape=[jax.ShapeDtypeStruct((T, hv), BF16),
                   jax.ShapeDtypeStruct((T, hv), BF16),
                   jax.ShapeDtypeStruct((n_units,) + state, F32)],
        compiler_params=_cparams(("arbitrary",), 32 * 1024 * 1024),
        name="gla_bidirectional",
    )(*tables, p, p, p, p, p, p, p, p, wdec, bdec, s0)


def _pool_kernel(cur, prv, nxt, wp, sc, o_ref, *, n_ctx_slabs, ctx_slabs, lat_slabs):
    i = pl.program_id(0)
    is_ctx = i < n_ctx_slabs
    seq_slabs = jnp.where(is_ctx, ctx_slabs, lat_slabs)
    j = jnp.where(is_ctx, i % ctx_slabs, (i - n_ctx_slabs) % lat_slabs)
    has_prev = j > 0
    has_next = j < seq_slabs - 1
    seq_len = seq_slabs * SLAB
    r = lax.broadcasted_iota(I32, (SLAB, SLAB), 0)
    c = lax.broadcasted_iota(I32, (SLAB, SLAB), 1)
    t = j * SLAB + lax.broadcasted_iota(I32, (SLAB, 1), 0)
    gd = POOL_GROUP_DIM
    for g, w in enumerate(POOL_WINDOWS):
        lo_off, hi_off = w // 2, w - w // 2
        cols = slice(g * gd, (g + 1) * gd)
        u = cur[:, cols]
        b_cur = jnp.logical_and(c >= r - lo_off, c < r + hi_off)
        b_prv = jnp.logical_and(c - SLAB >= r - lo_off, has_prev)
        b_nxt = jnp.logical_and(c + SLAB < r + hi_off, has_next)
        ssum = (_dot(b_cur.astype(BF16), u) + _dot(b_prv.astype(BF16), prv[:, cols])
                + _dot(b_nxt.astype(BF16), nxt[:, cols]))
        cnt = (jnp.minimum(t + hi_off, seq_len) - jnp.maximum(t - lo_off, 0)).astype(F32)
        pooled = ssum / cnt - u.astype(F32)
        o_ref[:, cols] = (_dot(pooled.astype(BF16), wp[0, g]) * sc[0, :, cols]).astype(BF16)


def _pool(p, w_pool, pool_scale, l, dims):
    T = p.shape[0]
    n = T // SLAB
    cb = C_PIN // 1024
    kern = functools.partial(_pool_kernel, n_ctx_slabs=dims["t_ctx"] // SLAB,
                             ctx_slabs=dims["l_ctx"] // SLAB, lat_slabs=dims["l_lat"] // SLAB)
    return pl.pallas_call(
        kern,
        grid=(n,),
        in_specs=[
            pl.BlockSpec((SLAB, 1024), lambda i: (i, cb)),
            pl.BlockSpec((SLAB, 1024), lambda i: (jnp.maximum(i - 1, 0), cb)),
            pl.BlockSpec((SLAB, 1024), lambda i: (jnp.minimum(i + 1, n - 1), cb)),
            pl.BlockSpec((1, 4, 256, 256), lambda i: (l, 0, 0, 0)),
            pl.BlockSpec((1, 1, 1024), lambda i: (l, 0, 0)),
        ],
        out_specs=pl.BlockSpec((SLAB, 1024), lambda i: (i, 0)),
        out_shape=jax.ShapeDtypeStruct((T, 1024), BF16),
        compiler_params=_cparams(("parallel",), 32 * 1024 * 1024),
        name="pool_mixer",
    )(p, p, p, w_pool, pool_scale)


def _swap16(x):
    lane = lax.broadcasted_iota(I32, x.shape, x.ndim - 1)
    n = x.shape[-1]
    fwd = pltpu.roll(x, n - 16, x.ndim - 1)
    bwd = pltpu.roll(x, 16, x.ndim - 1)
    return jnp.where((lane % 32) < 16, fwd, bwd)


def _mla_prep_kernel(kva_ref, qa_ref, kr_ref, cos_ref, sin_ref, wq, wkv,
                     g_qa, g_kva, g_qn, g_qr, g_kn, g_kr,
                     qr_ref, qu_ref, k_ref, v_ref, ckv_ref, kro_ref):
    ckv = _rms(kva_ref[...].astype(F32), MLA_KV_LORA) * g_kva[0]
    ckv_ref[...] = ckv
    kv = _dot(ckv.astype(BF16), wkv[0])
    qn = _rms(qa_ref[...].astype(F32), MLA_Q_LORA) * g_qa[0]
    q = _dot(qn.astype(BF16), wq[0])
    cos = cos_ref[...]
    sin = sin_ref[...]
    kr = _rms(kr_ref[...].astype(F32), MLA_ROPE) * g_kr[0]
    kro_ref[...] = kr
    kr_rot = (kr * cos + _swap16(kr) * sin).astype(BF16)
    for h in range(MLA_HEADS):
        c0 = 2 * LANES * h
        q_nope = _rms(q[:, c0:c0 + LANES], MLA_NOPE) * g_qn[0] * MLA_SCALE
        q_rope = _rms(q[:, c0 + LANES:c0 + 2 * LANES], MLA_ROPE) * g_qr[0]
        q_rot = q_rope * cos + _swap16(q_rope) * sin
        qr_ref[:, c0:c0 + LANES] = q_nope.astype(BF16)
        qu_ref[:, c0:c0 + LANES] = q_nope.astype(BF16)
        qr_ref[:, c0 + LANES:c0 + 2 * LANES] = (q_rot * MLA_SCALE).astype(BF16)
        qu_ref[:, c0 + LANES:c0 + 2 * LANES] = (q_rope * MLA_SCALE).astype(BF16)
        k_nope = _rms(kv[:, c0:c0 + LANES], MLA_NOPE) * g_kn[0]
        k_ref[:, c0:c0 + LANES] = k_nope.astype(BF16)
        k_ref[:, c0 + LANES:c0 + 2 * LANES] = kr_rot
        v_ref[:, LANES * h:LANES * (h + 1)] = kv[:, c0 + LANES:c0 + 2 * LANES].astype(BF16)


def _mla_prep(p, cos, sin, wq, wkv, gains, l, dims):
    T = p.shape[0]
    tm = _pick_tile((512, 256), dims["t_ctx"], dims["l_lat"])
    hw = MLA_HEADS * 2 * LANES

    def gspec(n):
        return pl.BlockSpec((1, 1, n), lambda i: (l, 0, 0))

    return pl.pallas_call(
        _mla_prep_kernel,
        grid=(T // tm,),
        in_specs=[
            pl.BlockSpec((tm, MLA_KV_LORA), lambda i: (i, C_KVA // MLA_KV_LORA)),
            pl.BlockSpec((tm, MLA_Q_LORA), lambda i: (i, C_QA // MLA_Q_LORA)),
            pl.BlockSpec((tm, LANES), lambda i: (i, C_KR // LANES)),
            pl.BlockSpec((tm, LANES), lambda i: (i, 0)),
            pl.BlockSpec((tm, LANES), lambda i: (i, 0)),
            pl.BlockSpec((1, MLA_Q_LORA, hw), lambda i: (l, 0, 0)),
            pl.BlockSpec((1, MLA_KV_LORA, hw), lambda i: (l, 0, 0)),
            gspec(MLA_Q_LORA), gspec(MLA_KV_LORA), gspec(LANES), gspec(LANES), gspec(LANES), gspec(LANES),
        ],
        out_specs=[
            pl.BlockSpec((tm, hw), lambda i: (i, 0)),
            pl.BlockSpec((tm, hw), lambda i: (i, 0)),
            pl.BlockSpec((tm, hw), lambda i: (i, 0)),
            pl.BlockSpec((tm, MLA_HEADS * MLA_V), lambda i: (i, 0)),
            pl.BlockSpec((tm, MLA_KV_LORA), lambda i: (i, 0)),
            pl.BlockSpec((tm, LANES), lambda i: (i, 0)),
        ],
        out_shape=[
            jax.ShapeDtypeStruct((T, hw), BF16),
            jax.ShapeDtypeStruct((T, hw), BF16),
            jax.ShapeDtypeStruct((T, hw), BF16),
            jax.ShapeDtypeStruct((T, MLA_HEADS * MLA_V), BF16),
            jax.ShapeDtypeStruct((T, MLA_KV_LORA), F32),
            jax.ShapeDtypeStruct((T, LANES), F32),
        ],
        compiler_params=_cparams(("parallel",), 48 * 1024 * 1024),
        name="mla_prep",
    )(p, p, p, cos, sin, wq, wkv, *gains)


def _cache_kv_kernel(ckv_ref, kr_ref, wkv, g_kn, k_ref, v_ref):
    kv = _dot(ckv_ref[0, 0].astype(BF16), wkv[0])
    kr = kr_ref[0, 0].astype(BF16)
    for h in range(MLA_HEADS):
        c0 = 2 * LANES * h
        k_nope = _rms(kv[:, c0:c0 + LANES], MLA_NOPE) * g_kn[0]
        k_ref[0, 0, :, c0:c0 + LANES] = k_nope.astype(BF16)
        k_ref[0, 0, :, c0 + LANES:c0 + 2 * LANES] = kr
        v_ref[0, 0, :, LANES * h:LANES * (h + 1)] = kv[:, c0 + LANES:c0 + 2 * LANES].astype(BF16)


def _cache_kv(cache_ckv, cache_kr, wkv, g_kn):
    b_lat, _, past, _ = cache_ckv.shape
    hw = MLA_HEADS * 2 * LANES
    return pl.pallas_call(
        _cache_kv_kernel,
        grid=(DEPTH, b_lat),
        in_specs=[
            pl.BlockSpec((1, 1, past, MLA_KV_LORA), lambda l, b: (b, l, 0, 0)),
            pl.BlockSpec((1, 1, past, LANES), lambda l, b: (b, l, 0, 0)),
            pl.BlockSpec((1, MLA_KV_LORA, hw), lambda l, b: (l, 0, 0)),
            pl.BlockSpec((1, 1, LANES), lambda l, b: (l, 0, 0)),
        ],
        out_specs=[
            pl.BlockSpec((1, 1, past, hw), lambda l, b: (l, b, 0, 0)),
            pl.BlockSpec((1, 1, past, MLA_HEADS * MLA_V), lambda l, b: (l, b, 0, 0)),
        ],
        out_shape=[
            jax.ShapeDtypeStruct((DEPTH, b_lat, past, hw), BF16),
            jax.ShapeDtypeStruct((DEPTH, b_lat, past, MLA_HEADS * MLA_V), BF16),
        ],
        compiler_params=_cparams(("parallel", "parallel"), 32 * 1024 * 1024),
        name="mla_cache_decompress",
    )(cache_ckv, cache_kr, wkv, g_kn)


def _attn_ctx_kernel(q_ref, k_ref, v_ref, o_ref):
    for h in range(MLA_HEADS):
        qk = slice(h * 2 * LANES, (h + 1) * 2 * LANES)
        vs = slice(h * MLA_V, (h + 1) * MLA_V)
        s = _dot_nt(q_ref[:, qk], k_ref[:, qk])
        m = jnp.max(s, axis=-1, keepdims=True)
        pr = jnp.exp(s - m)
        den = jnp.sum(pr, axis=-1, keepdims=True)
        o_ref[:, vs] = (_dot(pr.astype(BF16), v_ref[:, vs]) / den).astype(BF16)


def _attn_ctx(q, k, v, dims):
    lc = dims["l_ctx"]
    t_ctx = dims["t_ctx"]
    hw = MLA_HEADS * 2 * LANES
    return pl.pallas_call(
        _attn_ctx_kernel,
        grid=(dims["b_ctx"],),
        in_specs=[
            pl.BlockSpec((lc, hw), lambda s: (s, 0)),
            pl.BlockSpec((lc, hw), lambda s: (s, 0)),
            pl.BlockSpec((lc, MLA_HEADS * MLA_V), lambda s: (s, 0)),
        ],
        out_specs=pl.BlockSpec((lc, MLA_HEADS * MLA_V), lambda s: (s, 0)),
        out_shape=jax.ShapeDtypeStruct((t_ctx, MLA_HEADS * MLA_V), BF16),
        compiler_params=_cparams(("parallel",), 32 * 1024 * 1024),
        name="mla_attention_context",
    )(q, k, v)


ATTN_HEADS_PER_STEP = 2


def _attn_lat_kernel(qr_ref, qu_ref, k_ref, v_ref, kc_ref, vc_ref, o_ref):
    for h in range(ATTN_HEADS_PER_STEP):
        qk = slice(h * 2 * LANES, (h + 1) * 2 * LANES)
        vs = slice(h * MLA_V, (h + 1) * MLA_V)
        s1 = _dot_nt(qr_ref[:, qk], k_ref[:, qk])
        s2 = _dot_nt(qu_ref[:, qk], kc_ref[0, 0, :, qk])
        m = jnp.maximum(jnp.max(s1, axis=-1, keepdims=True), jnp.max(s2, axis=-1, keepdims=True))
        p1 = jnp.exp(s1 - m)
        p2 = jnp.exp(s2 - m)
        den = jnp.sum(p1, axis=-1, keepdims=True) + jnp.sum(p2, axis=-1, keepdims=True)
        o = _dot(p1.astype(BF16), v_ref[:, vs]) + _dot(p2.astype(BF16), vc_ref[0, 0, :, vs])
        o_ref[:, vs] = (o / den).astype(BF16)


def _attn_lat(qr, qu, k, v, kc, vc, l, dims):
    ll = dims["l_lat"]
    past = kc.shape[2]
    tq = _pick_tile((256,), ll)
    q0 = dims["t_ctx"] // tq
    k0 = dims["t_ctx"] // ll
    nq = ll // tq
    hp = ATTN_HEADS_PER_STEP
    qw, vw = hp * 2 * LANES, hp * MLA_V
    return pl.pallas_call(
        _attn_lat_kernel,
        grid=(dims["b_lat"], MLA_HEADS // hp, nq),
        in_specs=[
            pl.BlockSpec((tq, qw), lambda b, h, i: (q0 + b * nq + i, h)),
            pl.BlockSpec((tq, qw), lambda b, h, i: (q0 + b * nq + i, h)),
            pl.BlockSpec((ll, qw), lambda b, h, i: (k0 + b, h)),
            pl.BlockSpec((ll, vw), lambda b, h, i: (k0 + b, h)),
            pl.BlockSpec((1, 1, past, qw), lambda b, h, i: (l, b, 0, h)),
            pl.BlockSpec((1, 1, past, vw), lambda b, h, i: (l, b, 0, h)),
        ],
        out_specs=pl.BlockSpec((tq, vw), lambda b, h, i: (b * nq + i, h)),
        out_shape=jax.ShapeDtypeStruct((dims["b_lat"] * ll, MLA_HEADS * MLA_V), BF16),
        compiler_params=_cparams(("parallel", "parallel", "arbitrary"), 48 * 1024 * 1024),
        name="mla_attention_latent",
    )(qr, qu, k, v, kc, vc)


def _route(logits_t):
    ng, ne = N_GROUPS, EXPERTS_PER_GROUP
    lg = [logits_t[g:g + 1] for g in range(ng)]
    mg = functools.reduce(jnp.maximum, lg)
    zg = functools.reduce(lambda a, b: a + b, [jnp.exp(x - mg) for x in lg])
    pg_top = 1.0 / zg
    grp = jnp.full_like(mg, float(ng))
    for g in range(ng - 1, -1, -1):
        grp = jnp.where(lg[g] == mg, float(g), grp)
    el = []
    for j in range(ne):
        acc = jnp.zeros_like(mg)
        for g in range(ng):
            row = ng + g * ne + j
            acc = jnp.where(grp == float(g), logits_t[row:row + 1], acc)
        el.append(acc)
    m1 = functools.reduce(jnp.maximum, el)
    i1 = jnp.full_like(mg, float(ne))
    for j in range(ne - 1, -1, -1):
        i1 = jnp.where(el[j] == m1, float(j), i1)
    neg = jnp.full_like(mg, -jnp.inf)
    rest = [jnp.where(i1 == float(j), neg, el[j]) for j in range(ne)]
    m2 = functools.reduce(jnp.maximum, rest)
    i2 = jnp.full_like(mg, float(ne))
    for j in range(ne - 1, -1, -1):
        i2 = jnp.where(rest[j] == m2, float(j), i2)
    e2 = jnp.exp(m2 - m1)
    w1 = 1.0 / (1.0 + e2)
    w2 = e2 / (1.0 + e2)
    rows = [pg_top * (jnp.where(i1 == float(j), w1, 0.0) + jnp.where(i2 == float(j), w2, 0.0))
            for j in range(ne)]
    rows += [grp, jnp.zeros_like(mg), jnp.zeros_like(mg), jnp.zeros_like(mg)]
    return jnp.concatenate(rows, axis=0)


def _merge_kernel(x_ref, mod_ref, of_ref, ob_ref, gr_ref, pool_ref, attn_ref, gl0, gl1, gl2,
                  wbg, wbp, wbm, wo, g_gla, g2_ref, wr, br,
                  xo_ref, r_ref, a_scr, m_scr, *, tm):
    m = mod_ref[0, 0]
    gg = g_gla[0]
    for h in range(GLA_HEADS):
        cols = slice(h * GLA_DV, (h + 1) * GLA_DV)
        o = of_ref[:, cols].astype(F32) + ob_ref[:, cols].astype(F32)
        a_scr[:, cols] = (_rms(o, GLA_DV) * gg * _silu(gr_ref[:, cols].astype(F32))).astype(BF16)
    nb = 512
    for n in range(D // nb):
        cols = slice(n * nb, (n + 1) * nb)
        acc = jax.nn.sigmoid(gl0[:, cols].astype(F32)) * _dot(a_scr[...], wbg[0, :, cols])
        acc += jax.nn.sigmoid(gl1[:, cols].astype(F32)) * _dot(pool_ref[...], wbp[0, :, cols])
        acc += jax.nn.sigmoid(gl2[:, cols].astype(F32)) * _dot(attn_ref[...], wbm[0, :, cols])
        m_scr[:, cols] = acc.astype(BF16)
    xm = x_ref[...] + m[2:3] * _dot(m_scr[...], wo[0])
    xo_ref[...] = xm
    g2 = g2_ref[0]
    for ci in range(tm // LANES):
        rows = slice(ci * LANES, (ci + 1) * LANES)
        h2 = _rms(xm[rows], D) * g2 * (1.0 + m[4:5]) + m[3:4]
        hs = _split3(h2)
        lt = br[0]
        for ia, ib in ((0, 0), (0, 1), (1, 0), (0, 2), (2, 0), (1, 1)):
            lt = lt + _dot_nt(wr[0, ia], hs[ib])
        r_ref[:, rows] = _route(lt)


def _merge(x, mods, o_f, o_b, p, pool_out, attn, w, l, dims):
    T = x.shape[0]
    tm = _pick_tile((256,), dims["t_ctx"], dims["l_lat"])
    gmap = _group_map(dims, tm)

    def wspec(shape):
        return pl.BlockSpec((1,) + shape, lambda i: (l,) + (0,) * len(shape), pipeline_mode=pl.Buffered(1))

    br_tiled = w["router_b"]
    return pl.pallas_call(
        functools.partial(_merge_kernel, tm=tm),
        grid=(T // tm,),
        in_specs=[
            pl.BlockSpec((tm, D), lambda i: (i, 0)),
            pl.BlockSpec((1, 1, 6, D), lambda i: (l, gmap(i), 0, 0)),
            pl.BlockSpec((tm, 1024), lambda i: (i, 0)),
            pl.BlockSpec((tm, 1024), lambda i: (i, 0)),
            pl.BlockSpec((tm, 1024), lambda i: (i, C_GR // 1024)),
            pl.BlockSpec((tm, 1024), lambda i: (i, 0)),
            pl.BlockSpec((tm, 1024), lambda i: (i, 0)),
            pl.BlockSpec((tm, D), lambda i: (i, 0)),
            pl.BlockSpec((tm, D), lambda i: (i, 1)),
            pl.BlockSpec((tm, D), lambda i: (i, 2)),
            wspec((1024, D)), wspec((1024, D)), wspec((1024, D)), wspec((D, D)),
            pl.BlockSpec((1, 1, GLA_DV), lambda i: (l, 0, 0)),
            pl.BlockSpec((1, 1, D), lambda i: (l, 0, 0)),
            pl.BlockSpec((1, 3, 32, D), lambda i: (l, 0, 0, 0)),
            pl.BlockSpec((1, 32, LANES), lambda i: (l, 0, 0)),
        ],
        out_specs=[
            pl.BlockSpec((tm, D), lambda i: (i, 0)),
            pl.BlockSpec((8, tm), lambda i: (0, i)),
        ],
        out_shape=[jax.ShapeDtypeStruct((T, D), F32), jax.ShapeDtypeStruct((8, T), F32)],
        scratch_shapes=[pltpu.VMEM((tm, 1024), BF16), pltpu.VMEM((tm, D), BF16)],
        compiler_params=_cparams(("parallel",)),
        name="merge_out_route",
    )(x, mods, o_f, o_b, p, pool_out, attn, p, p, p,
      w["w_br_gla"], w["w_br_pool"], w["w_br_mla"], w["w_o"], w["g_gla"], w["g_norm2"],
      w["router_w"], br_tiled)


def _moe_kernel(tg, nvalid, src_ref, nsrc_ref, x_hbm, cm_ref, mod_ref, g2_ref, wg, wu, wd,
                o_hbm, xbuf, obuf, h_scr, gsem, ssem, *, tm, nt, n_cond):
    del tg
    t = pl.program_id(0)
    slot = t % 2
    n = nvalid[t]

    def gather_copy(sl, r, row):
        return pltpu.make_async_copy(x_hbm.at[pl.ds(row, 1)], xbuf.at[sl, pl.ds(r, 1)], gsem.at[sl])

    def scatter_copy(sl, r, row):
        return pltpu.make_async_copy(obuf.at[sl, pl.ds(r, 1)], o_hbm.at[pl.ds(row, 1)], ssem.at[sl])

    def start_rows(make, idx_ref, count):
        def pair(i, carry):
            make(2 * i, idx_ref[0, 0, 2 * i]).start(priority=0)
            make(2 * i + 1, idx_ref[0, 0, 2 * i + 1]).start(priority=1)
            return carry

        lax.fori_loop(0, count // 2, pair, 0)

        @pl.when(count % 2 == 1)
        def _():
            make(count - 1, idx_ref[0, 0, count - 1]).start(priority=0)

    def wait_rows(make, count):
        def one(r, carry):
            make(r, 0).wait()
            return carry

        lax.fori_loop(0, count, one, 0)

    @pl.when(t == 0)
    def _():
        xbuf[...] = jnp.zeros_like(xbuf)
        start_rows(functools.partial(gather_copy, 0), src_ref, n)

    @pl.when(t + 1 < nt)
    def _():
        start_rows(functools.partial(gather_copy, 1 - slot), nsrc_ref, nvalid[jnp.minimum(t + 1, nt - 1)])

    wait_rows(functools.partial(gather_copy, slot), n)

    @pl.when(t >= 2)
    def _():
        wait_rows(functools.partial(scatter_copy, slot), nvalid[jnp.maximum(t - 2, 0)])

    @pl.when(n > 0)
    def _():
        mods = mod_ref[0]
        g2 = g2_ref[0]

        def cond_rows(cg, k):
            out = mods[0, k:k + 1]
            for c in range(1, n_cond):
                out = jnp.where(cg == float(c), mods[c, k:k + 1], out)
            return out

        def norm_body(r, carry):
            rows = pl.ds(pl.multiple_of(r * LANES, LANES), LANES)
            cg = cm_ref[rows, 4:5]
            h2 = _rms(xbuf[slot, rows, :], D) * g2 * (1.0 + cond_rows(cg, 4)) + cond_rows(cg, 3)
            h_scr[rows, :] = h2.astype(BF16)
            return carry

        lax.fori_loop(0, tm // LANES, norm_body, 0)

        hb = h_scr[...]
        y = None
        for e in range(EXPERTS_PER_GROUP):
            hid = _silu(_dot(hb, wg[0, e])) * _dot(hb, wu[0, e]) * cm_ref[:, e:e + 1]
            part = _dot(hid.astype(BF16), wd[0, e])
            y = part if y is None else y + part
        obuf[slot] = y

        def out_body(r, carry):
            rows = pl.ds(pl.multiple_of(r * LANES, LANES), LANES)
            cg = cm_ref[rows, 4:5]
            obuf[slot, rows, :] = xbuf[slot, rows, :] + cond_rows(cg, 5) * obuf[slot, rows, :]
            return carry

        lax.fori_loop(0, tm // LANES, out_body, 0)
        start_rows(functools.partial(scatter_copy, slot), src_ref, n)

    @pl.when(t == nt - 1)
    def _():
        if nt >= 2:
            wait_rows(functools.partial(scatter_copy, 1 - slot), nvalid[jnp.maximum(t - 1, 0)])
        wait_rows(functools.partial(scatter_copy, slot), n)


def _moe_plan(route, dims, tm):
    T = route.shape[1]
    nt = T // tm + N_GROUPS
    grp = route[4].astype(I32)
    order = jnp.argsort(grp, stable=True).astype(I32)
    counts = jnp.sum(grp[None, :] == jnp.arange(N_GROUPS, dtype=I32)[:, None], axis=1).astype(I32)
    tiles = (counts + tm - 1) // tm
    tile_end = jnp.cumsum(tiles)
    tile_start = tile_end - tiles
    tok_start = jnp.cumsum(counts) - counts
    tidx = jnp.arange(nt, dtype=I32)
    tg = jnp.minimum(jnp.sum(tidx[:, None] >= tile_end[None, :], axis=1), N_GROUPS - 1).astype(I32)
    used = tidx < tile_end[-1]
    in_group = (tidx - tile_start[tg]) * tm
    nvalid = jnp.where(used, jnp.clip(counts[tg] - in_group, 0, tm), 0).astype(I32)
    slot = jnp.arange(tm, dtype=I32)[None, :]
    valid = slot < nvalid[:, None]
    pos = jnp.clip(tok_start[tg][:, None] + in_group[:, None] + slot, 0, T - 1)
    src = jnp.where(valid, order[pos], 0).astype(I32)
    comb = jnp.where(valid[None], route[:4][:, src], 0.0)
    cond = jnp.where(src < dims["t_ctx"], 0, 1 + (src - dims["t_ctx"]) // dims["l_lat"]).astype(F32)
    cm = jnp.concatenate([comb, cond[None], jnp.zeros((3,) + cond.shape, F32)], axis=0)
    cm = cm.transpose(1, 2, 0).reshape(nt * tm, 8)
    return tg, nvalid, src.reshape(nt, 1, tm), cm


def _moe(x, route, mods, w, l, dims):
    T = x.shape[0]
    tm = 512 if T % 512 == 0 else 256
    nt = T // tm + N_GROUPS
    tg, nvalid, src, cm = _moe_plan(route, dims, tm)
    ff = EXPERT_FF

    def wspec(shape):
        return pl.BlockSpec((1, EXPERTS_PER_GROUP) + shape, lambda t, tg_, nv: (l, tg_[t], 0, 0),
                            pipeline_mode=pl.Buffered(1))

    grid_spec = pltpu.PrefetchScalarGridSpec(
        num_scalar_prefetch=2, grid=(nt,),
        in_specs=[
            pl.BlockSpec((1, 1, tm), lambda t, tg_, nv: (t, 0, 0), memory_space=pltpu.SMEM),
            pl.BlockSpec((1, 1, tm), lambda t, tg_, nv: (jnp.minimum(t + 1, nt - 1), 0, 0),
                         memory_space=pltpu.SMEM),
            pl.BlockSpec(memory_space=pl.ANY),
            pl.BlockSpec((tm, 8), lambda t, tg_, nv: (t, 0)),
            pl.BlockSpec((1, 8, 6, D), lambda t, tg_, nv: (l, 0, 0, 0)),
            pl.BlockSpec((1, 1, D), lambda t, tg_, nv: (l, 0, 0)),
            wspec((D, ff)), wspec((D, ff)), wspec((ff, D)),
        ],
        out_specs=pl.BlockSpec(memory_space=pl.ANY),
        scratch_shapes=[pltpu.VMEM((2, tm, D), F32), pltpu.VMEM((2, tm, D), F32), pltpu.VMEM((tm, D), BF16),
                        pltpu.SemaphoreType.DMA((2,)), pltpu.SemaphoreType.DMA((2,))])
    return pl.pallas_call(
        functools.partial(_moe_kernel, tm=tm, nt=nt, n_cond=1 + dims["b_lat"]),
        grid_spec=grid_spec,
        out_shape=jax.ShapeDtypeStruct((T, D), F32),
        compiler_params=_cparams(("arbitrary",)),
        name="moe_group_experts",
    )(tg, nvalid, src, src, x, cm, mods, w["g_norm2"], w["w_exp_gate"], w["w_exp_up"], w["w_exp_down"])


def _pack_params(w_in, w_gla_dec, b_gla_dec, w_mla_q_up, g_q_rope, g_k_rope,
                 w_group_router, b_group_router, w_expert_router, b_expert_router):
    gq, gk, gv, gr, glow, pin, qa, kva, kr, gl = jnp.split(
        w_in, [int(v) for v in np.cumsum(SPLIT_SIZES)[:-1]], axis=-1)

    def padc(a, n):
        return jnp.pad(a, ((0, 0), (0, 0), (0, n - a.shape[-1])))

    wp = jnp.concatenate([gl, gq, gk, gv, gr, pin, kva, qa, padc(glow, LANES), padc(kr, LANES)],
                         axis=-1).astype(BF16)
    wdec = jnp.zeros((DEPTH, 2, LANES, GLA_HEADS * GLA_DK), F32)
    for d in range(2):
        wdec = wdec.at[:, d, d * GLA_RANK:(d + 1) * GLA_RANK, :].set(w_gla_dec[:, d])
    wdec = wdec.astype(BF16)
    bdec = b_gla_dec.reshape(DEPTH, 2, 1, GLA_HEADS * GLA_DK)
    wq = w_mla_q_up.reshape(DEPTH, MLA_Q_LORA, MLA_HEADS, MLA_NOPE + MLA_ROPE)
    wq = jnp.pad(wq, ((0, 0), (0, 0), (0, 0), (0, 2 * LANES - MLA_NOPE - MLA_ROPE)))
    wq = wq.reshape(DEPTH, MLA_Q_LORA, MLA_HEADS * 2 * LANES).astype(BF16)
    g_qr = jnp.pad(g_q_rope, ((0, 0), (0, LANES - MLA_ROPE))).reshape(DEPTH, 1, LANES)
    g_kr = jnp.pad(g_k_rope, ((0, 0), (0, LANES - MLA_ROPE))).reshape(DEPTH, 1, LANES)
    wr = jnp.concatenate([w_group_router, w_expert_router], axis=-1).transpose(0, 2, 1)
    wr = jnp.pad(wr, ((0, 0), (0, 32 - wr.shape[1]), (0, 0)))
    hi = wr.astype(BF16)
    r1 = wr - hi.astype(F32)
    mid = r1.astype(BF16)
    lo = (r1 - mid.astype(F32)).astype(BF16)
    router_w = jnp.stack([hi, mid, lo], axis=1)
    rb = jnp.concatenate([b_group_router, b_expert_router], axis=-1)
    rb = jnp.pad(rb, ((0, 0), (0, 32 - rb.shape[1])))
    router_b = jnp.broadcast_to(rb[:, :, None], (DEPTH, 32, LANES))
    return wp, wdec, bdec, wq, g_qr, g_kr, router_w, router_b


def _rope_tables(dims):
    ll = dims["l_lat"]
    t = jnp.arange(ll)
    row = (t // GRID_W).astype(F32)
    col = (t % GRID_W).astype(F32)
    n_freq = MLA_ROPE // 4
    inv = ROPE_THETA ** (-jnp.arange(n_freq, dtype=F32) / n_freq)
    ang = jnp.stack([row[:, None] * inv, col[:, None] * inv], axis=1)
    cos, sin = jnp.cos(ang), jnp.sin(ang)
    cos64 = jnp.concatenate([cos, cos], axis=-1).reshape(ll, MLA_ROPE)
    sin64 = jnp.concatenate([-sin, sin], axis=-1).reshape(ll, MLA_ROPE)
    pad = jnp.zeros((ll, LANES - MLA_ROPE), F32)
    cos_l = jnp.tile(jnp.concatenate([cos64, pad], axis=-1), (dims["b_lat"], 1))
    sin_l = jnp.tile(jnp.concatenate([sin64, pad], axis=-1), (dims["b_lat"], 1))
    cos_c = jnp.concatenate([jnp.ones((dims["t_ctx"], MLA_ROPE), F32),
                             jnp.zeros((dims["t_ctx"], LANES - MLA_ROPE), F32)], axis=-1)
    sin_c = jnp.zeros((dims["t_ctx"], LANES), F32)
    return jnp.concatenate([cos_c, cos_l], axis=0), jnp.concatenate([sin_c, sin_l], axis=0)


def kernel(x_prompt, x_sample, c, cache_mla_ckv, cache_mla_krope, state_gla, c_ctx, w_ada, b_ada, g_norm1, g_norm2, w_in, w_gla_dec, b_gla_dec, g_gla, w_pool, pool_scale, g_mla_qa, w_mla_q_up, g_mla_kva, w_mla_kv_up, g_q_nope, g_q_rope, g_k_nope, g_k_rope, w_br_gla, w_br_pool, w_br_mla, w_o, w_group_router, b_group_router, w_expert_router, b_expert_router, w_exp_gate, w_exp_up, w_exp_down):
    b_ctx, l_ctx, _ = x_prompt.shape
    b_lat, l_lat, _ = x_sample.shape
    dims = dict(b_ctx=b_ctx, l_ctx=l_ctx, b_lat=b_lat, l_lat=l_lat,
                t_ctx=b_ctx * l_ctx, t_lat=b_lat * l_lat)
    t_ctx = dims["t_ctx"]
    assert l_ctx % SLAB == 0 and l_lat % SLAB == 0 and t_ctx % l_lat == 0 and 1 + b_lat <= 8

    wp, wdec, bdec, wq, g_qr, g_kr, router_w, router_b = _pack_params(
        w_in, w_gla_dec, b_gla_dec, w_mla_q_up, g_q_rope, g_k_rope,
        w_group_router, b_group_router, w_expert_router, b_expert_router)
    wkv = w_mla_kv_up.astype(BF16)
    r3 = lambda a: a.reshape(DEPTH, 1, a.shape[-1])
    gains = (r3(g_mla_qa), r3(g_mla_kva), r3(g_q_nope), g_qr, r3(g_k_nope), g_kr)
    w = dict(w_br_gla=w_br_gla.astype(BF16), w_br_pool=w_br_pool.astype(BF16),
             w_br_mla=w_br_mla.astype(BF16), w_o=w_o.astype(BF16), g_gla=r3(g_gla),
             g_norm2=r3(g_norm2), router_w=router_w, router_b=router_b,
             w_exp_gate=w_exp_gate.astype(BF16), w_exp_up=w_exp_up.astype(BF16),
             w_exp_down=w_exp_down.astype(BF16))
    w_pool_b = w_pool.astype(BF16)
    pool_scale3 = r3(pool_scale)
    g1 = r3(g_norm1)

    cond = jnp.concatenate([c_ctx[None, :], c, jnp.zeros((8 - 1 - b_lat, D), F32)], axis=0)
    mods = _modulation(cond, w_ada, b_ada)

    cos, sin = _rope_tables(dims)
    cache_kr = jnp.pad(cache_mla_krope, ((0, 0), (0, 0), (0, 0), (0, LANES - MLA_ROPE)))
    kc, vc = _cache_kv(cache_mla_ckv, cache_kr, wkv, r3(g_k_nope))
    gla_tables = _gla_tables(dims)

    x = jnp.concatenate([x_prompt.reshape(t_ctx, D), x_sample.reshape(dims["t_lat"], D)], axis=0)
    ckv_l, kr_l, st_l = [], [], []
    for l in range(DEPTH):
        p = _proj_in(x, mods, g1, wp, l, dims)
        s0 = jnp.concatenate([jnp.zeros((1, 2, GLA_HEADS, GLA_DV, GLA_DK), F32),
                              jnp.swapaxes(state_gla[:, l], -1, -2)], axis=0)
        o_f, o_b, s_fin = _gla(p, wdec[l], bdec[l], s0, gla_tables, dims)
        pool_out = _pool(p, w_pool_b, pool_scale3, l, dims)
        qr, qu, k, v, ckv, kro = _mla_prep(p, cos, sin, wq, wkv, gains, l, dims)
        attn = jnp.concatenate([_attn_ctx(qr, k, v, dims),
                                _attn_lat(qr, qu, k, v, kc, vc, l, dims)], axis=0)
        x_mid, route = _merge(x, mods, o_f, o_b, p, pool_out, attn, w, l, dims)
        x = _moe(x_mid, route, mods, w, l, dims)
        ckv_l.append(ckv[:t_ctx].reshape(b_ctx, l_ctx, MLA_KV_LORA))
        kr_l.append(kro[:t_ctx, :MLA_ROPE].reshape(b_ctx, l_ctx, MLA_ROPE))
        st_l.append(jnp.swapaxes(s_fin[:b_ctx], -1, -2))

    y_prompt = x[:t_ctx].reshape(b_ctx, l_ctx, D)
    y_sample = x[t_ctx:].reshape(b_lat, l_lat, D)
    return (y_prompt, y_sample, jnp.stack(ckv_l, axis=1), jnp.stack(kr_l, axis=1),
            jnp.stack(st_l, axis=1))
```

```python
import functools
import math

import numpy as np
import jax
import jax.numpy as jnp
from jax import lax
from jax.experimental import pallas as pl
from jax.experimental.pallas import tpu as pltpu

F32 = jnp.float32
BF16 = jnp.bfloat16
I32 = jnp.int32

D = 2048
DEPTH = 4
EPS = 1e-6
GRID_W = 64
GLA_HEADS, GLA_DK, GLA_DV, GLA_RANK, GLA_TAU, GLA_CHUNK = 4, 128, 256, 16, 16.0, 64
POOL_WINDOWS = (2, 4, 8, 16)
POOL_GROUP_DIM = 256
MLA_HEADS, MLA_Q_LORA, MLA_KV_LORA, MLA_NOPE, MLA_ROPE, MLA_V = 8, 768, 512, 128, 64, 128
MLA_SCALE = 1.0 / math.sqrt(MLA_NOPE + MLA_ROPE)
ROPE_THETA = 10000.0
N_GROUPS, EXPERTS_PER_GROUP, EXPERT_FF = 4, 4, 512
SPLIT_SIZES = (512, 512, 1024, 1024, 32, 1024, 768, 512, 64, 6144)

LANES = 128
SLAB = 256
VMEM_LIMIT_BYTES = 60000 * 1024

C_GL = 0
C_Q = 6144
C_K = 6656
C_V = 7168
C_GR = 8192
C_PIN = 9216
C_KVA = 10240
C_QA = 10752
C_GLOW = 11520
C_KR = 11648
NP = 11776


def _dot(a, b):
    return jnp.dot(a, b, preferred_element_type=F32)


def _dot_nt(a, b):
    return lax.dot_general(a, b, (((1,), (1,)), ((), ())), preferred_element_type=F32)


def _dot_tn(a, b):
    return lax.dot_general(a, b, (((0,), (0,)), ((), ())), preferred_element_type=F32)


def _split3(x):
    hi = x.astype(BF16)
    r1 = x - hi.astype(F32)
    mid = r1.astype(BF16)
    lo = (r1 - mid.astype(F32)).astype(BF16)
    return hi, mid, lo


def _silu(x):
    return x * jax.nn.sigmoid(x)


def _rms(x, n):
    ms = jnp.sum(x * x, axis=-1, keepdims=True) * (1.0 / n)
    return x * lax.rsqrt(ms + EPS)


def _cparams(sem, vmem=VMEM_LIMIT_BYTES):
    return pltpu.CompilerParams(dimension_semantics=sem, vmem_limit_bytes=vmem)


def _pick_tile(cands, *extents):
    for c in cands:
        if all(e % c == 0 for e in extents):
            return c
    raise ValueError(f"no tile in {cands} divides {extents}")


def _mod_kernel(c_ref, w_ref, b_ref, o_ref):
    a = _silu(c_ref[...]).astype(BF16)
    o_ref[0] = _dot(a, w_ref[0].astype(BF16)) + b_ref[0]


def _modulation(cond, w_ada, b_ada):
    tn = 1024
    out = pl.pallas_call(
        _mod_kernel,
        grid=(DEPTH, 6 * D // tn),
        in_specs=[
            pl.BlockSpec((8, D), lambda l, j: (0, 0)),
            pl.BlockSpec((1, D, tn), lambda l, j: (l, 0, j)),
            pl.BlockSpec((1, 1, tn), lambda l, j: (l, 0, j)),
        ],
        out_specs=pl.BlockSpec((1, 8, tn), lambda l, j: (l, 0, j)),
        out_shape=jax.ShapeDtypeStruct((DEPTH, 8, 6 * D), F32),
        compiler_params=_cparams(("parallel", "parallel"), 40 * 1024 * 1024),
        name="adaln_modulation",
    )(cond, w_ada, b_ada.reshape(DEPTH, 1, 6 * D))
    return out.reshape(DEPTH, 8, 6, D)


def _proj_in_kernel(x_ref, mod_ref, g_ref, w_ref, o_ref, h_scr, *, tm):
    @pl.when(pl.program_id(1) == 0)
    def _():
        m = mod_ref[0, 0]
        g = g_ref[0]

        def body(r, carry):
            rows = pl.ds(pl.multiple_of(r * LANES, LANES), LANES)
            y = _rms(x_ref[rows, :], D) * g
            h_scr[rows, :] = (y * (1.0 + m[1:2]) + m[0:1]).astype(BF16)
            return carry

        lax.fori_loop(0, tm // LANES, body, 0)

    o_ref[...] = _dot(h_scr[...], w_ref[0]).astype(BF16)


def _proj_in(x, mods, g1, wp, l, dims):
    T = dims["t"]
    tm = _pick_tile((1024, 512, 256), dims["t_ctx"], dims["l_lat"])
    tn = 512
    gmap = _group_map(dims, tm)
    return pl.pallas_call(
        functools.partial(_proj_in_kernel, tm=tm),
        grid=(T // tm, NP // tn),
        in_specs=[
            pl.BlockSpec((tm, D), lambda i, j: (i, 0)),
            pl.BlockSpec((1, 1, 6, D), lambda i, j: (l, gmap(i), 0, 0)),
            pl.BlockSpec((1, 1, D), lambda i, j: (l, 0, 0)),
            pl.BlockSpec((1, D, tn), lambda i, j: (l, 0, j)),
        ],
        out_specs=pl.BlockSpec((tm, tn), lambda i, j: (i, j)),
        out_shape=jax.ShapeDtypeStruct((T, NP), BF16),
        scratch_shapes=[pltpu.VMEM((tm, D), BF16)],
        compiler_params=_cparams(("parallel", "arbitrary")),
        name="norm1_proj_in",
    )(x, mods, g1, wp)


def _group_map(dims, tm):
    n_ctx_tiles = dims["t_ctx"] // tm
    per_lat = dims["l_lat"] // tm

    def gmap(i):
        return jnp.where(i < n_ctx_tiles, 0, 1 + (i - n_ctx_tiles) // per_lat)

    return gmap


def _gla_kernel(fblk, bblk, first, last, sidx, unit,
                qf, kf, vf, gf, qb, kb, vb, gb, wdec, bdec, s0,
                of, ob, sout, s_scr):
    del fblk, bblk, sidx, unit
    s = pl.program_id(0)

    @pl.when(first[s] == 1)
    def _():
        s_scr[...] = s0[0]

    n_chunks = SLAB // GLA_CHUNK
    r = lax.broadcasted_iota(I32, (SLAB, SLAB), 0)
    c = lax.broadcasted_iota(I32, (SLAB, SLAB), 1)
    same = (r // GLA_CHUNK) == (c // GLA_CHUNK)

    def rows(x, ci):
        return x[ci * GLA_CHUNK:(ci + 1) * GLA_CHUNK]

    dirs = ((qf, kf, vf, gf, of), (qb, kb, vb, gb, ob))
    for d, (q_ref, k_ref, v_ref, g_ref, o_ref) in enumerate(dirs):
        tri = jnp.logical_and(same, (c <= r) if d == 0 else (c >= r))
        tri_b = tri.astype(BF16)
        dec = _dot(g_ref[...], wdec[d]) + bdec[d]
        la = (jnp.minimum(dec, 0.0) - jnp.log1p(jnp.exp(-jnp.abs(dec)))) * (1.0 / GLA_TAU)
        hi, mid, lo = _split3(la)
        b_all = _dot(tri_b, hi) + _dot(tri_b, mid) + _dot(tri_b, lo)
        edge = GLA_CHUNK - 1 if d == 0 else 0
        tot = [b_all[ci * GLA_CHUNK + edge:ci * GLA_CHUNK + edge + 1] for ci in range(n_chunks)]
        bl_all = jnp.concatenate([jnp.broadcast_to(t, (GLA_CHUNK, t.shape[1])) for t in tot], axis=0)
        order = range(n_chunks) if d == 0 else range(n_chunks - 1, -1, -1)
        for h in range(GLA_HEADS):
            kc = slice(h * GLA_DK, (h + 1) * GLA_DK)
            vc = slice(h * GLA_DV, (h + 1) * GLA_DV)
            b = b_all[:, kc]
            bl = bl_all[:, kc]
            q = q_ref[:, kc].astype(F32) * (GLA_DK ** -0.5)
            k = k_ref[:, kc].astype(F32)
            v = v_ref[:, vc]
            qt = (q * jnp.exp(b)).astype(BF16)
            kt = (k * jnp.exp(-b)).astype(BF16)
            ke = (k * jnp.exp(bl - b)).astype(BF16)
            a = jnp.where(tri, _dot_nt(qt, kt), 0.0).astype(BF16)
            o_intra = _dot(a, v)
            ds_t = [_dot_tn(rows(v, ci), rows(ke, ci)) for ci in range(n_chunks)]
            decay = [jnp.exp(tot[ci][:, kc]) for ci in range(n_chunks)]
            sd = s_scr[d, h]
            o_inter = [None] * n_chunks
            for ci in order:
                o_inter[ci] = _dot_nt(rows(qt, ci), sd.astype(BF16))
                sd = sd * decay[ci] + ds_t[ci]
            o_ref[:, vc] = (o_intra + jnp.concatenate(o_inter, axis=0)).astype(BF16)
            s_scr[d, h] = sd

            @pl.when(last[s] == 1)
            def _(sd=sd, d=d, h=h):
                sout[0, d, h] = sd


def _gla_tables(dims):
    ctx_slabs = dims["l_ctx"] // SLAB
    lat_slabs = dims["l_lat"] // SLAB
    fblk, bblk, first, last, sidx, unit = [], [], [], [], [], []
    base = 0
    for u in range(dims["b_ctx"] + dims["b_lat"]):
        is_ctx = u < dims["b_ctx"]
        n = ctx_slabs if is_ctx else lat_slabs
        for j in range(n):
            fblk.append(base + j)
            bblk.append(base + n - 1 - j)
            first.append(int(j == 0))
            last.append(int(j == n - 1))
            sidx.append(0 if is_ctx else 1 + u - dims["b_ctx"])
            unit.append(u)
        base += n
    return [jnp.asarray(np.asarray(t, np.int32)) for t in (fblk, bblk, first, last, sidx, unit)]


def _gla(p, wdec, bdec, s0, tables, dims):
    T = p.shape[0]
    n_steps = T // SLAB
    n_units = dims["b_ctx"] + dims["b_lat"]

    hk, hv = GLA_HEADS * GLA_DK, GLA_HEADS * GLA_DV
    state = (2, GLA_HEADS, GLA_DV, GLA_DK)

    def pspec(width, col0, which):
        cb = col0 // width
        if which == 0:
            return pl.BlockSpec((SLAB, width), lambda s, fb, bb, fi, la, si, un: (fb[s], cb))
        return pl.BlockSpec((SLAB, width), lambda s, fb, bb, fi, la, si, un: (bb[s], cb))

    in_specs = []
    for which in (0, 1):
        in_specs += [pspec(hk, C_Q, which), pspec(hk, C_K, which), pspec(hv, C_V, which),
                     pspec(LANES, C_GLOW, which)]
    in_specs += [
        pl.BlockSpec((2, LANES, hk), lambda s, *_: (0, 0, 0)),
        pl.BlockSpec((2, 1, hk), lambda s, *_: (0, 0, 0)),
        pl.BlockSpec((1,) + state, lambda s, fb, bb, fi, la, si, un: (si[s], 0, 0, 0, 0)),
    ]
    out_specs = [
        pl.BlockSpec((SLAB, hv), lambda s, fb, bb, fi, la, si, un: (fb[s], 0)),
        pl.BlockSpec((SLAB, hv), lambda s, fb, bb, fi, la, si, un: (bb[s], 0)),
        pl.BlockSpec((1,) + state, lambda s, fb, bb, fi, la, si, un: (un[s], 0, 0, 0, 0)),
    ]
    grid_spec = pltpu.PrefetchScalarGridSpec(
        num_scalar_prefetch=6, grid=(n_steps,),
        in_specs=in_specs, out_specs=out_specs,
        scratch_shapes=[pltpu.VMEM(state, F32)])
    return pl.pallas_call(
        _gla_kernel,
        grid_spec=grid_spec,
        out_shape=[jax.ShapeDtypeStruct((T, hv), BF16),
                   jax.ShapeDtypeStruct((T, hv), BF16),
                   jax.ShapeDtypeStruct((n_units,) + state, F32)],
        compiler_params=_cparams(("arbitrary",), 32 * 1024 * 1024),
        name="gla_bidirectional",
    )(*tables, p, p, p, p, p, p, p, p, wdec, bdec, s0)


def _pool_kernel(cur, prv, nxt, wp, sc, o_ref, *, n_ctx_slabs, ctx_slabs, lat_slabs):
    i = pl.program_id(0)
    is_ctx = i < n_ctx_slabs
    seq_slabs = jnp.where(is_ctx, ctx_slabs, lat_slabs)
    j = jnp.where(is_ctx, i % ctx_slabs, (i - n_ctx_slabs) % lat_slabs)
    has_prev = j > 0
    has_next = j < seq_slabs - 1
    seq_len = seq_slabs * SLAB
    r = lax.broadcasted_iota(I32, (SLAB, SLAB), 0)
    c = lax.broadcasted_iota(I32, (SLAB, SLAB), 1)
    t = j * SLAB + lax.broadcasted_iota(I32, (SLAB, 1), 0)
    gd = POOL_GROUP_DIM
    for g, w in enumerate(POOL_WINDOWS):
        lo_off, hi_off = w // 2, w - w // 2
        cols = slice(g * gd, (g + 1) * gd)
        u = cur[:, cols]
        b_cur = jnp.logical_and(c >= r - lo_off, c < r + hi_off)
        b_prv = jnp.logical_and(c - SLAB >= r - lo_off, has_prev)
        b_nxt = jnp.logical_and(c + SLAB < r + hi_off, has_next)
        ssum = (_dot(b_cur.astype(BF16), u) + _dot(b_prv.astype(BF16), prv[:, cols])
                + _dot(b_nxt.astype(BF16), nxt[:, cols]))
        cnt = (jnp.minimum(t + hi_off, seq_len) - jnp.maximum(t - lo_off, 0)).astype(F32)
        pooled = ssum / cnt - u.astype(F32)
        o_ref[:, cols] = (_dot(pooled.astype(BF16), wp[0, g]) * sc[0, :, cols]).astype(BF16)


def _pool(p, w_pool, pool_scale, l, dims):
    T = p.shape[0]
    n = T // SLAB
    cb = C_PIN // 1024
    kern = functools.partial(_pool_kernel, n_ctx_slabs=dims["t_ctx"] // SLAB,
                             ctx_slabs=dims["l_ctx"] // SLAB, lat_slabs=dims["l_lat"] // SLAB)
    return pl.pallas_call(
        kern,
        grid=(n,),
        in_specs=[
            pl.BlockSpec((SLAB, 1024), lambda i: (i, cb)),
            pl.BlockSpec((SLAB, 1024), lambda i: (jnp.maximum(i - 1, 0), cb)),
            pl.BlockSpec((SLAB, 1024), lambda i: (jnp.minimum(i + 1, n - 1), cb)),
            pl.BlockSpec((1, 4, 256, 256), lambda i: (l, 0, 0, 0)),
            pl.BlockSpec((1, 1, 1024), lambda i: (l, 0, 0)),
        ],
        out_specs=pl.BlockSpec((SLAB, 1024), lambda i: (i, 0)),
        out_shape=jax.ShapeDtypeStruct((T, 1024), BF16),
        compiler_params=_cparams(("parallel",), 32 * 1024 * 1024),
        name="pool_mixer",
    )(p, p, p, w_pool, pool_scale)


def _swap16(x):
    lane = lax.broadcasted_iota(I32, x.shape, x.ndim - 1)
    n = x.shape[-1]
    fwd = pltpu.roll(x, n - 16, x.ndim - 1)
    bwd = pltpu.roll(x, 16, x.ndim - 1)
    return jnp.where((lane % 32) < 16, fwd, bwd)


def _mla_prep_kernel(kva_ref, qa_ref, kr_ref, cos_ref, sin_ref, wq, wkv,
                     g_qa, g_kva, g_qn, g_qr, g_kn, g_kr,
                     qr_ref, qu_ref, k_ref, v_ref, ckv_ref, kro_ref):
    ckv = _rms(kva_ref[...].astype(F32), MLA_KV_LORA) * g_kva[0]
    ckv_ref[...] = ckv
    kv = _dot(ckv.astype(BF16), wkv[0])
    qn = _rms(qa_ref[...].astype(F32), MLA_Q_LORA) * g_qa[0]
    q = _dot(qn.astype(BF16), wq[0])
    cos = cos_ref[...]
    sin = sin_ref[...]
    kr = _rms(kr_ref[...].astype(F32), MLA_ROPE) * g_kr[0]
    kro_ref[...] = kr
    kr_rot = (kr * cos + _swap16(kr) * sin).astype(BF16)
    for h in range(MLA_HEADS):
        c0 = 2 * LANES * h
        q_nope = _rms(q[:, c0:c0 + LANES], MLA_NOPE) * g_qn[0] * MLA_SCALE
        q_rope = _rms(q[:, c0 + LANES:c0 + 2 * LANES], MLA_ROPE) * g_qr[0]
        q_rot = q_rope * cos + _swap16(q_rope) * sin
        qr_ref[:, c0:c0 + LANES] = q_nope.astype(BF16)
        qu_ref[:, c0:c0 + LANES] = q_nope.astype(BF16)
        qr_ref[:, c0 + LANES:c0 + 2 * LANES] = (q_rot * MLA_SCALE).astype(BF16)
        qu_ref[:, c0 + LANES:c0 + 2 * LANES] = (q_rope * MLA_SCALE).astype(BF16)
        k_nope = _rms(kv[:, c0:c0 + LANES], MLA_NOPE) * g_kn[0]
        k_ref[:, c0:c0 + LANES] = k_nope.astype(BF16)
        k_ref[:, c0 + LANES:c0 + 2 * LANES] = kr_rot
        v_ref[:, LANES * h:LANES * (h + 1)] = kv[:, c0 + LANES:c0 + 2 * LANES].astype(BF16)


def _mla_prep(p, cos, sin, wq, wkv, gains, l, dims):
    T = p.shape[0]
    tm = _pick_tile((512, 256), dims["t_ctx"], dims["l_lat"])
    hw = MLA_HEADS * 2 * LANES

    def gspec(n):
        return pl.BlockSpec((1, 1, n), lambda i: (l, 0, 0))

    return pl.pallas_call(
        _mla_prep_kernel,
        grid=(T // tm,),
        in_specs=[
            pl.BlockSpec((tm, MLA_KV_LORA), lambda i: (i, C_KVA // MLA_KV_LORA)),
            pl.BlockSpec((tm, MLA_Q_LORA), lambda i: (i, C_QA // MLA_Q_LORA)),
            pl.BlockSpec((tm, LANES), lambda i: (i, C_KR // LANES)),
            pl.BlockSpec((tm, LANES), lambda i: (i, 0)),
            pl.BlockSpec((tm, LANES), lambda i: (i, 0)),
            pl.BlockSpec((1, MLA_Q_LORA, hw), lambda i: (l, 0, 0)),
            pl.BlockSpec((1, MLA_KV_LORA, hw), lambda i: (l, 0, 0)),
            gspec(MLA_Q_LORA), gspec(MLA_KV_LORA), gspec(LANES), gspec(LANES), gspec(LANES), gspec(LANES),
        ],
        out_specs=[
            pl.BlockSpec((tm, hw), lambda i: (i, 0)),
            pl.BlockSpec((tm, hw), lambda i: (i, 0)),
            pl.BlockSpec((tm, hw), lambda i: (i, 0)),
            pl.BlockSpec((tm, MLA_HEADS * MLA_V), lambda i: (i, 0)),
            pl.BlockSpec((tm, MLA_KV_LORA), lambda i: (i, 0)),
            pl.BlockSpec((tm, LANES), lambda i: (i, 0)),
        ],
        out_shape=[
            jax.ShapeDtypeStruct((T, hw), BF16),
            jax.ShapeDtypeStruct((T, hw), BF16),
            jax.ShapeDtypeStruct((T, hw), BF16),
            jax.ShapeDtypeStruct((T, MLA_HEADS * MLA_V), BF16),
            jax.ShapeDtypeStruct((T, MLA_KV_LORA), F32),
            jax.ShapeDtypeStruct((T, LANES), F32),
        ],
        compiler_params=_cparams(("parallel",), 48 * 1024 * 1024),
        name="mla_prep",
    )(p, p, p, cos, sin, wq, wkv, *gains)


def _cache_kv_kernel(ckv_ref, kr_ref, wkv, g_kn, k_ref, v_ref):
    kv = _dot(ckv_ref[0, 0].astype(BF16), wkv[0])
    kr = kr_ref[0, 0].astype(BF16)
    for h in range(MLA_HEADS):
        c0 = 2 * LANES * h
        k_nope = _rms(kv[:, c0:c0 + LANES], MLA_NOPE) * g_kn[0]
        k_ref[0, 0, :, c0:c0 + LANES] = k_nope.astype(BF16)
        k_ref[0, 0, :, c0 + LANES:c0 + 2 * LANES] = kr
        v_ref[0, 0, :, LANES * h:LANES * (h + 1)] = kv[:, c0 + LANES:c0 + 2 * LANES].astype(BF16)


def _cache_kv(cache_ckv, cache_kr, wkv, g_kn):
    b_lat, _, past, _ = cache_ckv.shape
    hw = MLA_HEADS * 2 * LANES
    return pl.pallas_call(
        _cache_kv_kernel,
        grid=(DEPTH, b_lat),
        in_specs=[
            pl.BlockSpec((1, 1, past, MLA_KV_LORA), lambda l, b: (b, l, 0, 0)),
            pl.BlockSpec((1, 1, past, LANES), lambda l, b: (b, l, 0, 0)),
            pl.BlockSpec((1, MLA_KV_LORA, hw), lambda l, b: (l, 0, 0)),
            pl.BlockSpec((1, 1, LANES), lambda l, b: (l, 0, 0)),
        ],
        out_specs=[
            pl.BlockSpec((1, 1, past, hw), lambda l, b: (l, b, 0, 0)),
            pl.BlockSpec((1, 1, past, MLA_HEADS * MLA_V), lambda l, b: (l, b, 0, 0)),
        ],
        out_shape=[
            jax.ShapeDtypeStruct((DEPTH, b_lat, past, hw), BF16),
            jax.ShapeDtypeStruct((DEPTH, b_lat, past, MLA_HEADS * MLA_V), BF16),
        ],
        compiler_params=_cparams(("parallel", "parallel"), 32 * 1024 * 1024),
        name="mla_cache_decompress",
    )(cache_ckv, cache_kr, wkv, g_kn)


def _attn_ctx_kernel(q_ref, k_ref, v_ref, o_ref):
    for h in range(MLA_HEADS):
        qk = slice(h * 2 * LANES, (h + 1) * 2 * LANES)
        vs = slice(h * MLA_V, (h + 1) * MLA_V)
        s = _dot_nt(q_ref[:, qk], k_ref[:, qk])
        m = jnp.max(s, axis=-1, keepdims=True)
        pr = jnp.exp(s - m)
        den = jnp.sum(pr, axis=-1, keepdims=True)
        o_ref[:, vs] = (_dot(pr.astype(BF16), v_ref[:, vs]) / den).astype(BF16)


def _attn_ctx(q, k, v, dims):
    lc = dims["l_ctx"]
    t_ctx = dims["t_ctx"]
    hw = MLA_HEADS * 2 * LANES
    return pl.pallas_call(
        _attn_ctx_kernel,
        grid=(dims["b_ctx"],),
        in_specs=[
            pl.BlockSpec((lc, hw), lambda s: (s, 0)),
            pl.BlockSpec((lc, hw), lambda s: (s, 0)),
            pl.BlockSpec((lc, MLA_HEADS * MLA_V), lambda s: (s, 0)),
        ],
        out_specs=pl.BlockSpec((lc, MLA_HEADS * MLA_V), lambda s: (s, 0)),
        out_shape=jax.ShapeDtypeStruct((t_ctx, MLA_HEADS * MLA_V), BF16),
        compiler_params=_cparams(("parallel",), 32 * 1024 * 1024),
        name="mla_attention_context",
    )(q, k, v)


ATTN_HEADS_PER_STEP = 2


def _attn_lat_kernel(qr_ref, qu_ref, k_ref, v_ref, kc_ref, vc_ref, o_ref):
    for h in range(ATTN_HEADS_PER_STEP):
        qk = slice(h * 2 * LANES, (h + 1) * 2 * LANES)
        vs = slice(h * MLA_V, (h + 1) * MLA_V)
        s1 = _dot_nt(qr_ref[:, qk], k_ref[:, qk])
        s2 = _dot_nt(qu_ref[:, qk], kc_ref[0, 0, :, qk])
        m = jnp.maximum(jnp.max(s1, axis=-1, keepdims=True), jnp.max(s2, axis=-1, keepdims=True))
        p1 = jnp.exp(s1 - m)
        p2 = jnp.exp(s2 - m)
        den = jnp.sum(p1, axis=-1, keepdims=True) + jnp.sum(p2, axis=-1, keepdims=True)
        o = _dot(p1.astype(BF16), v_ref[:, vs]) + _dot(p2.astype(BF16), vc_ref[0, 0, :, vs])
        o_ref[:, vs] = (o / den).astype(BF16)


def _attn_lat(qr, qu, k, v, kc, vc, l, dims):
    ll = dims["l_lat"]
    past = kc.shape[2]
    tq = _pick_tile((256,), ll)
    q0 = dims["t_ctx"] // tq
    k0 = dims["t_ctx"] // ll
    nq = ll // tq
    hp = ATTN_HEADS_PER_STEP
    qw, vw = hp * 2 * LANES, hp * MLA_V
    return pl.pallas_call(
        _attn_lat_kernel,
        grid=(dims["b_lat"], MLA_HEADS // hp, nq),
        in_specs=[
            pl.BlockSpec((tq, qw), lambda b, h, i: (q0 + b * nq + i, h)),
            pl.BlockSpec((tq, qw), lambda b, h, i: (q0 + b * nq + i, h)),
            pl.BlockSpec((ll, qw), lambda b, h, i: (k0 + b, h)),
            pl.BlockSpec((ll, vw), lambda b, h, i: (k0 + b, h)),
            pl.BlockSpec((1, 1, past, qw), lambda b, h, i: (l, b, 0, h)),
            pl.BlockSpec((1, 1, past, vw), lambda b, h, i: (l, b, 0, h)),
        ],
        out_specs=pl.BlockSpec((tq, vw), lambda b, h, i: (b * nq + i, h)),
        out_shape=jax.ShapeDtypeStruct((dims["b_lat"] * ll, MLA_HEADS * MLA_V), BF16),
        compiler_params=_cparams(("parallel", "parallel", "arbitrary"), 48 * 1024 * 1024),
        name="mla_attention_latent",
    )(qr, qu, k, v, kc, vc)


def _route(logits_t):
    ng, ne = N_GROUPS, EXPERTS_PER_GROUP
    lg = [logits_t[g:g + 1] for g in range(ng)]
    mg = functools.reduce(jnp.maximum, lg)
    zg = functools.reduce(lambda a, b: a + b, [jnp.exp(x - mg) for x in lg])
    pg_top = 1.0 / zg
    grp = jnp.full_like(mg, float(ng))
    for g in range(ng - 1, -1, -1):
        grp = jnp.where(lg[g] == mg, float(g), grp)
    el = []
    for j in range(ne):
        acc = jnp.zeros_like(mg)
        for g in range(ng):
            row = ng + g * ne + j
            acc = jnp.where(grp == float(g), logits_t[row:row + 1], acc)
        el.append(acc)
    m1 = functools.reduce(jnp.maximum, el)
    i1 = jnp.full_like(mg, float(ne))
    for j in range(ne - 1, -1, -1):
        i1 = jnp.where(el[j] == m1, float(j), i1)
    neg = jnp.full_like(mg, -jnp.inf)
    rest = [jnp.where(i1 == float(j), neg, el[j]) for j in range(ne)]
    m2 = functools.reduce(jnp.maximum, rest)
    i2 = jnp.full_like(mg, float(ne))
    for j in range(ne - 1, -1, -1):
        i2 = jnp.where(rest[j] == m2, float(j), i2)
    e2 = jnp.exp(m2 - m1)
    w1 = 1.0 / (1.0 + e2)
    w2 = e2 / (1.0 + e2)
    rows = [pg_top * (jnp.where(i1 == float(j), w1, 0.0) + jnp.where(i2 == float(j), w2, 0.0))
            for j in range(ne)]
    rows += [grp, jnp.zeros_like(mg), jnp.zeros_like(mg), jnp.zeros_like(mg)]
    return jnp.concatenate(rows, axis=0)


def _merge_kernel(x_ref, mod_ref, of_ref, ob_ref, gr_ref, pool_ref, attn_ref, gl0, gl1, gl2,
                  wbg, wbp, wbm, wo, g_gla, g2_ref, wr, br,
                  xo_ref, r_ref, a_scr, m_scr, *, tm):
    m = mod_ref[0, 0]
    gg = g_gla[0]
    for h in range(GLA_HEADS):
        cols = slice(h * GLA_DV, (h + 1) * GLA_DV)
        o = of_ref[:, cols].astype(F32) + ob_ref[:, cols].astype(F32)
        a_scr[:, cols] = (_rms(o, GLA_DV) * gg * _silu(gr_ref[:, cols].astype(F32))).astype(BF16)
    nb = 512
    for n in range(D // nb):
        cols = slice(n * nb, (n + 1) * nb)
        acc = jax.nn.sigmoid(gl0[:, cols].astype(F32)) * _dot(a_scr[...], wbg[0, :, cols])
        acc += jax.nn.sigmoid(gl1[:, cols].astype(F32)) * _dot(pool_ref[...], wbp[0, :, cols])
        acc += jax.nn.sigmoid(gl2[:, cols].astype(F32)) * _dot(attn_ref[...], wbm[0, :, cols])
        m_scr[:, cols] = acc.astype(BF16)
    xm = x_ref[...] + m[2:3] * _dot(m_scr[...], wo[0])
    xo_ref[...] = xm
    g2 = g2_ref[0]
    for ci in range(tm // LANES):
        rows = slice(ci * LANES, (ci + 1) * LANES)
        h2 = _rms(xm[rows], D) * g2 * (1.0 + m[4:5]) + m[3:4]
        h_hi = h2.astype(BF16)
        hs = (h_hi, (h2 - h_hi.astype(F32)).astype(BF16))
        lt = br[0]
        for ia, ib in ((0, 0), (0, 1), (1, 0)):
            lt = lt + _dot_nt(wr[0, ia], hs[ib])
        r_ref[:, rows] = _route(lt)


def _merge(x, mods, o_f, o_b, p, pool_out, attn, w, l, dims):
    T = dims["t"]
    tm = _pick_tile((256,), dims["t_ctx"], dims["l_lat"])
    gmap = _group_map(dims, tm)

    def wspec(shape):
        return pl.BlockSpec((1,) + shape, lambda i: (l,) + (0,) * len(shape), pipeline_mode=pl.Buffered(1))

    br_tiled = w["router_b"]
    return pl.pallas_call(
        functools.partial(_merge_kernel, tm=tm),
        grid=(T // tm,),
        in_specs=[
            pl.BlockSpec((tm, D), lambda i: (i, 0)),
            pl.BlockSpec((1, 1, 6, D), lambda i: (l, gmap(i), 0, 0)),
            pl.BlockSpec((tm, 1024), lambda i: (i, 0)),
            pl.BlockSpec((tm, 1024), lambda i: (i, 0)),
            pl.BlockSpec((tm, 1024), lambda i: (i, C_GR // 1024)),
            pl.BlockSpec((tm, 1024), lambda i: (i, 0)),
            pl.BlockSpec((tm, 1024), lambda i: (i, 0)),
            pl.BlockSpec((tm, D), lambda i: (i, 0)),
            pl.BlockSpec((tm, D), lambda i: (i, 1)),
            pl.BlockSpec((tm, D), lambda i: (i, 2)),
            wspec((1024, D)), wspec((1024, D)), wspec((1024, D)), wspec((D, D)),
            pl.BlockSpec((1, 1, GLA_DV), lambda i: (l, 0, 0)),
            pl.BlockSpec((1, 1, D), lambda i: (l, 0, 0)),
            pl.BlockSpec((1, 2, 32, D), lambda i: (l, 0, 0, 0)),
            pl.BlockSpec((1, 32, LANES), lambda i: (l, 0, 0)),
        ],
        out_specs=[
            pl.BlockSpec((tm, D), lambda i: (i, 0)),
            pl.BlockSpec((8, tm), lambda i: (0, i)),
        ],
        out_shape=[jax.ShapeDtypeStruct((T, D), F32), jax.ShapeDtypeStruct((8, T), F32)],
        scratch_shapes=[pltpu.VMEM((tm, 1024), BF16), pltpu.VMEM((tm, D), BF16)],
        compiler_params=_cparams(("parallel",)),
        name="merge_out_route",
    )(x, mods, o_f, o_b, p, pool_out, attn, p, p, p,
      w["w_br_gla"], w["w_br_pool"], w["w_br_mla"], w["w_o"], w["g_gla"], w["g_norm2"],
      w["router_w"], br_tiled)


def _moe_kernel(tg, nvalid, gi_ref, gn_ref, si_ref, x_hbm, cm_ref, mod_ref, g2_ref, wg, wu, wd,
                o_hbm, xbuf, obuf, h_scr, gsem, ssem, *, tm, nt, n_cond):
    del tg
    t = pl.program_id(0)
    slot = t % 2

    def used(i):
        return nvalid[jnp.clip(i, 0, nt - 1)] > 0

    def gather_copy(sl, r, row):
        return pltpu.make_async_copy(x_hbm.at[pl.ds(row, 1)], xbuf.at[sl, pl.ds(r, 1)], gsem.at[sl])

    def scatter_copy(sl, r, row):
        return pltpu.make_async_copy(obuf.at[sl, pl.ds(r, 1)], o_hbm.at[pl.ds(row, 1)], ssem.at[sl])

    def start_rows(make, idx_ref):
        def pair(i, carry):
            make(2 * i, idx_ref[0, 0, 2 * i]).start(priority=0)
            make(2 * i + 1, idx_ref[0, 0, 2 * i + 1]).start(priority=1)
            return carry

        lax.fori_loop(0, tm // 2, pair, 0, unroll=8)

    def wait_gather(sl):
        pltpu.make_async_copy(x_hbm.at[pl.ds(0, tm)], xbuf.at[sl], gsem.at[sl]).wait()

    def wait_scatter(sl):
        pltpu.make_async_copy(obuf.at[sl], o_hbm.at[pl.ds(0, tm)], ssem.at[sl]).wait()

    @pl.when(t == 0)
    def _():
        obuf[1] = jnp.zeros((tm, D), F32)
        n_tok = o_hbm.shape[0] - N_GROUPS * tm
        for g in range(N_GROUPS):
            pltpu.make_async_copy(obuf.at[1], o_hbm.at[pl.ds(n_tok + g * tm, tm)], ssem.at[1]).start()
        for g in range(N_GROUPS):
            wait_scatter(1)

    @pl.when(jnp.logical_and(t == 0, used(0)))
    def _():
        start_rows(functools.partial(gather_copy, 0), gi_ref)

    @pl.when(jnp.logical_and(t + 1 < nt, used(t + 1)))
    def _():
        start_rows(functools.partial(gather_copy, 1 - slot), gn_ref)

    @pl.when(jnp.logical_and(t >= 2, used(t - 2)))
    def _():
        wait_scatter(slot)

    @pl.when(used(t))
    def _():
        wait_gather(slot)
        mods = mod_ref[0]
        g2 = g2_ref[0]

        def cond_rows(cg, k):
            out = mods[0, k:k + 1]
            for c in range(1, n_cond):
                out = jnp.where(cg == float(c), mods[c, k:k + 1], out)
            return out

        def norm_body(r, carry):
            rows = pl.ds(pl.multiple_of(r * LANES, LANES), LANES)
            cg = cm_ref[rows, 4:5]
            h2 = _rms(xbuf[slot, rows, :], D) * g2 * (1.0 + cond_rows(cg, 4)) + cond_rows(cg, 3)
            h_scr[rows, :] = h2.astype(BF16)
            return carry

        lax.fori_loop(0, tm // LANES, norm_body, 0)

        hb = h_scr[...]
        y = None
        for e in range(EXPERTS_PER_GROUP):
            hid = _silu(_dot(hb, wg[0, e])) * _dot(hb, wu[0, e]) * cm_ref[:, e:e + 1]
            part = _dot(hid.astype(BF16), wd[0, e])
            y = part if y is None else y + part
        obuf[slot] = y

        def out_body(r, carry):
            rows = pl.ds(pl.multiple_of(r * LANES, LANES), LANES)
            cg = cm_ref[rows, 4:5]
            obuf[slot, rows, :] = xbuf[slot, rows, :] + cond_rows(cg, 5) * obuf[slot, rows, :]
            return carry

        lax.fori_loop(0, tm // LANES, out_body, 0)
        start_rows(functools.partial(scatter_copy, slot), si_ref)

    @pl.when(jnp.logical_and(t == nt - 1, used(t - 1)))
    def _():
        wait_scatter(1 - slot)

    @pl.when(jnp.logical_and(t == nt - 1, used(t)))
    def _():
        wait_scatter(slot)


def _moe_plan(route, dims, tm):
    T = route.shape[1]
    nt = T // tm + N_GROUPS
    grp = route[4].astype(I32)
    order = jnp.argsort(grp, stable=True).astype(I32)
    counts = jnp.sum(grp[None, :] == jnp.arange(N_GROUPS, dtype=I32)[:, None], axis=1).astype(I32)
    tiles = (counts + tm - 1) // tm
    tile_end = jnp.cumsum(tiles)
    tile_start = tile_end - tiles
    tok_start = jnp.cumsum(counts) - counts
    tidx = jnp.arange(nt, dtype=I32)
    tg = jnp.minimum(jnp.sum(tidx[:, None] >= tile_end[None, :], axis=1), N_GROUPS - 1).astype(I32)
    used = tidx < tile_end[-1]
    in_group = (tidx - tile_start[tg]) * tm
    nvalid = jnp.where(used, jnp.clip(counts[tg] - in_group, 0, tm), 0).astype(I32)
    slot = jnp.arange(tm, dtype=I32)[None, :]
    valid = slot < nvalid[:, None]
    pos = jnp.clip(tok_start[tg][:, None] + in_group[:, None] + slot, 0, T - 1)
    src = jnp.where(valid, order[pos], 0).astype(I32)
    dst = jnp.where(valid, src, T + tg[:, None] * tm + slot).astype(I32)
    comb = jnp.where(valid[None], route[:4][:, src], 0.0)
    cond = jnp.where(src < dims["t_ctx"], 0, 1 + (src - dims["t_ctx"]) // dims["l_lat"]).astype(F32)
    cm = jnp.concatenate([comb, cond[None], jnp.zeros((3,) + cond.shape, F32)], axis=0)
    cm = cm.transpose(1, 2, 0).reshape(nt * tm, 8)
    return tg, nvalid, src.reshape(nt, 1, tm), dst.reshape(nt, 1, tm), cm


def _moe_tile(dims):
    return 512 if dims["t"] % 512 == 0 else 256


def _moe(x, route, mods, w, l, dims):
    T = dims["t"]
    tm = _moe_tile(dims)
    nt = T // tm + N_GROUPS
    tg, nvalid, src, dst, cm = _moe_plan(route, dims, tm)
    ff = EXPERT_FF

    def wspec(shape):
        return pl.BlockSpec((1, EXPERTS_PER_GROUP) + shape, lambda t, tg_, nv: (l, tg_[t], 0, 0),
                            pipeline_mode=pl.Buffered(1))

    grid_spec = pltpu.PrefetchScalarGridSpec(
        num_scalar_prefetch=2, grid=(nt,),
        in_specs=[
            pl.BlockSpec((1, 1, tm), lambda t, tg_, nv: (t, 0, 0), memory_space=pltpu.SMEM),
            pl.BlockSpec((1, 1, tm), lambda t, tg_, nv: (jnp.minimum(t + 1, nt - 1), 0, 0),
                         memory_space=pltpu.SMEM),
            pl.BlockSpec((1, 1, tm), lambda t, tg_, nv: (t, 0, 0), memory_space=pltpu.SMEM),
            pl.BlockSpec(memory_space=pl.ANY),
            pl.BlockSpec((tm, 8), lambda t, tg_, nv: (t, 0)),
            pl.BlockSpec((1, 8, 6, D), lambda t, tg_, nv: (l, 0, 0, 0)),
            pl.BlockSpec((1, 1, D), lambda t, tg_, nv: (l, 0, 0)),
            wspec((D, ff)), wspec((D, ff)), wspec((ff, D)),
        ],
        out_specs=pl.BlockSpec(memory_space=pl.ANY),
        scratch_shapes=[pltpu.VMEM((2, tm, D), F32), pltpu.VMEM((2, tm, D), F32), pltpu.VMEM((tm, D), BF16),
                        pltpu.SemaphoreType.DMA((2,)), pltpu.SemaphoreType.DMA((2,))])
    return pl.pallas_call(
        functools.partial(_moe_kernel, tm=tm, nt=nt, n_cond=1 + dims["b_lat"]),
        grid_spec=grid_spec,
        out_shape=jax.ShapeDtypeStruct((T + N_GROUPS * tm, D), F32),
        compiler_params=_cparams(("arbitrary",)),
        name="moe_group_experts",
    )(tg, nvalid, src, src, dst, x, cm, mods, w["g_norm2"], w["w_exp_gate"], w["w_exp_up"], w["w_exp_down"])


def _pack_params(w_in, w_gla_dec, b_gla_dec, w_mla_q_up, g_q_rope, g_k_rope,
                 w_group_router, b_group_router, w_expert_router, b_expert_router):
    gq, gk, gv, gr, glow, pin, qa, kva, kr, gl = jnp.split(
        w_in, [int(v) for v in np.cumsum(SPLIT_SIZES)[:-1]], axis=-1)

    def padc(a, n):
        return jnp.pad(a, ((0, 0), (0, 0), (0, n - a.shape[-1])))

    pieces = [gl, gq, gk, gv, gr, pin, kva, qa, padc(glow, LANES), padc(kr, LANES)]
    wp = jnp.concatenate([a.astype(BF16) for a in pieces], axis=-1)
    wdec = jnp.zeros((DEPTH, 2, LANES, GLA_HEADS * GLA_DK), F32)
    for d in range(2):
        wdec = wdec.at[:, d, d * GLA_RANK:(d + 1) * GLA_RANK, :].set(w_gla_dec[:, d])
    wdec = wdec.astype(BF16)
    bdec = b_gla_dec.reshape(DEPTH, 2, 1, GLA_HEADS * GLA_DK)
    wq = w_mla_q_up.reshape(DEPTH, MLA_Q_LORA, MLA_HEADS, MLA_NOPE + MLA_ROPE)
    wq = jnp.pad(wq, ((0, 0), (0, 0), (0, 0), (0, 2 * LANES - MLA_NOPE - MLA_ROPE)))
    wq = wq.reshape(DEPTH, MLA_Q_LORA, MLA_HEADS * 2 * LANES).astype(BF16)
    g_qr = jnp.pad(g_q_rope, ((0, 0), (0, LANES - MLA_ROPE))).reshape(DEPTH, 1, LANES)
    g_kr = jnp.pad(g_k_rope, ((0, 0), (0, LANES - MLA_ROPE))).reshape(DEPTH, 1, LANES)
    wr = jnp.concatenate([w_group_router, w_expert_router], axis=-1).transpose(0, 2, 1)
    wr = jnp.pad(wr, ((0, 0), (0, 32 - wr.shape[1]), (0, 0)))
    hi = wr.astype(BF16)
    lo = (wr - hi.astype(F32)).astype(BF16)
    router_w = jnp.stack([hi, lo], axis=1)
    rb = jnp.concatenate([b_group_router, b_expert_router], axis=-1)
    rb = jnp.pad(rb, ((0, 0), (0, 32 - rb.shape[1])))
    router_b = jnp.broadcast_to(rb[:, :, None], (DEPTH, 32, LANES))
    return wp, wdec, bdec, wq, g_qr, g_kr, router_w, router_b


def _rope_tables(dims):
    ll = dims["l_lat"]
    t = jnp.arange(ll)
    row = (t // GRID_W).astype(F32)
    col = (t % GRID_W).astype(F32)
    n_freq = MLA_ROPE // 4
    inv = ROPE_THETA ** (-jnp.arange(n_freq, dtype=F32) / n_freq)
    ang = jnp.stack([row[:, None] * inv, col[:, None] * inv], axis=1)
    cos, sin = jnp.cos(ang), jnp.sin(ang)
    cos64 = jnp.concatenate([cos, cos], axis=-1).reshape(ll, MLA_ROPE)
    sin64 = jnp.concatenate([-sin, sin], axis=-1).reshape(ll, MLA_ROPE)
    pad = jnp.zeros((ll, LANES - MLA_ROPE), F32)
    cos_l = jnp.tile(jnp.concatenate([cos64, pad], axis=-1), (dims["b_lat"], 1))
    sin_l = jnp.tile(jnp.concatenate([sin64, pad], axis=-1), (dims["b_lat"], 1))
    cos_c = jnp.concatenate([jnp.ones((dims["t_ctx"], MLA_ROPE), F32),
                             jnp.zeros((dims["t_ctx"], LANES - MLA_ROPE), F32)], axis=-1)
    sin_c = jnp.zeros((dims["t_ctx"], LANES), F32)
    return jnp.concatenate([cos_c, cos_l], axis=0), jnp.concatenate([sin_c, sin_l], axis=0)


def kernel(x_prompt, x_sample, c, cache_mla_ckv, cache_mla_krope, state_gla, c_ctx, w_ada, b_ada, g_norm1, g_norm2, w_in, w_gla_dec, b_gla_dec, g_gla, w_pool, pool_scale, g_mla_qa, w_mla_q_up, g_mla_kva, w_mla_kv_up, g_q_nope, g_q_rope, g_k_nope, g_k_rope, w_br_gla, w_br_pool, w_br_mla, w_o, w_group_router, b_group_router, w_expert_router, b_expert_router, w_exp_gate, w_exp_up, w_exp_down):
    b_ctx, l_ctx, _ = x_prompt.shape
    b_lat, l_lat, _ = x_sample.shape
    dims = dict(b_ctx=b_ctx, l_ctx=l_ctx, b_lat=b_lat, l_lat=l_lat,
                t_ctx=b_ctx * l_ctx, t_lat=b_lat * l_lat, t=b_ctx * l_ctx + b_lat * l_lat)
    t_ctx = dims["t_ctx"]
    assert l_ctx % SLAB == 0 and l_lat % SLAB == 0 and t_ctx % l_lat == 0 and 1 + b_lat <= 8

    wp, wdec, bdec, wq, g_qr, g_kr, router_w, router_b = _pack_params(
        w_in, w_gla_dec, b_gla_dec, w_mla_q_up, g_q_rope, g_k_rope,
        w_group_router, b_group_router, w_expert_router, b_expert_router)
    wkv = w_mla_kv_up.astype(BF16)
    r3 = lambda a: a.reshape(DEPTH, 1, a.shape[-1])
    gains = (r3(g_mla_qa), r3(g_mla_kva), r3(g_q_nope), g_qr, r3(g_k_nope), g_kr)
    w = dict(w_br_gla=w_br_gla.astype(BF16), w_br_pool=w_br_pool.astype(BF16),
             w_br_mla=w_br_mla.astype(BF16), w_o=w_o.astype(BF16), g_gla=r3(g_gla),
             g_norm2=r3(g_norm2), router_w=router_w, router_b=router_b,
             w_exp_gate=w_exp_gate.astype(BF16), w_exp_up=w_exp_up.astype(BF16),
             w_exp_down=w_exp_down.astype(BF16))
    w_pool_b = w_pool.astype(BF16)
    pool_scale3 = r3(pool_scale)
    g1 = r3(g_norm1)

    cond = jnp.concatenate([c_ctx[None, :], c, jnp.zeros((8 - 1 - b_lat, D), F32)], axis=0)
    mods = _modulation(cond, w_ada, b_ada)

    cos, sin = _rope_tables(dims)
    cache_kr = jnp.pad(cache_mla_krope, ((0, 0), (0, 0), (0, 0), (0, LANES - MLA_ROPE)))
    kc, vc = _cache_kv(cache_mla_ckv, cache_kr, wkv, r3(g_k_nope))
    gla_tables = _gla_tables(dims)

    x = jnp.concatenate([x_prompt.reshape(t_ctx, D), x_sample.reshape(dims["t_lat"], D)], axis=0)
    ckv_l, kr_l, st_l = [], [], []
    for l in range(DEPTH):
        p = _proj_in(x, mods, g1, wp, l, dims)
        s0 = jnp.concatenate([jnp.zeros((1, 2, GLA_HEADS, GLA_DV, GLA_DK), F32),
                              jnp.swapaxes(state_gla[:, l], -1, -2)], axis=0)
        o_f, o_b, s_fin = _gla(p, wdec[l], bdec[l], s0, gla_tables, dims)
        pool_out = _pool(p, w_pool_b, pool_scale3, l, dims)
        qr, qu, k, v, ckv, kro = _mla_prep(p, cos, sin, wq, wkv, gains, l, dims)
        attn = jnp.concatenate([_attn_ctx(qr, k, v, dims),
                                _attn_lat(qr, qu, k, v, kc, vc, l, dims)], axis=0)
        x_mid, route = _merge(x, mods, o_f, o_b, p, pool_out, attn, w, l, dims)
        x = _moe(x_mid, route, mods, w, l, dims)
        ckv_l.append(ckv[:t_ctx].reshape(b_ctx, l_ctx, MLA_KV_LORA))
        kr_l.append(kro[:t_ctx, :MLA_ROPE].reshape(b_ctx, l_ctx, MLA_ROPE))
        st_l.append(jnp.swapaxes(s_fin[:b_ctx], -1, -2))

    y_prompt = x[:t_ctx].reshape(b_ctx, l_ctx, D)
    y_sample = x[t_ctx:dims["t"]].reshape(b_lat, l_lat, D)
    return (y_prompt, y_sample, jnp.stack(ckv_l, axis=1), jnp.stack(kr_l, axis=1),
            jnp.stack(st_l, axis=1))
```

```python
import functools
import math

import numpy as np
import jax
import jax.numpy as jnp
from jax import lax
from jax.experimental import pallas as pl
from jax.experimental.pallas import tpu as pltpu

F32 = jnp.float32
BF16 = jnp.bfloat16
I32 = jnp.int32

D = 2048
DEPTH = 4
EPS = 1e-6
GRID_W = 64
GLA_HEADS, GLA_DK, GLA_DV, GLA_RANK, GLA_TAU, GLA_CHUNK = 4, 128, 256, 16, 16.0, 64
POOL_WINDOWS = (2, 4, 8, 16)
POOL_GROUP_DIM = 256
MLA_HEADS, MLA_Q_LORA, MLA_KV_LORA, MLA_NOPE, MLA_ROPE, MLA_V = 8, 768, 512, 128, 64, 128
MLA_SCALE = 1.0 / math.sqrt(MLA_NOPE + MLA_ROPE)
ROPE_THETA = 10000.0
N_GROUPS, EXPERTS_PER_GROUP, EXPERT_FF = 4, 4, 512
SPLIT_SIZES = (512, 512, 1024, 1024, 32, 1024, 768, 512, 64, 6144)

LANES = 128
SLAB = 256
VMEM_LIMIT_BYTES = 60000 * 1024

C_GL = 0
C_Q = 6144
C_K = 6656
C_V = 7168
C_GR = 8192
C_PIN = 9216
C_KVA = 10240
C_QA = 10752
C_GLOW = 11520
C_KR = 11648
NP = 11776


def _dot(a, b):
    return jnp.dot(a, b, preferred_element_type=F32)


def _dot_nt(a, b):
    return lax.dot_general(a, b, (((1,), (1,)), ((), ())), preferred_element_type=F32)


def _dot_tn(a, b):
    return lax.dot_general(a, b, (((0,), (0,)), ((), ())), preferred_element_type=F32)


def _split3(x):
    hi = x.astype(BF16)
    r1 = x - hi.astype(F32)
    mid = r1.astype(BF16)
    lo = (r1 - mid.astype(F32)).astype(BF16)
    return hi, mid, lo


def _sigmoid(x):
    return 0.5 * jnp.tanh(0.5 * x) + 0.5


def _silu(x):
    return x * _sigmoid(x)


def _rms(x, n):
    ms = jnp.sum(x * x, axis=-1, keepdims=True) * (1.0 / n)
    return x * lax.rsqrt(ms + EPS)


def _cparams(sem, vmem=VMEM_LIMIT_BYTES):
    return pltpu.CompilerParams(dimension_semantics=sem, vmem_limit_bytes=vmem)


def _pick_tile(cands, *extents):
    for c in cands:
        if all(e % c == 0 for e in extents):
            return c
    raise ValueError(f"no tile in {cands} divides {extents}")


def _mod_kernel(c_ref, w_ref, b_ref, o_ref):
    a = _silu(c_ref[...]).astype(BF16)
    o_ref[0] = _dot(a, w_ref[0].astype(BF16)) + b_ref[0]


def _modulation(cond, w_ada, b_ada):
    tn = 1024
    out = pl.pallas_call(
        _mod_kernel,
        grid=(DEPTH, 6 * D // tn),
        in_specs=[
            pl.BlockSpec((8, D), lambda l, j: (0, 0)),
            pl.BlockSpec((1, D, tn), lambda l, j: (l, 0, j)),
            pl.BlockSpec((1, 1, tn), lambda l, j: (l, 0, j)),
        ],
        out_specs=pl.BlockSpec((1, 8, tn), lambda l, j: (l, 0, j)),
        out_shape=jax.ShapeDtypeStruct((DEPTH, 8, 6 * D), F32),
        compiler_params=_cparams(("parallel", "parallel"), 40 * 1024 * 1024),
        name="adaln_modulation",
    )(cond, w_ada, b_ada.reshape(DEPTH, 1, 6 * D))
    return out.reshape(DEPTH, 8, 6, D)


def _proj_in_kernel(x_ref, mod_ref, g_ref, w_ref, o_ref, h_scr, *, tm):
    @pl.when(pl.program_id(1) == 0)
    def _():
        m = mod_ref[0, 0]
        g = g_ref[0]

        def body(r, carry):
            rows = pl.ds(pl.multiple_of(r * LANES, LANES), LANES)
            y = _rms(x_ref[rows, :], D) * g
            h_scr[rows, :] = (y * (1.0 + m[1:2]) + m[0:1]).astype(BF16)
            return carry

        lax.fori_loop(0, tm // LANES, body, 0)

    o_ref[...] = _dot(h_scr[...], w_ref[0]).astype(BF16)


def _proj_in(x, mods, g1, wp, l, dims):
    T = dims["t"]
    tm = _pick_tile((1024, 512, 256), dims["t_ctx"], dims["l_lat"])
    tn = 512
    gmap = _group_map(dims, tm)
    return pl.pallas_call(
        functools.partial(_proj_in_kernel, tm=tm),
        grid=(T // tm, NP // tn),
        in_specs=[
            pl.BlockSpec((tm, D), lambda i, j: (i, 0)),
            pl.BlockSpec((1, 1, 6, D), lambda i, j: (l, gmap(i), 0, 0)),
            pl.BlockSpec((1, 1, D), lambda i, j: (l, 0, 0)),
            pl.BlockSpec((1, D, tn), lambda i, j: (l, 0, j)),
        ],
        out_specs=pl.BlockSpec((tm, tn), lambda i, j: (i, j)),
        out_shape=jax.ShapeDtypeStruct((T, NP), BF16),
        scratch_shapes=[pltpu.VMEM((tm, D), BF16)],
        compiler_params=_cparams(("parallel", "arbitrary")),
        name="norm1_proj_in",
    )(x, mods, g1, wp)


def _group_map(dims, tm):
    n_ctx_tiles = dims["t_ctx"] // tm
    per_lat = dims["l_lat"] // tm

    def gmap(i):
        return jnp.where(i < n_ctx_tiles, 0, 1 + (i - n_ctx_tiles) // per_lat)

    return gmap


def _gla_kernel(fblk, bblk, first, last, sidx, unit,
                qf, kf, vf, gf, qb, kb, vb, gb, wdec, bdec, s0,
                of, ob, sout, s_scr):
    del fblk, bblk, sidx, unit
    s = pl.program_id(0)

    @pl.when(first[s] == 1)
    def _():
        s_scr[...] = s0[0]

    n_chunks = SLAB // GLA_CHUNK
    r = lax.broadcasted_iota(I32, (SLAB, SLAB), 0)
    c = lax.broadcasted_iota(I32, (SLAB, SLAB), 1)
    same = (r // GLA_CHUNK) == (c // GLA_CHUNK)

    def rows(x, ci):
        return x[ci * GLA_CHUNK:(ci + 1) * GLA_CHUNK]

    dirs = ((qf, kf, vf, gf, of), (qb, kb, vb, gb, ob))
    for d, (q_ref, k_ref, v_ref, g_ref, o_ref) in enumerate(dirs):
        tri = jnp.logical_and(same, (c <= r) if d == 0 else (c >= r))
        tri_b = tri.astype(BF16)
        dec = _dot(g_ref[...], wdec[d]) + bdec[d]
        la = (jnp.minimum(dec, 0.0) - jnp.log1p(jnp.exp(-jnp.abs(dec)))) * (1.0 / GLA_TAU)
        hi, mid, lo = _split3(la)
        b_all = _dot(tri_b, hi) + _dot(tri_b, mid) + _dot(tri_b, lo)
        edge = GLA_CHUNK - 1 if d == 0 else 0
        tot = [b_all[ci * GLA_CHUNK + edge:ci * GLA_CHUNK + edge + 1] for ci in range(n_chunks)]
        bl_all = jnp.concatenate([jnp.broadcast_to(t, (GLA_CHUNK, t.shape[1])) for t in tot], axis=0)
        order = range(n_chunks) if d == 0 else range(n_chunks - 1, -1, -1)
        for h in range(GLA_HEADS):
            kc = slice(h * GLA_DK, (h + 1) * GLA_DK)
            vc = slice(h * GLA_DV, (h + 1) * GLA_DV)
            b = b_all[:, kc]
            bl = bl_all[:, kc]
            q = q_ref[:, kc].astype(F32) * (GLA_DK ** -0.5)
            k = k_ref[:, kc].astype(F32)
            v = v_ref[:, vc]
            qt = (q * jnp.exp(b)).astype(BF16)
            kt = (k * jnp.exp(-b)).astype(BF16)
            ke = (k * jnp.exp(bl - b)).astype(BF16)
            a = jnp.where(tri, _dot_nt(qt, kt), 0.0).astype(BF16)
            o_intra = _dot(a, v)
            ds_t = [_dot_tn(rows(v, ci), rows(ke, ci)) for ci in range(n_chunks)]
            decay = [jnp.exp(tot[ci][:, kc]) for ci in range(n_chunks)]
            sd = s_scr[d, h]
            o_inter = [None] * n_chunks
            for ci in order:
                o_inter[ci] = _dot_nt(rows(qt, ci), sd.astype(BF16))
                sd = sd * decay[ci] + ds_t[ci]
            o_ref[:, vc] = (o_intra + jnp.concatenate(o_inter, axis=0)).astype(BF16)
            s_scr[d, h] = sd

            @pl.when(last[s] == 1)
            def _(sd=sd, d=d, h=h):
                sout[0, d, h] = sd


def _gla_tables(dims):
    ctx_slabs = dims["l_ctx"] // SLAB
    lat_slabs = dims["l_lat"] // SLAB
    fblk, bblk, first, last, sidx, unit = [], [], [], [], [], []
    base = 0
    for u in range(dims["b_ctx"] + dims["b_lat"]):
        is_ctx = u < dims["b_ctx"]
        n = ctx_slabs if is_ctx else lat_slabs
        for j in range(n):
            fblk.append(base + j)
            bblk.append(base + n - 1 - j)
            first.append(int(j == 0))
            last.append(int(j == n - 1))
            sidx.append(0 if is_ctx else 1 + u - dims["b_ctx"])
            unit.append(u)
        base += n
    return [jnp.asarray(np.asarray(t, np.int32)) for t in (fblk, bblk, first, last, sidx, unit)]


def _gla(p, wdec, bdec, s0, tables, dims):
    T = p.shape[0]
    n_steps = T // SLAB
    n_units = dims["b_ctx"] + dims["b_lat"]

    hk, hv = GLA_HEADS * GLA_DK, GLA_HEADS * GLA_DV
    state = (2, GLA_HEADS, GLA_DV, GLA_DK)

    def pspec(width, col0, which):
        cb = col0 // width
        if which == 0:
            return pl.BlockSpec((SLAB, width), lambda s, fb, bb, fi, la, si, un: (fb[s], cb))
        return pl.BlockSpec((SLAB, width), lambda s, fb, bb, fi, la, si, un: (bb[s], cb))

    in_specs = []
    for which in (0, 1):
        in_specs += [pspec(hk, C_Q, which), pspec(hk, C_K, which), pspec(hv, C_V, which),
                     pspec(LANES, C_GLOW, which)]
    in_specs += [
        pl.BlockSpec((2, LANES, hk), lambda s, *_: (0, 0, 0)),
        pl.BlockSpec((2, 1, hk), lambda s, *_: (0, 0, 0)),
        pl.BlockSpec((1,) + state, lambda s, fb, bb, fi, la, si, un: (si[s], 0, 0, 0, 0)),
    ]
    out_specs = [
        pl.BlockSpec((SLAB, hv), lambda s, fb, bb, fi, la, si, un: (fb[s], 0)),
        pl.BlockSpec((SLAB, hv), lambda s, fb, bb, fi, la, si, un: (bb[s], 0)),
        pl.BlockSpec((1,) + state, lambda s, fb, bb, fi, la, si, un: (un[s], 0, 0, 0, 0)),
    ]
    grid_spec = pltpu.PrefetchScalarGridSpec(
        num_scalar_prefetch=6, grid=(n_steps,),
        in_specs=in_specs, out_specs=out_specs,
        scratch_shapes=[pltpu.VMEM(state, F32)])
    return pl.pallas_call(
        _gla_kernel,
        grid_spec=grid_spec,
        out_shape=[jax.ShapeDtypeStruct((T, hv), BF16),
                   jax.ShapeDtypeStruct((T, hv), BF16),
                   jax.ShapeDtypeStruct((n_units,) + state, F32)],
        compiler_params=_cparams(("arbitrary",), 32 * 1024 * 1024),
        name="gla_bidirectional",
    )(*tables, p, p, p, p, p, p, p, p, wdec, bdec, s0)


def _pool_kernel(cur, prv, nxt, wp, sc, o_ref, *, n_ctx_slabs, ctx_slabs, lat_slabs):
    i = pl.program_id(0)
    is_ctx = i < n_ctx_slabs
    seq_slabs = jnp.where(is_ctx, ctx_slabs, lat_slabs)
    j = jnp.where(is_ctx, i % ctx_slabs, (i - n_ctx_slabs) % lat_slabs)
    has_prev = j > 0
    has_next = j < seq_slabs - 1
    seq_len = seq_slabs * SLAB
    r = lax.broadcasted_iota(I32, (SLAB, SLAB), 0)
    c = lax.broadcasted_iota(I32, (SLAB, SLAB), 1)
    t = j * SLAB + lax.broadcasted_iota(I32, (SLAB, 1), 0)
    gd = POOL_GROUP_DIM
    for g, w in enumerate(POOL_WINDOWS):
        lo_off, hi_off = w // 2, w - w // 2
        cols = slice(g * gd, (g + 1) * gd)
        u = cur[:, cols]
        b_cur = jnp.logical_and(c >= r - lo_off, c < r + hi_off)
        b_prv = jnp.logical_and(c - SLAB >= r - lo_off, has_prev)
        b_nxt = jnp.logical_and(c + SLAB < r + hi_off, has_next)
        ssum = (_dot(b_cur.astype(BF16), u) + _dot(b_prv.astype(BF16), prv[:, cols])
                + _dot(b_nxt.astype(BF16), nxt[:, cols]))
        cnt = (jnp.minimum(t + hi_off, seq_len) - jnp.maximum(t - lo_off, 0)).astype(F32)
        pooled = ssum / cnt - u.astype(F32)
        o_ref[:, cols] = (_dot(pooled.astype(BF16), wp[0, g]) * sc[0, :, cols]).astype(BF16)


def _pool(p, w_pool, pool_scale, l, dims):
    T = p.shape[0]
    n = T // SLAB
    cb = C_PIN // 1024
    kern = functools.partial(_pool_kernel, n_ctx_slabs=dims["t_ctx"] // SLAB,
                             ctx_slabs=dims["l_ctx"] // SLAB, lat_slabs=dims["l_lat"] // SLAB)
    return pl.pallas_call(
        kern,
        grid=(n,),
        in_specs=[
            pl.BlockSpec((SLAB, 1024), lambda i: (i, cb)),
            pl.BlockSpec((SLAB, 1024), lambda i: (jnp.maximum(i - 1, 0), cb)),
            pl.BlockSpec((SLAB, 1024), lambda i: (jnp.minimum(i + 1, n - 1), cb)),
            pl.BlockSpec((1, 4, 256, 256), lambda i: (l, 0, 0, 0)),
            pl.BlockSpec((1, 1, 1024), lambda i: (l, 0, 0)),
        ],
        out_specs=pl.BlockSpec((SLAB, 1024), lambda i: (i, 0)),
        out_shape=jax.ShapeDtypeStruct((T, 1024), BF16),
        compiler_params=_cparams(("parallel",), 32 * 1024 * 1024),
        name="pool_mixer",
    )(p, p, p, w_pool, pool_scale)


def _swap16(x):
    lane = lax.broadcasted_iota(I32, x.shape, x.ndim - 1)
    n = x.shape[-1]
    fwd = pltpu.roll(x, n - 16, x.ndim - 1)
    bwd = pltpu.roll(x, 16, x.ndim - 1)
    return jnp.where((lane % 32) < 16, fwd, bwd)


def _mla_prep_kernel(kva_ref, qa_ref, kr_ref, cos_ref, sin_ref, wq, wkv,
                     g_qa, g_kva, g_qn, g_qr, g_kn, g_kr,
                     qr_ref, qu_ref, k_ref, v_ref, ckv_ref, kro_ref):
    ckv = _rms(kva_ref[...].astype(F32), MLA_KV_LORA) * g_kva[0]
    ckv_ref[...] = ckv
    kv = _dot(ckv.astype(BF16), wkv[0])
    qn = _rms(qa_ref[...].astype(F32), MLA_Q_LORA) * g_qa[0]
    q = _dot(qn.astype(BF16), wq[0])
    cos = cos_ref[...]
    sin = sin_ref[...]
    kr = _rms(kr_ref[...].astype(F32), MLA_ROPE) * g_kr[0]
    kro_ref[...] = kr
    kr_rot = (kr * cos + _swap16(kr) * sin).astype(BF16)
    for h in range(MLA_HEADS):
        c0 = 2 * LANES * h
        q_nope = _rms(q[:, c0:c0 + LANES], MLA_NOPE) * g_qn[0] * MLA_SCALE
        q_rope = _rms(q[:, c0 + LANES:c0 + 2 * LANES], MLA_ROPE) * g_qr[0]
        q_rot = q_rope * cos + _swap16(q_rope) * sin
        qr_ref[:, c0:c0 + LANES] = q_nope.astype(BF16)
        qu_ref[:, c0:c0 + LANES] = q_nope.astype(BF16)
        qr_ref[:, c0 + LANES:c0 + 2 * LANES] = (q_rot * MLA_SCALE).astype(BF16)
        qu_ref[:, c0 + LANES:c0 + 2 * LANES] = (q_rope * MLA_SCALE).astype(BF16)
        k_nope = _rms(kv[:, c0:c0 + LANES], MLA_NOPE) * g_kn[0]
        k_ref[:, c0:c0 + LANES] = k_nope.astype(BF16)
        k_ref[:, c0 + LANES:c0 + 2 * LANES] = kr_rot
        v_ref[:, LANES * h:LANES * (h + 1)] = kv[:, c0 + LANES:c0 + 2 * LANES].astype(BF16)


def _mla_prep(p, cos, sin, wq, wkv, gains, l, dims):
    T = p.shape[0]
    tm = _pick_tile((512, 256), dims["t_ctx"], dims["l_lat"])
    hw = MLA_HEADS * 2 * LANES

    def gspec(n):
        return pl.BlockSpec((1, 1, n), lambda i: (l, 0, 0))

    return pl.pallas_call(
        _mla_prep_kernel,
        grid=(T // tm,),
        in_specs=[
            pl.BlockSpec((tm, MLA_KV_LORA), lambda i: (i, C_KVA // MLA_KV_LORA)),
            pl.BlockSpec((tm, MLA_Q_LORA), lambda i: (i, C_QA // MLA_Q_LORA)),
            pl.BlockSpec((tm, LANES), lambda i: (i, C_KR // LANES)),
            pl.BlockSpec((tm, LANES), lambda i: (i, 0)),
            pl.BlockSpec((tm, LANES), lambda i: (i, 0)),
            pl.BlockSpec((1, MLA_Q_LORA, hw), lambda i: (l, 0, 0)),
            pl.BlockSpec((1, MLA_KV_LORA, hw), lambda i: (l, 0, 0)),
            gspec(MLA_Q_LORA), gspec(MLA_KV_LORA), gspec(LANES), gspec(LANES), gspec(LANES), gspec(LANES),
        ],
        out_specs=[
            pl.BlockSpec((tm, hw), lambda i: (i, 0)),
            pl.BlockSpec((tm, hw), lambda i: (i, 0)),
            pl.BlockSpec((tm, hw), lambda i: (i, 0)),
            pl.BlockSpec((tm, MLA_HEADS * MLA_V), lambda i: (i, 0)),
            pl.BlockSpec((tm, MLA_KV_LORA), lambda i: (i, 0)),
            pl.BlockSpec((tm, LANES), lambda i: (i, 0)),
        ],
        out_shape=[
            jax.ShapeDtypeStruct((T, hw), BF16),
            jax.ShapeDtypeStruct((T, hw), BF16),
            jax.ShapeDtypeStruct((T, hw), BF16),
            jax.ShapeDtypeStruct((T, MLA_HEADS * MLA_V), BF16),
            jax.ShapeDtypeStruct((T, MLA_KV_LORA), F32),
            jax.ShapeDtypeStruct((T, LANES), F32),
        ],
        compiler_params=_cparams(("parallel",), 48 * 1024 * 1024),
        name="mla_prep",
    )(p, p, p, cos, sin, wq, wkv, *gains)


def _cache_kv_kernel(ckv_ref, kr_ref, wkv, g_kn, k_ref, v_ref):
    kv = _dot(ckv_ref[0, 0].astype(BF16), wkv[0])
    kr = kr_ref[0, 0].astype(BF16)
    for h in range(MLA_HEADS):
        c0 = 2 * LANES * h
        k_nope = _rms(kv[:, c0:c0 + LANES], MLA_NOPE) * g_kn[0]
        k_ref[0, 0, :, c0:c0 + LANES] = k_nope.astype(BF16)
        k_ref[0, 0, :, c0 + LANES:c0 + 2 * LANES] = kr
        v_ref[0, 0, :, LANES * h:LANES * (h + 1)] = kv[:, c0 + LANES:c0 + 2 * LANES].astype(BF16)


def _cache_kv(cache_ckv, cache_kr, wkv, g_kn):
    b_lat, _, past, _ = cache_ckv.shape
    hw = MLA_HEADS * 2 * LANES
    return pl.pallas_call(
        _cache_kv_kernel,
        grid=(DEPTH, b_lat),
        in_specs=[
            pl.BlockSpec((1, 1, past, MLA_KV_LORA), lambda l, b: (b, l, 0, 0)),
            pl.BlockSpec((1, 1, past, LANES), lambda l, b: (b, l, 0, 0)),
            pl.BlockSpec((1, MLA_KV_LORA, hw), lambda l, b: (l, 0, 0)),
            pl.BlockSpec((1, 1, LANES), lambda l, b: (l, 0, 0)),
        ],
        out_specs=[
            pl.BlockSpec((1, 1, past, hw), lambda l, b: (l, b, 0, 0)),
            pl.BlockSpec((1, 1, past, MLA_HEADS * MLA_V), lambda l, b: (l, b, 0, 0)),
        ],
        out_shape=[
            jax.ShapeDtypeStruct((DEPTH, b_lat, past, hw), BF16),
            jax.ShapeDtypeStruct((DEPTH, b_lat, past, MLA_HEADS * MLA_V), BF16),
        ],
        compiler_params=_cparams(("parallel", "parallel"), 32 * 1024 * 1024),
        name="mla_cache_decompress",
    )(cache_ckv, cache_kr, wkv, g_kn)


def _attn_ctx_kernel(q_ref, k_ref, v_ref, o_ref):
    for h in range(MLA_HEADS):
        qk = slice(h * 2 * LANES, (h + 1) * 2 * LANES)
        vs = slice(h * MLA_V, (h + 1) * MLA_V)
        s = _dot_nt(q_ref[:, qk], k_ref[:, qk])
        m = jnp.max(s, axis=-1, keepdims=True)
        pr = jnp.exp(s - m)
        den = jnp.sum(pr, axis=-1, keepdims=True)
        o_ref[:, vs] = (_dot(pr.astype(BF16), v_ref[:, vs]) / den).astype(BF16)


def _attn_ctx(q, k, v, dims):
    lc = dims["l_ctx"]
    t_ctx = dims["t_ctx"]
    hw = MLA_HEADS * 2 * LANES
    return pl.pallas_call(
        _attn_ctx_kernel,
        grid=(dims["b_ctx"],),
        in_specs=[
            pl.BlockSpec((lc, hw), lambda s: (s, 0)),
            pl.BlockSpec((lc, hw), lambda s: (s, 0)),
            pl.BlockSpec((lc, MLA_HEADS * MLA_V), lambda s: (s, 0)),
        ],
        out_specs=pl.BlockSpec((lc, MLA_HEADS * MLA_V), lambda s: (s, 0)),
        out_shape=jax.ShapeDtypeStruct((t_ctx, MLA_HEADS * MLA_V), BF16),
        compiler_params=_cparams(("parallel",), 32 * 1024 * 1024),
        name="mla_attention_context",
    )(q, k, v)


ATTN_HEADS_PER_STEP = 2


def _attn_lat_kernel(qr_ref, qu_ref, k_ref, v_ref, kc_ref, vc_ref, o_ref):
    for h in range(ATTN_HEADS_PER_STEP):
        qk = slice(h * 2 * LANES, (h + 1) * 2 * LANES)
        vs = slice(h * MLA_V, (h + 1) * MLA_V)
        s1 = _dot_nt(qr_ref[:, qk], k_ref[:, qk])
        s2 = _dot_nt(qu_ref[:, qk], kc_ref[0, 0, :, qk])
        m = jnp.maximum(jnp.max(s1, axis=-1, keepdims=True), jnp.max(s2, axis=-1, keepdims=True))
        p1 = jnp.exp(s1 - m)
        p2 = jnp.exp(s2 - m)
        den = jnp.sum(p1, axis=-1, keepdims=True) + jnp.sum(p2, axis=-1, keepdims=True)
        o = _dot(p1.astype(BF16), v_ref[:, vs]) + _dot(p2.astype(BF16), vc_ref[0, 0, :, vs])
        o_ref[:, vs] = (o / den).astype(BF16)


def _attn_lat(qr, qu, k, v, kc, vc, l, dims):
    ll = dims["l_lat"]
    past = kc.shape[2]
    tq = _pick_tile((256,), ll)
    q0 = dims["t_ctx"] // tq
    k0 = dims["t_ctx"] // ll
    nq = ll // tq
    hp = ATTN_HEADS_PER_STEP
    qw, vw = hp * 2 * LANES, hp * MLA_V
    return pl.pallas_call(
        _attn_lat_kernel,
        grid=(dims["b_lat"], MLA_HEADS // hp, nq),
        in_specs=[
            pl.BlockSpec((tq, qw), lambda b, h, i: (q0 + b * nq + i, h)),
            pl.BlockSpec((tq, qw), lambda b, h, i: (q0 + b * nq + i, h)),
            pl.BlockSpec((ll, qw), lambda b, h, i: (k0 + b, h)),
            pl.BlockSpec((ll, vw), lambda b, h, i: (k0 + b, h)),
            pl.BlockSpec((1, 1, past, qw), lambda b, h, i: (l, b, 0, h)),
            pl.BlockSpec((1, 1, past, vw), lambda b, h, i: (l, b, 0, h)),
        ],
        out_specs=pl.BlockSpec((tq, vw), lambda b, h, i: (b * nq + i, h)),
        out_shape=jax.ShapeDtypeStruct((dims["b_lat"] * ll, MLA_HEADS * MLA_V), BF16),
        compiler_params=_cparams(("parallel", "parallel", "arbitrary")),
        name="mla_attention_latent",
    )(qr, qu, k, v, kc, vc)


def _route(logits_t):
    ng, ne = N_GROUPS, EXPERTS_PER_GROUP
    lg = [logits_t[g:g + 1] for g in range(ng)]
    mg = functools.reduce(jnp.maximum, lg)
    zg = functools.reduce(lambda a, b: a + b, [jnp.exp(x - mg) for x in lg])
    pg_top = 1.0 / zg
    grp = jnp.full_like(mg, float(ng))
    for g in range(ng - 1, -1, -1):
        grp = jnp.where(lg[g] == mg, float(g), grp)
    el = []
    for j in range(ne):
        acc = jnp.zeros_like(mg)
        for g in range(ng):
            row = ng + g * ne + j
            acc = jnp.where(grp == float(g), logits_t[row:row + 1], acc)
        el.append(acc)
    m1 = functools.reduce(jnp.maximum, el)
    i1 = jnp.full_like(mg, float(ne))
    for j in range(ne - 1, -1, -1):
        i1 = jnp.where(el[j] == m1, float(j), i1)
    neg = jnp.full_like(mg, -jnp.inf)
    rest = [jnp.where(i1 == float(j), neg, el[j]) for j in range(ne)]
    m2 = functools.reduce(jnp.maximum, rest)
    i2 = jnp.full_like(mg, float(ne))
    for j in range(ne - 1, -1, -1):
        i2 = jnp.where(rest[j] == m2, float(j), i2)
    e2 = jnp.exp(m2 - m1)
    w1 = 1.0 / (1.0 + e2)
    w2 = e2 / (1.0 + e2)
    rows = [pg_top * (jnp.where(i1 == float(j), w1, 0.0) + jnp.where(i2 == float(j), w2, 0.0))
            for j in range(ne)]
    rows += [grp, jnp.zeros_like(mg), jnp.zeros_like(mg), jnp.zeros_like(mg)]
    return jnp.concatenate(rows, axis=0)


def _branch_kernel(of_ref, ob_ref, gr_ref, pool_ref, attn_ref, gl0, gl1, gl2,
                   wbg, wbp, wbm, g_gla, o_ref, a_scr):
    gg = g_gla[0]
    for h in range(GLA_HEADS):
        cols = slice(h * GLA_DV, (h + 1) * GLA_DV)
        o = of_ref[:, cols].astype(F32) + ob_ref[:, cols].astype(F32)
        a_scr[:, cols] = (_rms(o, GLA_DV) * gg * _silu(gr_ref[:, cols].astype(F32))).astype(BF16)
    nb = 512
    for n in range(D // nb):
        cols = slice(n * nb, (n + 1) * nb)
        acc = _sigmoid(gl0[:, cols].astype(F32)) * _dot(a_scr[...], wbg[0, :, cols])
        acc += _sigmoid(gl1[:, cols].astype(F32)) * _dot(pool_ref[...], wbp[0, :, cols])
        acc += _sigmoid(gl2[:, cols].astype(F32)) * _dot(attn_ref[...], wbm[0, :, cols])
        o_ref[:, cols] = acc.astype(BF16)


def _branch_merge(o_f, o_b, p, pool_out, attn, w, l, dims):
    T = dims["t"]
    tm = _pick_tile((512, 256), dims["t_ctx"], dims["l_lat"])

    def wspec(shape):
        return pl.BlockSpec((1,) + shape, lambda i: (l,) + (0,) * len(shape), pipeline_mode=pl.Buffered(1))

    return pl.pallas_call(
        _branch_kernel,
        grid=(T // tm,),
        in_specs=[
            pl.BlockSpec((tm, 1024), lambda i: (i, 0)),
            pl.BlockSpec((tm, 1024), lambda i: (i, 0)),
            pl.BlockSpec((tm, 1024), lambda i: (i, C_GR // 1024)),
            pl.BlockSpec((tm, 1024), lambda i: (i, 0)),
            pl.BlockSpec((tm, 1024), lambda i: (i, 0)),
            pl.BlockSpec((tm, D), lambda i: (i, 0)),
            pl.BlockSpec((tm, D), lambda i: (i, 1)),
            pl.BlockSpec((tm, D), lambda i: (i, 2)),
            wspec((1024, D)), wspec((1024, D)), wspec((1024, D)),
            pl.BlockSpec((1, 1, GLA_DV), lambda i: (l, 0, 0)),
        ],
        out_specs=pl.BlockSpec((tm, D), lambda i: (i, 0)),
        out_shape=jax.ShapeDtypeStruct((T, D), BF16),
        scratch_shapes=[pltpu.VMEM((tm, 1024), BF16)],
        compiler_params=_cparams(("parallel",)),
        name="branch_merge",
    )(o_f, o_b, p, pool_out, attn, p, p, p,
      w["w_br_gla"], w["w_br_pool"], w["w_br_mla"], w["g_gla"])


def _merge_kernel(x_ref, mod_ref, mg_ref, wo, g2_ref, wr, br, xo_ref, r_ref, *, tm):
    m = mod_ref[0, 0]
    xm = x_ref[...] + m[2:3] * _dot(mg_ref[...], wo[0])
    xo_ref[...] = xm
    g2 = g2_ref[0]
    for ci in range(tm // LANES):
        rows = slice(ci * LANES, (ci + 1) * LANES)
        h2 = _rms(xm[rows], D) * g2 * (1.0 + m[4:5]) + m[3:4]
        h_hi = h2.astype(BF16)
        hs = (h_hi, (h2 - h_hi.astype(F32)).astype(BF16))
        lt = br[0]
        for ia, ib in ((0, 0), (0, 1), (1, 0)):
            lt = lt + _dot_nt(wr[0, ia], hs[ib])
        r_ref[:, rows] = _route(lt)


def _merge(x, mods, merged, w, l, dims):
    T = dims["t"]
    tm = _pick_tile((512, 256), dims["t_ctx"], dims["l_lat"])
    gmap = _group_map(dims, tm)
    return pl.pallas_call(
        functools.partial(_merge_kernel, tm=tm),
        grid=(T // tm,),
        in_specs=[
            pl.BlockSpec((tm, D), lambda i: (i, 0)),
            pl.BlockSpec((1, 1, 6, D), lambda i: (l, gmap(i), 0, 0)),
            pl.BlockSpec((tm, D), lambda i: (i, 0)),
            pl.BlockSpec((1, D, D), lambda i: (l, 0, 0), pipeline_mode=pl.Buffered(1)),
            pl.BlockSpec((1, 1, D), lambda i: (l, 0, 0)),
            pl.BlockSpec((1, 2, 32, D), lambda i: (l, 0, 0, 0)),
            pl.BlockSpec((1, 32, LANES), lambda i: (l, 0, 0)),
        ],
        out_specs=[
            pl.BlockSpec((tm, D), lambda i: (i, 0)),
            pl.BlockSpec((8, tm), lambda i: (0, i)),
        ],
        out_shape=[jax.ShapeDtypeStruct((T, D), F32), jax.ShapeDtypeStruct((8, T), F32)],
        compiler_params=_cparams(("parallel",)),
        name="merge_out_route",
    )(x, mods, merged, w["w_o"], w["g_norm2"], w["router_w"], w["router_b"])


def _moe_kernel(tg, nvalid, gi_ref, gn_ref, si_ref, x_hbm, cm_ref, mod_ref, g2_ref, wg, wu, wd,
                o_hbm, xbuf, obuf, h_scr, gsem, ssem, *, tm, nt, n_cond):
    del tg
    t = pl.program_id(0)
    slot = t % 2

    def used(i):
        return nvalid[jnp.clip(i, 0, nt - 1)] > 0

    def gather_copy(sl, r, row):
        return pltpu.make_async_copy(x_hbm.at[pl.ds(row, 1)], xbuf.at[sl, pl.ds(r, 1)], gsem.at[sl])

    def scatter_copy(sl, r, row):
        return pltpu.make_async_copy(obuf.at[sl, pl.ds(r, 1)], o_hbm.at[pl.ds(row, 1)], ssem.at[sl])

    def start_rows(make, idx_ref):
        for r in range(tm):
            make(r, idx_ref[0, 0, r]).start(priority=r % 2)

    def wait_gather(sl):
        pltpu.make_async_copy(x_hbm.at[pl.ds(0, tm)], xbuf.at[sl], gsem.at[sl]).wait()

    def wait_scatter(sl):
        pltpu.make_async_copy(obuf.at[sl], o_hbm.at[pl.ds(0, tm)], ssem.at[sl]).wait()

    @pl.when(t == 0)
    def _():
        obuf[1] = jnp.zeros((tm, D), F32)
        n_tok = o_hbm.shape[0] - N_GROUPS * tm
        for g in range(N_GROUPS):
            pltpu.make_async_copy(obuf.at[1], o_hbm.at[pl.ds(n_tok + g * tm, tm)], ssem.at[1]).start()
        for g in range(N_GROUPS):
            wait_scatter(1)

    @pl.when(jnp.logical_and(t == 0, used(0)))
    def _():
        start_rows(functools.partial(gather_copy, 0), gi_ref)

    @pl.when(jnp.logical_and(t + 1 < nt, used(t + 1)))
    def _():
        start_rows(functools.partial(gather_copy, 1 - slot), gn_ref)

    @pl.when(jnp.logical_and(t >= 2, used(t - 2)))
    def _():
        wait_scatter(slot)

    @pl.when(used(t))
    def _():
        wait_gather(slot)
        mods = mod_ref[0]
        g2 = g2_ref[0]

        def cond_rows(cg, k):
            out = mods[0, k:k + 1]
            for c in range(1, n_cond):
                out = jnp.where(cg == float(c), mods[c, k:k + 1], out)
            return out

        for r in range(tm // LANES):
            rows = slice(r * LANES, (r + 1) * LANES)
            cg = cm_ref[rows, 4:5]
            h2 = _rms(xbuf[slot, rows, :], D) * g2 * (1.0 + cond_rows(cg, 4)) + cond_rows(cg, 3)
            h_scr[rows, :] = h2.astype(BF16)

        hb = h_scr[...]
        y = None
        for e in range(EXPERTS_PER_GROUP):
            hid = _silu(_dot(hb, wg[0, e])) * _dot(hb, wu[0, e]) * cm_ref[:, e:e + 1]
            part = _dot(hid.astype(BF16), wd[0, e])
            y = part if y is None else y + part
        for r in range(tm // LANES):
            rows = slice(r * LANES, (r + 1) * LANES)
            cg = cm_ref[rows, 4:5]
            obuf[slot, rows, :] = xbuf[slot, rows, :] + cond_rows(cg, 5) * y[rows]
        start_rows(functools.partial(scatter_copy, slot), si_ref)

    @pl.when(jnp.logical_and(t == nt - 1, used(t - 1)))
    def _():
        wait_scatter(1 - slot)

    @pl.when(jnp.logical_and(t == nt - 1, used(t)))
    def _():
        wait_scatter(slot)


def _moe_plan(route, dims, tm):
    T = route.shape[1]
    nt = T // tm + N_GROUPS
    grp = route[4].astype(I32)
    _, order, *comb_sorted = lax.sort(
        (grp, jnp.arange(T, dtype=I32), route[0], route[1], route[2], route[3]), num_keys=1, is_stable=True)
    gids = jnp.arange(N_GROUPS, dtype=I32)
    counts = jnp.sum(grp[None, :] == gids[:, None], axis=1).astype(I32)
    tiles = (counts + tm - 1) // tm
    tile_end = jnp.cumsum(tiles)
    tile_start = tile_end - tiles
    tok_start = jnp.cumsum(counts) - counts
    tidx = jnp.arange(nt, dtype=I32)
    tg = jnp.minimum(jnp.sum(tidx[:, None] >= tile_end[None, :], axis=1), N_GROUPS - 1).astype(I32)
    onehot = (tg[:, None] == gids[None, :]).astype(I32)
    used = tidx < tile_end[-1]
    in_group = (tidx - jnp.sum(onehot * tile_start[None, :], axis=1)) * tm
    nvalid = jnp.where(used, jnp.clip(jnp.sum(onehot * counts[None, :], axis=1) - in_group, 0, tm), 0)
    nvalid = nvalid.astype(I32)
    slot = jnp.arange(tm, dtype=I32)[None, :]
    valid = slot < nvalid[:, None]
    shift = tile_start * tm - tok_start
    pad = jnp.zeros((nt * tm - T,), I32)
    ints = jnp.concatenate([order, pad])
    flts = jnp.concatenate([jnp.stack(comb_sorted, axis=0), jnp.zeros((4, nt * tm - T), F32)], axis=1)
    slot_g = jnp.broadcast_to(tg[:, None], (nt, tm)).reshape(nt * tm)
    src = jnp.zeros((nt * tm,), I32)
    comb = jnp.zeros((4, nt * tm), F32)
    for g in range(N_GROUPS):
        src = jnp.where(slot_g == g, jnp.roll(ints, shift[g]), src)
        comb = jnp.where((slot_g == g)[None, :], jnp.roll(flts, shift[g], axis=1), comb)
    src = jnp.where(valid, src.reshape(nt, tm), 0)
    comb = jnp.where(valid[None], comb.reshape(4, nt, tm), 0.0)
    dst = jnp.where(valid, src, T + tg[:, None] * tm + slot).astype(I32)
    cond = jnp.where(src < dims["t_ctx"], 0, 1 + (src - dims["t_ctx"]) // dims["l_lat"]).astype(F32)
    cm = jnp.concatenate([comb, cond[None], jnp.zeros((3,) + cond.shape, F32)], axis=0)
    cm = cm.transpose(1, 2, 0).reshape(nt * tm, 8)
    return tg, nvalid, src.reshape(nt, 1, tm), dst.reshape(nt, 1, tm), cm


def _moe_tile(dims):
    return 512 if dims["t"] % 512 == 0 else 256


def _moe(x, route, mods, w, l, dims):
    T = dims["t"]
    tm = _moe_tile(dims)
    nt = T // tm + N_GROUPS
    tg, nvalid, src, dst, cm = _moe_plan(route, dims, tm)
    ff = EXPERT_FF

    def wspec(shape):
        return pl.BlockSpec((1, EXPERTS_PER_GROUP) + shape, lambda t, tg_, nv: (l, tg_[t], 0, 0),
                            pipeline_mode=pl.Buffered(1))

    grid_spec = pltpu.PrefetchScalarGridSpec(
        num_scalar_prefetch=2, grid=(nt,),
        in_specs=[
            pl.BlockSpec((1, 1, tm), lambda t, tg_, nv: (t, 0, 0), memory_space=pltpu.SMEM),
            pl.BlockSpec((1, 1, tm), lambda t, tg_, nv: (jnp.minimum(t + 1, nt - 1), 0, 0),
                         memory_space=pltpu.SMEM),
            pl.BlockSpec((1, 1, tm), lambda t, tg_, nv: (t, 0, 0), memory_space=pltpu.SMEM),
            pl.BlockSpec(memory_space=pl.ANY),
            pl.BlockSpec((tm, 8), lambda t, tg_, nv: (t, 0)),
            pl.BlockSpec((1, 8, 6, D), lambda t, tg_, nv: (l, 0, 0, 0)),
            pl.BlockSpec((1, 1, D), lambda t, tg_, nv: (l, 0, 0)),
            wspec((D, ff)), wspec((D, ff)), wspec((ff, D)),
        ],
        out_specs=pl.BlockSpec(memory_space=pl.ANY),
        scratch_shapes=[pltpu.VMEM((2, tm, D), F32), pltpu.VMEM((2, tm, D), F32), pltpu.VMEM((tm, D), BF16),
                        pltpu.SemaphoreType.DMA((2,)), pltpu.SemaphoreType.DMA((2,))])
    return pl.pallas_call(
        functools.partial(_moe_kernel, tm=tm, nt=nt, n_cond=1 + dims["b_lat"]),
        grid_spec=grid_spec,
        out_shape=jax.ShapeDtypeStruct((T + N_GROUPS * tm, D), F32),
        compiler_params=_cparams(("arbitrary",)),
        name="moe_group_experts",
    )(tg, nvalid, src, src, dst, x, cm, mods, w["g_norm2"], w["w_exp_gate"], w["w_exp_up"], w["w_exp_down"])


def _pack_params(w_in, w_gla_dec, b_gla_dec, w_mla_q_up, g_q_rope, g_k_rope,
                 w_group_router, b_group_router, w_expert_router, b_expert_router):
    gq, gk, gv, gr, glow, pin, qa, kva, kr, gl = jnp.split(
        w_in, [int(v) for v in np.cumsum(SPLIT_SIZES)[:-1]], axis=-1)

    def padc(a, n):
        return jnp.pad(a, ((0, 0), (0, 0), (0, n - a.shape[-1])))

    pieces = [gl, gq, gk, gv, gr, pin, kva, qa, padc(glow, LANES), padc(kr, LANES)]
    wp = jnp.concatenate([a.astype(BF16) for a in pieces], axis=-1)
    wdec = jnp.zeros((DEPTH, 2, LANES, GLA_HEADS * GLA_DK), F32)
    for d in range(2):
        wdec = wdec.at[:, d, d * GLA_RANK:(d + 1) * GLA_RANK, :].set(w_gla_dec[:, d])
    wdec = wdec.astype(BF16)
    bdec = b_gla_dec.reshape(DEPTH, 2, 1, GLA_HEADS * GLA_DK)
    wq = w_mla_q_up.reshape(DEPTH, MLA_Q_LORA, MLA_HEADS, MLA_NOPE + MLA_ROPE)
    wq = jnp.pad(wq, ((0, 0), (0, 0), (0, 0), (0, 2 * LANES - MLA_NOPE - MLA_ROPE)))
    wq = wq.reshape(DEPTH, MLA_Q_LORA, MLA_HEADS * 2 * LANES).astype(BF16)
    g_qr = jnp.pad(g_q_rope, ((0, 0), (0, LANES - MLA_ROPE))).reshape(DEPTH, 1, LANES)
    g_kr = jnp.pad(g_k_rope, ((0, 0), (0, LANES - MLA_ROPE))).reshape(DEPTH, 1, LANES)
    wr = jnp.concatenate([w_group_router, w_expert_router], axis=-1).transpose(0, 2, 1)
    wr = jnp.pad(wr, ((0, 0), (0, 32 - wr.shape[1]), (0, 0)))
    hi = wr.astype(BF16)
    lo = (wr - hi.astype(F32)).astype(BF16)
    router_w = jnp.stack([hi, lo], axis=1)
    rb = jnp.concatenate([b_group_router, b_expert_router], axis=-1)
    rb = jnp.pad(rb, ((0, 0), (0, 32 - rb.shape[1])))
    router_b = jnp.broadcast_to(rb[:, :, None], (DEPTH, 32, LANES))
    return wp, wdec, bdec, wq, g_qr, g_kr, router_w, router_b


def _rope_tables(dims):
    ll = dims["l_lat"]
    t = jnp.arange(ll)
    row = (t // GRID_W).astype(F32)
    col = (t % GRID_W).astype(F32)
    n_freq = MLA_ROPE // 4
    inv = ROPE_THETA ** (-jnp.arange(n_freq, dtype=F32) / n_freq)
    ang = jnp.stack([row[:, None] * inv, col[:, None] * inv], axis=1)
    cos, sin = jnp.cos(ang), jnp.sin(ang)
    cos64 = jnp.concatenate([cos, cos], axis=-1).reshape(ll, MLA_ROPE)
    sin64 = jnp.concatenate([-sin, sin], axis=-1).reshape(ll, MLA_ROPE)
    pad = jnp.zeros((ll, LANES - MLA_ROPE), F32)
    cos_l = jnp.tile(jnp.concatenate([cos64, pad], axis=-1), (dims["b_lat"], 1))
    sin_l = jnp.tile(jnp.concatenate([sin64, pad], axis=-1), (dims["b_lat"], 1))
    cos_c = jnp.concatenate([jnp.ones((dims["t_ctx"], MLA_ROPE), F32),
                             jnp.zeros((dims["t_ctx"], LANES - MLA_ROPE), F32)], axis=-1)
    sin_c = jnp.zeros((dims["t_ctx"], LANES), F32)
    return jnp.concatenate([cos_c, cos_l], axis=0), jnp.concatenate([sin_c, sin_l], axis=0)


def kernel(x_prompt, x_sample, c, cache_mla_ckv, cache_mla_krope, state_gla, c_ctx, w_ada, b_ada, g_norm1, g_norm2, w_in, w_gla_dec, b_gla_dec, g_gla, w_pool, pool_scale, g_mla_qa, w_mla_q_up, g_mla_kva, w_mla_kv_up, g_q_nope, g_q_rope, g_k_nope, g_k_rope, w_br_gla, w_br_pool, w_br_mla, w_o, w_group_router, b_group_router, w_expert_router, b_expert_router, w_exp_gate, w_exp_up, w_exp_down):
    b_ctx, l_ctx, _ = x_prompt.shape
    b_lat, l_lat, _ = x_sample.shape
    dims = dict(b_ctx=b_ctx, l_ctx=l_ctx, b_lat=b_lat, l_lat=l_lat,
                t_ctx=b_ctx * l_ctx, t_lat=b_lat * l_lat, t=b_ctx * l_ctx + b_lat * l_lat)
    t_ctx = dims["t_ctx"]
    assert l_ctx % SLAB == 0 and l_lat % SLAB == 0 and t_ctx % l_lat == 0 and 1 + b_lat <= 8

    wp, wdec, bdec, wq, g_qr, g_kr, router_w, router_b = _pack_params(
        w_in, w_gla_dec, b_gla_dec, w_mla_q_up, g_q_rope, g_k_rope,
        w_group_router, b_group_router, w_expert_router, b_expert_router)
    wkv = w_mla_kv_up.astype(BF16)
    r3 = lambda a: a.reshape(DEPTH, 1, a.shape[-1])
    gains = (r3(g_mla_qa), r3(g_mla_kva), r3(g_q_nope), g_qr, r3(g_k_nope), g_kr)
    w = dict(w_br_gla=w_br_gla.astype(BF16), w_br_pool=w_br_pool.astype(BF16),
             w_br_mla=w_br_mla.astype(BF16), w_o=w_o.astype(BF16), g_gla=r3(g_gla),
             g_norm2=r3(g_norm2), router_w=router_w, router_b=router_b,
             w_exp_gate=w_exp_gate.astype(BF16), w_exp_up=w_exp_up.astype(BF16),
             w_exp_down=w_exp_down.astype(BF16))
    w_pool_b = w_pool.astype(BF16)
    pool_scale3 = r3(pool_scale)
    g1 = r3(g_norm1)

    cond = jnp.concatenate([c_ctx[None, :], c, jnp.zeros((8 - 1 - b_lat, D), F32)], axis=0)
    mods = _modulation(cond, w_ada, b_ada)

    cos, sin = _rope_tables(dims)
    cache_kr = jnp.pad(cache_mla_krope, ((0, 0), (0, 0), (0, 0), (0, LANES - MLA_ROPE)))
    kc, vc = _cache_kv(cache_mla_ckv, cache_kr, wkv, r3(g_k_nope))
    gla_tables = _gla_tables(dims)

    x = jnp.concatenate([x_prompt.reshape(t_ctx, D), x_sample.reshape(dims["t_lat"], D)], axis=0)
    ckv_l, kr_l, st_l = [], [], []
    for l in range(DEPTH):
        p = _proj_in(x, mods, g1, wp, l, dims)
        s0 = jnp.concatenate([jnp.zeros((1, 2, GLA_HEADS, GLA_DV, GLA_DK), F32),
                              jnp.swapaxes(state_gla[:, l], -1, -2)], axis=0)
        o_f, o_b, s_fin = _gla(p, wdec[l], bdec[l], s0, gla_tables, dims)
        pool_out = _pool(p, w_pool_b, pool_scale3, l, dims)
        qr, qu, k, v, ckv, kro = _mla_prep(p, cos, sin, wq, wkv, gains, l, dims)
        attn = jnp.concatenate([_attn_ctx(qr, k, v, dims),
                                _attn_lat(qr, qu, k, v, kc, vc, l, dims)], axis=0)
        merged = _branch_merge(o_f, o_b, p, pool_out, attn, w, l, dims)
        x_mid, route = _merge(x, mods, merged, w, l, dims)
        x = _moe(x_mid, route, mods, w, l, dims)
        ckv_l.append(ckv[:t_ctx].reshape(b_ctx, l_ctx, MLA_KV_LORA))
        kr_l.append(kro[:t_ctx, :MLA_ROPE].reshape(b_ctx, l_ctx, MLA_ROPE))
        st_l.append(jnp.swapaxes(s_fin[:b_ctx], -1, -2))

    y_prompt = x[:t_ctx].reshape(b_ctx, l_ctx, D)
    y_sample = x[t_ctx:dims["t"]].reshape(b_lat, l_lat, D)
    return (y_prompt, y_sample, jnp.stack(ckv_l, axis=1), jnp.stack(kr_l, axis=1),
            jnp.stack(st_l, axis=1))
```

```python
import functools
import math

import numpy as np
import jax
import jax.numpy as jnp
from jax import lax
from jax.experimental import pallas as pl
from jax.experimental.pallas import tpu as pltpu

F32 = jnp.float32
BF16 = jnp.bfloat16
I32 = jnp.int32

D = 2048
DEPTH = 4
EPS = 1e-6
GRID_W = 64
GLA_HEADS, GLA_DK, GLA_DV, GLA_RANK, GLA_TAU, GLA_CHUNK = 4, 128, 256, 16, 16.0, 64
POOL_WINDOWS = (2, 4, 8, 16)
POOL_GROUP_DIM = 256
MLA_HEADS, MLA_Q_LORA, MLA_KV_LORA, MLA_NOPE, MLA_ROPE, MLA_V = 8, 768, 512, 128, 64, 128
MLA_SCALE = 1.0 / math.sqrt(MLA_NOPE + MLA_ROPE)
ROPE_THETA = 10000.0
N_GROUPS, EXPERTS_PER_GROUP, EXPERT_FF = 4, 4, 512
SPLIT_SIZES = (512, 512, 1024, 1024, 32, 1024, 768, 512, 64, 6144)

LANES = 128
SLAB = 256
VMEM_LIMIT_BYTES = 60000 * 1024

C_GL = 0
C_Q = 6144
C_K = 6656
C_V = 7168
C_GR = 8192
C_PIN = 9216
C_KVA = 10240
C_QA = 10752
C_GLOW = 11520
C_KR = 11648
NP = 11776


def _dot(a, b):
    return jnp.dot(a, b, preferred_element_type=F32)


def _dot_nt(a, b):
    return lax.dot_general(a, b, (((1,), (1,)), ((), ())), preferred_element_type=F32)


def _dot_tn(a, b):
    return lax.dot_general(a, b, (((0,), (0,)), ((), ())), preferred_element_type=F32)


def _split3(x):
    hi = x.astype(BF16)
    r1 = x - hi.astype(F32)
    mid = r1.astype(BF16)
    lo = (r1 - mid.astype(F32)).astype(BF16)
    return hi, mid, lo


def _sigmoid(x):
    return 0.5 * jnp.tanh(0.5 * x) + 0.5


def _silu(x):
    return x * _sigmoid(x)


def _rms(x, n):
    ms = jnp.sum(x * x, axis=-1, keepdims=True) * (1.0 / n)
    return x * lax.rsqrt(ms + EPS)


def _cparams(sem, vmem=VMEM_LIMIT_BYTES):
    return pltpu.CompilerParams(dimension_semantics=sem, vmem_limit_bytes=vmem)


def _pick_tile(cands, *extents):
    for c in cands:
        if all(e % c == 0 for e in extents):
            return c
    raise ValueError(f"no tile in {cands} divides {extents}")


def _mod_kernel(c_ref, w_ref, b_ref, o_ref):
    a = _silu(c_ref[...]).astype(BF16)
    o_ref[0] = _dot(a, w_ref[0].astype(BF16)) + b_ref[0]


def _modulation(cond, w_ada, b_ada):
    tn = 1024
    out = pl.pallas_call(
        _mod_kernel,
        grid=(DEPTH, 6 * D // tn),
        in_specs=[
            pl.BlockSpec((8, D), lambda l, j: (0, 0)),
            pl.BlockSpec((1, D, tn), lambda l, j: (l, 0, j)),
            pl.BlockSpec((1, 1, tn), lambda l, j: (l, 0, j)),
        ],
        out_specs=pl.BlockSpec((1, 8, tn), lambda l, j: (l, 0, j)),
        out_shape=jax.ShapeDtypeStruct((DEPTH, 8, 6 * D), F32),
        compiler_params=_cparams(("parallel", "parallel"), 40 * 1024 * 1024),
        name="adaln_modulation",
    )(cond, w_ada, b_ada.reshape(DEPTH, 1, 6 * D))
    return out.reshape(DEPTH, 8, 6, D)


def _proj_in_kernel(x_ref, mod_ref, g_ref, w_ref, o_ref, h_scr, *, tm):
    @pl.when(pl.program_id(1) == 0)
    def _():
        m = mod_ref[0, 0]
        g = g_ref[0]

        def body(r, carry):
            rows = pl.ds(pl.multiple_of(r * LANES, LANES), LANES)
            y = _rms(x_ref[rows, :], D) * g
            h_scr[rows, :] = (y * (1.0 + m[1:2]) + m[0:1]).astype(BF16)
            return carry

        lax.fori_loop(0, tm // LANES, body, 0)

    o_ref[...] = _dot(h_scr[...], w_ref[0]).astype(BF16)


def _proj_in(x, mods, g1, wp, l, dims):
    T = dims["t"]
    tm = _pick_tile((1024, 512, 256), dims["t_ctx"], dims["l_lat"])
    tn = 512
    gmap = _group_map(dims, tm)
    return pl.pallas_call(
        functools.partial(_proj_in_kernel, tm=tm),
        grid=(T // tm, NP // tn),
        in_specs=[
            pl.BlockSpec((tm, D), lambda i, j: (i, 0)),
            pl.BlockSpec((1, 1, 6, D), lambda i, j: (l, gmap(i), 0, 0)),
            pl.BlockSpec((1, 1, D), lambda i, j: (l, 0, 0)),
            pl.BlockSpec((1, D, tn), lambda i, j: (l, 0, j)),
        ],
        out_specs=pl.BlockSpec((tm, tn), lambda i, j: (i, j)),
        out_shape=jax.ShapeDtypeStruct((T, NP), BF16),
        scratch_shapes=[pltpu.VMEM((tm, D), BF16)],
        compiler_params=_cparams(("parallel", "arbitrary")),
        name="norm1_proj_in",
    )(x, mods, g1, wp)


def _group_map(dims, tm):
    n_ctx_tiles = dims["t_ctx"] // tm
    per_lat = dims["l_lat"] // tm

    def gmap(i):
        return jnp.where(i < n_ctx_tiles, 0, 1 + (i - n_ctx_tiles) // per_lat)

    return gmap


def _gla_kernel(fblk, bblk, first, last, sidx, unit,
                qf, kf, vf, gf, qb, kb, vb, gb, wdec, bdec, s0,
                of, ob, sout, s_scr):
    del fblk, bblk, sidx, unit
    s = pl.program_id(0)

    @pl.when(first[s] == 1)
    def _():
        s_scr[...] = s0[0]

    n_chunks = SLAB // GLA_CHUNK
    r = lax.broadcasted_iota(I32, (SLAB, SLAB), 0)
    c = lax.broadcasted_iota(I32, (SLAB, SLAB), 1)
    same = (r // GLA_CHUNK) == (c // GLA_CHUNK)

    def rows(x, ci):
        return x[ci * GLA_CHUNK:(ci + 1) * GLA_CHUNK]

    dirs = ((qf, kf, vf, gf, of), (qb, kb, vb, gb, ob))
    for d, (q_ref, k_ref, v_ref, g_ref, o_ref) in enumerate(dirs):
        tri = jnp.logical_and(same, (c <= r) if d == 0 else (c >= r))
        tri_b = tri.astype(BF16)
        dec = _dot(g_ref[...], wdec[d]) + bdec[d]
        la = (jnp.minimum(dec, 0.0) - jnp.log1p(jnp.exp(-jnp.abs(dec)))) * (1.0 / GLA_TAU)
        hi, mid, lo = _split3(la)
        b_all = _dot(tri_b, hi) + _dot(tri_b, mid) + _dot(tri_b, lo)
        edge = GLA_CHUNK - 1 if d == 0 else 0
        tot = [b_all[ci * GLA_CHUNK + edge:ci * GLA_CHUNK + edge + 1] for ci in range(n_chunks)]
        bl_all = jnp.concatenate([jnp.broadcast_to(t, (GLA_CHUNK, t.shape[1])) for t in tot], axis=0)
        order = range(n_chunks) if d == 0 else range(n_chunks - 1, -1, -1)
        for h in range(GLA_HEADS):
            kc = slice(h * GLA_DK, (h + 1) * GLA_DK)
            vc = slice(h * GLA_DV, (h + 1) * GLA_DV)
            b = b_all[:, kc]
            bl = bl_all[:, kc]
            q = q_ref[:, kc].astype(F32) * (GLA_DK ** -0.5)
            k = k_ref[:, kc].astype(F32)
            v = v_ref[:, vc]
            qt = (q * jnp.exp(b)).astype(BF16)
            kt = (k * jnp.exp(-b)).astype(BF16)
            ke = (k * jnp.exp(bl - b)).astype(BF16)
            a = jnp.where(tri, _dot_nt(qt, kt), 0.0).astype(BF16)
            o_intra = _dot(a, v)
            ds_t = [_dot_tn(rows(v, ci), rows(ke, ci)) for ci in range(n_chunks)]
            decay = [jnp.exp(tot[ci][:, kc]) for ci in range(n_chunks)]
            sd = s_scr[d, h]
            o_inter = [None] * n_chunks
            for ci in order:
                o_inter[ci] = _dot_nt(rows(qt, ci), sd.astype(BF16))
                sd = sd * decay[ci] + ds_t[ci]
            o_ref[:, vc] = (o_intra + jnp.concatenate(o_inter, axis=0)).astype(BF16)
            s_scr[d, h] = sd

            @pl.when(last[s] == 1)
            def _(sd=sd, d=d, h=h):
                sout[0, d, h] = sd


def _gla_tables(dims):
    ctx_slabs = dims["l_ctx"] // SLAB
    lat_slabs = dims["l_lat"] // SLAB
    fblk, bblk, first, last, sidx, unit = [], [], [], [], [], []
    base = 0
    for u in range(dims["b_ctx"] + dims["b_lat"]):
        is_ctx = u < dims["b_ctx"]
        n = ctx_slabs if is_ctx else lat_slabs
        for j in range(n):
            fblk.append(base + j)
            bblk.append(base + n - 1 - j)
            first.append(int(j == 0))
            last.append(int(j == n - 1))
            sidx.append(0 if is_ctx else 1 + u - dims["b_ctx"])
            unit.append(u)
        base += n
    return [jnp.asarray(np.asarray(t, np.int32)) for t in (fblk, bblk, first, last, sidx, unit)]


def _gla(p, wdec, bdec, s0, tables, dims):
    T = p.shape[0]
    n_steps = T // SLAB
    n_units = dims["b_ctx"] + dims["b_lat"]

    hk, hv = GLA_HEADS * GLA_DK, GLA_HEADS * GLA_DV
    state = (2, GLA_HEADS, GLA_DV, GLA_DK)

    def pspec(width, col0, which):
        cb = col0 // width
        if which == 0:
            return pl.BlockSpec((SLAB, width), lambda s, fb, bb, fi, la, si, un: (fb[s], cb))
        return pl.BlockSpec((SLAB, width), lambda s, fb, bb, fi, la, si, un: (bb[s], cb))

    in_specs = []
    for which in (0, 1):
        in_specs += [pspec(hk, C_Q, which), pspec(hk, C_K, which), pspec(hv, C_V, which),
                     pspec(LANES, C_GLOW, which)]
    in_specs += [
        pl.BlockSpec((2, LANES, hk), lambda s, *_: (0, 0, 0)),
        pl.BlockSpec((2, 1, hk), lambda s, *_: (0, 0, 0)),
        pl.BlockSpec((1,) + state, lambda s, fb, bb, fi, la, si, un: (si[s], 0, 0, 0, 0)),
    ]
    out_specs = [
        pl.BlockSpec((SLAB, hv), lambda s, fb, bb, fi, la, si, un: (fb[s], 0)),
        pl.BlockSpec((SLAB, hv), lambda s, fb, bb, fi, la, si, un: (bb[s], 0)),
        pl.BlockSpec((1,) + state, lambda s, fb, bb, fi, la, si, un: (un[s], 0, 0, 0, 0)),
    ]
    grid_spec = pltpu.PrefetchScalarGridSpec(
        num_scalar_prefetch=6, grid=(n_steps,),
        in_specs=in_specs, out_specs=out_specs,
        scratch_shapes=[pltpu.VMEM(state, F32)])
    return pl.pallas_call(
        _gla_kernel,
        grid_spec=grid_spec,
        out_shape=[jax.ShapeDtypeStruct((T, hv), BF16),
                   jax.ShapeDtypeStruct((T, hv), BF16),
                   jax.ShapeDtypeStruct((n_units,) + state, F32)],
        compiler_params=_cparams(("arbitrary",), 32 * 1024 * 1024),
        name="gla_bidirectional",
    )(*tables, p, p, p, p, p, p, p, p, wdec, bdec, s0)


def _pool_kernel(cur, prv, nxt, wp, sc, o_ref, *, n_ctx_slabs, ctx_slabs, lat_slabs):
    i = pl.program_id(0)
    is_ctx = i < n_ctx_slabs
    seq_slabs = jnp.where(is_ctx, ctx_slabs, lat_slabs)
    j = jnp.where(is_ctx, i % ctx_slabs, (i - n_ctx_slabs) % lat_slabs)
    has_prev = j > 0
    has_next = j < seq_slabs - 1
    seq_len = seq_slabs * SLAB
    r = lax.broadcasted_iota(I32, (SLAB, SLAB), 0)
    c = lax.broadcasted_iota(I32, (SLAB, SLAB), 1)
    t = j * SLAB + lax.broadcasted_iota(I32, (SLAB, 1), 0)
    gd = POOL_GROUP_DIM
    for g, w in enumerate(POOL_WINDOWS):
        lo_off, hi_off = w // 2, w - w // 2
        cols = slice(g * gd, (g + 1) * gd)
        u = cur[:, cols]
        b_cur = jnp.logical_and(c >= r - lo_off, c < r + hi_off)
        b_prv = jnp.logical_and(c - SLAB >= r - lo_off, has_prev)
        b_nxt = jnp.logical_and(c + SLAB < r + hi_off, has_next)
        ssum = (_dot(b_cur.astype(BF16), u) + _dot(b_prv.astype(BF16), prv[:, cols])
                + _dot(b_nxt.astype(BF16), nxt[:, cols]))
        cnt = (jnp.minimum(t + hi_off, seq_len) - jnp.maximum(t - lo_off, 0)).astype(F32)
        pooled = ssum / cnt - u.astype(F32)
        o_ref[:, cols] = (_dot(pooled.astype(BF16), wp[0, g]) * sc[0, :, cols]).astype(BF16)


def _pool(p, w_pool, pool_scale, l, dims):
    T = p.shape[0]
    n = T // SLAB
    cb = C_PIN // 1024
    kern = functools.partial(_pool_kernel, n_ctx_slabs=dims["t_ctx"] // SLAB,
                             ctx_slabs=dims["l_ctx"] // SLAB, lat_slabs=dims["l_lat"] // SLAB)
    return pl.pallas_call(
        kern,
        grid=(n,),
        in_specs=[
            pl.BlockSpec((SLAB, 1024), lambda i: (i, cb)),
            pl.BlockSpec((SLAB, 1024), lambda i: (jnp.maximum(i - 1, 0), cb)),
            pl.BlockSpec((SLAB, 1024), lambda i: (jnp.minimum(i + 1, n - 1), cb)),
            pl.BlockSpec((1, 4, 256, 256), lambda i: (l, 0, 0, 0)),
            pl.BlockSpec((1, 1, 1024), lambda i: (l, 0, 0)),
        ],
        out_specs=pl.BlockSpec((SLAB, 1024), lambda i: (i, 0)),
        out_shape=jax.ShapeDtypeStruct((T, 1024), BF16),
        compiler_params=_cparams(("parallel",), 32 * 1024 * 1024),
        name="pool_mixer",
    )(p, p, p, w_pool, pool_scale)


def _swap16(x):
    lane = lax.broadcasted_iota(I32, x.shape, x.ndim - 1)
    n = x.shape[-1]
    fwd = pltpu.roll(x, n - 16, x.ndim - 1)
    bwd = pltpu.roll(x, 16, x.ndim - 1)
    return jnp.where((lane % 32) < 16, fwd, bwd)


def _mla_prep_kernel(kva_ref, qa_ref, kr_ref, cos_ref, sin_ref, wq, wkv,
                     g_qa, g_kva, g_qn, g_qr, g_kn, g_kr,
                     qr_ref, qu_ref, k_ref, v_ref, ckv_ref, kro_ref):
    ckv = _rms(kva_ref[...].astype(F32), MLA_KV_LORA) * g_kva[0]
    ckv_ref[...] = ckv
    kv = _dot(ckv.astype(BF16), wkv[0])
    qn = _rms(qa_ref[...].astype(F32), MLA_Q_LORA) * g_qa[0]
    q = _dot(qn.astype(BF16), wq[0])
    cos = cos_ref[...]
    sin = sin_ref[...]
    kr = _rms(kr_ref[...].astype(F32), MLA_ROPE) * g_kr[0]
    kro_ref[...] = kr
    kr_rot = (kr * cos + _swap16(kr) * sin).astype(BF16)
    for h in range(MLA_HEADS):
        c0 = 2 * LANES * h
        q_nope = _rms(q[:, c0:c0 + LANES], MLA_NOPE) * g_qn[0] * MLA_SCALE
        q_rope = _rms(q[:, c0 + LANES:c0 + 2 * LANES], MLA_ROPE) * g_qr[0]
        q_rot = q_rope * cos + _swap16(q_rope) * sin
        qr_ref[:, c0:c0 + LANES] = q_nope.astype(BF16)
        qu_ref[:, c0:c0 + LANES] = q_nope.astype(BF16)
        qr_ref[:, c0 + LANES:c0 + 2 * LANES] = (q_rot * MLA_SCALE).astype(BF16)
        qu_ref[:, c0 + LANES:c0 + 2 * LANES] = (q_rope * MLA_SCALE).astype(BF16)
        k_nope = _rms(kv[:, c0:c0 + LANES], MLA_NOPE) * g_kn[0]
        k_ref[:, c0:c0 + LANES] = k_nope.astype(BF16)
        k_ref[:, c0 + LANES:c0 + 2 * LANES] = kr_rot
        v_ref[:, LANES * h:LANES * (h + 1)] = kv[:, c0 + LANES:c0 + 2 * LANES].astype(BF16)


def _mla_prep(p, cos, sin, wq, wkv, gains, l, dims):
    T = p.shape[0]
    tm = _pick_tile((512, 256), dims["t_ctx"], dims["l_lat"])
    hw = MLA_HEADS * 2 * LANES

    def gspec(n):
        return pl.BlockSpec((1, 1, n), lambda i: (l, 0, 0))

    return pl.pallas_call(
        _mla_prep_kernel,
        grid=(T // tm,),
        in_specs=[
            pl.BlockSpec((tm, MLA_KV_LORA), lambda i: (i, C_KVA // MLA_KV_LORA)),
            pl.BlockSpec((tm, MLA_Q_LORA), lambda i: (i, C_QA // MLA_Q_LORA)),
            pl.BlockSpec((tm, LANES), lambda i: (i, C_KR // LANES)),
            pl.BlockSpec((tm, LANES), lambda i: (i, 0)),
            pl.BlockSpec((tm, LANES), lambda i: (i, 0)),
            pl.BlockSpec((1, MLA_Q_LORA, hw), lambda i: (l, 0, 0)),
            pl.BlockSpec((1, MLA_KV_LORA, hw), lambda i: (l, 0, 0)),
            gspec(MLA_Q_LORA), gspec(MLA_KV_LORA), gspec(LANES), gspec(LANES), gspec(LANES), gspec(LANES),
        ],
        out_specs=[
            pl.BlockSpec((tm, hw), lambda i: (i, 0)),
            pl.BlockSpec((tm, hw), lambda i: (i, 0)),
            pl.BlockSpec((tm, hw), lambda i: (i, 0)),
            pl.BlockSpec((tm, MLA_HEADS * MLA_V), lambda i: (i, 0)),
            pl.BlockSpec((tm, MLA_KV_LORA), lambda i: (i, 0)),
            pl.BlockSpec((tm, LANES), lambda i: (i, 0)),
        ],
        out_shape=[
            jax.ShapeDtypeStruct((T, hw), BF16),
            jax.ShapeDtypeStruct((T, hw), BF16),
            jax.ShapeDtypeStruct((T, hw), BF16),
            jax.ShapeDtypeStruct((T, MLA_HEADS * MLA_V), BF16),
            jax.ShapeDtypeStruct((T, MLA_KV_LORA), F32),
            jax.ShapeDtypeStruct((T, LANES), F32),
        ],
        compiler_params=_cparams(("parallel",), 48 * 1024 * 1024),
        name="mla_prep",
    )(p, p, p, cos, sin, wq, wkv, *gains)


def _cache_kv_kernel(ckv_ref, kr_ref, wkv, g_kn, k_ref, v_ref):
    kv = _dot(ckv_ref[0, 0].astype(BF16), wkv[0])
    kr = kr_ref[0, 0].astype(BF16)
    for h in range(MLA_HEADS):
        c0 = 2 * LANES * h
        k_nope = _rms(kv[:, c0:c0 + LANES], MLA_NOPE) * g_kn[0]
        k_ref[0, 0, :, c0:c0 + LANES] = k_nope.astype(BF16)
        k_ref[0, 0, :, c0 + LANES:c0 + 2 * LANES] = kr
        v_ref[0, 0, :, LANES * h:LANES * (h + 1)] = kv[:, c0 + LANES:c0 + 2 * LANES].astype(BF16)


def _cache_kv(cache_ckv, cache_kr, wkv, g_kn):
    b_lat, _, past, _ = cache_ckv.shape
    hw = MLA_HEADS * 2 * LANES
    return pl.pallas_call(
        _cache_kv_kernel,
        grid=(DEPTH, b_lat),
        in_specs=[
            pl.BlockSpec((1, 1, past, MLA_KV_LORA), lambda l, b: (b, l, 0, 0)),
            pl.BlockSpec((1, 1, past, LANES), lambda l, b: (b, l, 0, 0)),
            pl.BlockSpec((1, MLA_KV_LORA, hw), lambda l, b: (l, 0, 0)),
            pl.BlockSpec((1, 1, LANES), lambda l, b: (l, 0, 0)),
        ],
        out_specs=[
            pl.BlockSpec((1, 1, past, hw), lambda l, b: (l, b, 0, 0)),
            pl.BlockSpec((1, 1, past, MLA_HEADS * MLA_V), lambda l, b: (l, b, 0, 0)),
        ],
        out_shape=[
            jax.ShapeDtypeStruct((DEPTH, b_lat, past, hw), BF16),
            jax.ShapeDtypeStruct((DEPTH, b_lat, past, MLA_HEADS * MLA_V), BF16),
        ],
        compiler_params=_cparams(("parallel", "parallel"), 32 * 1024 * 1024),
        name="mla_cache_decompress",
    )(cache_ckv, cache_kr, wkv, g_kn)


def _attn_ctx_kernel(q_ref, k_ref, v_ref, o_ref):
    for h in range(MLA_HEADS):
        qk = slice(h * 2 * LANES, (h + 1) * 2 * LANES)
        vs = slice(h * MLA_V, (h + 1) * MLA_V)
        s = _dot_nt(q_ref[:, qk], k_ref[:, qk])
        m = jnp.max(s, axis=-1, keepdims=True)
        pr = jnp.exp(s - m)
        den = jnp.sum(pr, axis=-1, keepdims=True)
        o_ref[:, vs] = (_dot(pr.astype(BF16), v_ref[:, vs]) / den).astype(BF16)


def _attn_ctx(q, k, v, dims):
    lc = dims["l_ctx"]
    t_ctx = dims["t_ctx"]
    hw = MLA_HEADS * 2 * LANES
    return pl.pallas_call(
        _attn_ctx_kernel,
        grid=(dims["b_ctx"],),
        in_specs=[
            pl.BlockSpec((lc, hw), lambda s: (s, 0)),
            pl.BlockSpec((lc, hw), lambda s: (s, 0)),
            pl.BlockSpec((lc, MLA_HEADS * MLA_V), lambda s: (s, 0)),
        ],
        out_specs=pl.BlockSpec((lc, MLA_HEADS * MLA_V), lambda s: (s, 0)),
        out_shape=jax.ShapeDtypeStruct((t_ctx, MLA_HEADS * MLA_V), BF16),
        compiler_params=_cparams(("parallel",), 32 * 1024 * 1024),
        name="mla_attention_context",
    )(q, k, v)


ATTN_HEADS_PER_STEP = 2


def _attn_lat_kernel(qr_ref, qu_ref, k_ref, v_ref, kc_ref, vc_ref, o_ref):
    for h in range(ATTN_HEADS_PER_STEP):
        qk = slice(h * 2 * LANES, (h + 1) * 2 * LANES)
        vs = slice(h * MLA_V, (h + 1) * MLA_V)
        s1 = _dot_nt(qr_ref[:, qk], k_ref[:, qk])
        s2 = _dot_nt(qu_ref[:, qk], kc_ref[0, 0, :, qk])
        m = jnp.maximum(jnp.max(s1, axis=-1, keepdims=True), jnp.max(s2, axis=-1, keepdims=True))
        p1 = jnp.exp(s1 - m)
        p2 = jnp.exp(s2 - m)
        den = jnp.sum(p1, axis=-1, keepdims=True) + jnp.sum(p2, axis=-1, keepdims=True)
        o = _dot(p1.astype(BF16), v_ref[:, vs]) + _dot(p2.astype(BF16), vc_ref[0, 0, :, vs])
        o_ref[:, vs] = (o / den).astype(BF16)


def _attn_lat(qr, qu, k, v, kc, vc, l, dims):
    ll = dims["l_lat"]
    past = kc.shape[2]
    tq = _pick_tile((256,), ll)
    q0 = dims["t_ctx"] // tq
    k0 = dims["t_ctx"] // ll
    nq = ll // tq
    hp = ATTN_HEADS_PER_STEP
    qw, vw = hp * 2 * LANES, hp * MLA_V
    return pl.pallas_call(
        _attn_lat_kernel,
        grid=(dims["b_lat"], MLA_HEADS // hp, nq),
        in_specs=[
            pl.BlockSpec((tq, qw), lambda b, h, i: (q0 + b * nq + i, h)),
            pl.BlockSpec((tq, qw), lambda b, h, i: (q0 + b * nq + i, h)),
            pl.BlockSpec((ll, qw), lambda b, h, i: (k0 + b, h)),
            pl.BlockSpec((ll, vw), lambda b, h, i: (k0 + b, h)),
            pl.BlockSpec((1, 1, past, qw), lambda b, h, i: (l, b, 0, h)),
            pl.BlockSpec((1, 1, past, vw), lambda b, h, i: (l, b, 0, h)),
        ],
        out_specs=pl.BlockSpec((tq, vw), lambda b, h, i: (b * nq + i, h)),
        out_shape=jax.ShapeDtypeStruct((dims["b_lat"] * ll, MLA_HEADS * MLA_V), BF16),
        compiler_params=_cparams(("parallel", "parallel", "arbitrary")),
        name="mla_attention_latent",
    )(qr, qu, k, v, kc, vc)


def _route(logits_t):
    ng, ne = N_GROUPS, EXPERTS_PER_GROUP
    lg = [logits_t[g:g + 1] for g in range(ng)]
    mg = functools.reduce(jnp.maximum, lg)
    zg = functools.reduce(lambda a, b: a + b, [jnp.exp(x - mg) for x in lg])
    pg_top = 1.0 / zg
    grp = jnp.full_like(mg, float(ng))
    for g in range(ng - 1, -1, -1):
        grp = jnp.where(lg[g] == mg, float(g), grp)
    el = []
    for j in range(ne):
        acc = jnp.zeros_like(mg)
        for g in range(ng):
            row = ng + g * ne + j
            acc = jnp.where(grp == float(g), logits_t[row:row + 1], acc)
        el.append(acc)
    m1 = functools.reduce(jnp.maximum, el)
    i1 = jnp.full_like(mg, float(ne))
    for j in range(ne - 1, -1, -1):
        i1 = jnp.where(el[j] == m1, float(j), i1)
    neg = jnp.full_like(mg, -jnp.inf)
    rest = [jnp.where(i1 == float(j), neg, el[j]) for j in range(ne)]
    m2 = functools.reduce(jnp.maximum, rest)
    i2 = jnp.full_like(mg, float(ne))
    for j in range(ne - 1, -1, -1):
        i2 = jnp.where(rest[j] == m2, float(j), i2)
    e2 = jnp.exp(m2 - m1)
    w1 = 1.0 / (1.0 + e2)
    w2 = e2 / (1.0 + e2)
    rows = [pg_top * (jnp.where(i1 == float(j), w1, 0.0) + jnp.where(i2 == float(j), w2, 0.0))
            for j in range(ne)]
    rows += [grp, jnp.zeros_like(mg), jnp.zeros_like(mg), jnp.zeros_like(mg)]
    return jnp.concatenate(rows, axis=0)


def _branch_kernel(of_ref, ob_ref, gr_ref, pool_ref, attn_ref, gl0, gl1, gl2,
                   wbg, wbp, wbm, g_gla, o_ref, a_scr):
    gg = g_gla[0]
    for h in range(GLA_HEADS):
        cols = slice(h * GLA_DV, (h + 1) * GLA_DV)
        o = of_ref[:, cols].astype(F32) + ob_ref[:, cols].astype(F32)
        a_scr[:, cols] = (_rms(o, GLA_DV) * gg * _silu(gr_ref[:, cols].astype(F32))).astype(BF16)
    nb = 512
    for n in range(D // nb):
        cols = slice(n * nb, (n + 1) * nb)
        acc = _sigmoid(gl0[:, cols].astype(F32)) * _dot(a_scr[...], wbg[0, :, cols])
        acc += _sigmoid(gl1[:, cols].astype(F32)) * _dot(pool_ref[...], wbp[0, :, cols])
        acc += _sigmoid(gl2[:, cols].astype(F32)) * _dot(attn_ref[...], wbm[0, :, cols])
        o_ref[:, cols] = acc.astype(BF16)


def _branch_merge(o_f, o_b, p, pool_out, attn, w, l, dims):
    T = dims["t"]
    tm = _pick_tile((512, 256), dims["t_ctx"], dims["l_lat"])

    def wspec(shape):
        return pl.BlockSpec((1,) + shape, lambda i: (l,) + (0,) * len(shape), pipeline_mode=pl.Buffered(1))

    return pl.pallas_call(
        _branch_kernel,
        grid=(T // tm,),
        in_specs=[
            pl.BlockSpec((tm, 1024), lambda i: (i, 0)),
            pl.BlockSpec((tm, 1024), lambda i: (i, 0)),
            pl.BlockSpec((tm, 1024), lambda i: (i, C_GR // 1024)),
            pl.BlockSpec((tm, 1024), lambda i: (i, 0)),
            pl.BlockSpec((tm, 1024), lambda i: (i, 0)),
            pl.BlockSpec((tm, D), lambda i: (i, 0)),
            pl.BlockSpec((tm, D), lambda i: (i, 1)),
            pl.BlockSpec((tm, D), lambda i: (i, 2)),
            wspec((1024, D)), wspec((1024, D)), wspec((1024, D)),
            pl.BlockSpec((1, 1, GLA_DV), lambda i: (l, 0, 0)),
        ],
        out_specs=pl.BlockSpec((tm, D), lambda i: (i, 0)),
        out_shape=jax.ShapeDtypeStruct((T, D), BF16),
        scratch_shapes=[pltpu.VMEM((tm, 1024), BF16)],
        compiler_params=_cparams(("parallel",)),
        name="branch_merge",
    )(o_f, o_b, p, pool_out, attn, p, p, p,
      w["w_br_gla"], w["w_br_pool"], w["w_br_mla"], w["g_gla"])


def _merge_kernel(x_ref, mod_ref, mg_ref, wo, g2_ref, wr, br, xo_ref, r_ref, *, tm):
    m = mod_ref[0, 0]
    xm = x_ref[...] + m[2:3] * _dot(mg_ref[...], wo[0])
    xo_ref[...] = xm
    g2 = g2_ref[0]
    for ci in range(tm // LANES):
        rows = slice(ci * LANES, (ci + 1) * LANES)
        h2 = _rms(xm[rows], D) * g2 * (1.0 + m[4:5]) + m[3:4]
        h_hi = h2.astype(BF16)
        hs = (h_hi, (h2 - h_hi.astype(F32)).astype(BF16))
        lt = br[0]
        for ia, ib in ((0, 0), (0, 1), (1, 0)):
            lt = lt + _dot_nt(wr[0, ia], hs[ib])
        r_ref[:, rows] = _route(lt)


def _merge(x, mods, merged, w, l, dims):
    T = dims["t"]
    tm = _pick_tile((512, 256), dims["t_ctx"], dims["l_lat"])
    gmap = _group_map(dims, tm)
    return pl.pallas_call(
        functools.partial(_merge_kernel, tm=tm),
        grid=(T // tm,),
        in_specs=[
            pl.BlockSpec((tm, D), lambda i: (i, 0)),
            pl.BlockSpec((1, 1, 6, D), lambda i: (l, gmap(i), 0, 0)),
            pl.BlockSpec((tm, D), lambda i: (i, 0)),
            pl.BlockSpec((1, D, D), lambda i: (l, 0, 0), pipeline_mode=pl.Buffered(1)),
            pl.BlockSpec((1, 1, D), lambda i: (l, 0, 0)),
            pl.BlockSpec((1, 2, 32, D), lambda i: (l, 0, 0, 0)),
            pl.BlockSpec((1, 32, LANES), lambda i: (l, 0, 0)),
        ],
        out_specs=[
            pl.BlockSpec((tm, D), lambda i: (i, 0)),
            pl.BlockSpec((8, tm), lambda i: (0, i)),
        ],
        out_shape=[jax.ShapeDtypeStruct((T, D), F32), jax.ShapeDtypeStruct((8, T), F32)],
        compiler_params=_cparams(("parallel",)),
        name="merge_out_route",
    )(x, mods, merged, w["w_o"], w["g_norm2"], w["router_w"], w["router_b"])


def _moe_kernel(tg, nvalid, gi_ref, gn_ref, si_ref, x_hbm, cm_ref, mod_ref, g2_ref, wg, wu, wd,
                o_hbm, xbuf, obuf, h_scr, gsem, ssem, *, tm, nt, n_cond):
    del tg
    t = pl.program_id(0)
    slot = t % 2

    def used(i):
        return nvalid[jnp.clip(i, 0, nt - 1)] > 0

    def gather_copy(sl, r, row):
        return pltpu.make_async_copy(x_hbm.at[pl.ds(row, 1)], xbuf.at[sl, pl.ds(r, 1)], gsem.at[sl])

    def scatter_copy(sl, r, row):
        return pltpu.make_async_copy(obuf.at[sl, pl.ds(r, 1)], o_hbm.at[pl.ds(row, 1)], ssem.at[sl])

    def start_rows(make, idx_ref):
        for r in range(tm):
            make(r, idx_ref[0, 0, r]).start(priority=r % 2)

    def wait_gather(sl):
        pltpu.make_async_copy(x_hbm.at[pl.ds(0, tm)], xbuf.at[sl], gsem.at[sl]).wait()

    def wait_scatter(sl):
        pltpu.make_async_copy(obuf.at[sl], o_hbm.at[pl.ds(0, tm)], ssem.at[sl]).wait()

    @pl.when(t == 0)
    def _():
        obuf[1] = jnp.zeros((tm, D), F32)
        n_tok = o_hbm.shape[0] - N_GROUPS * tm
        for g in range(N_GROUPS):
            pltpu.make_async_copy(obuf.at[1], o_hbm.at[pl.ds(n_tok + g * tm, tm)], ssem.at[1]).start()
        for g in range(N_GROUPS):
            wait_scatter(1)

    @pl.when(jnp.logical_and(t == 0, used(0)))
    def _():
        start_rows(functools.partial(gather_copy, 0), gi_ref)

    has_next = jnp.logical_and(t + 1 < nt, used(t + 1))

    @pl.when(jnp.logical_and(t >= 2, used(t - 2)))
    def _():
        wait_scatter(slot)

    @pl.when(used(t))
    def _():
        wait_gather(slot)
        mods = mod_ref[0]
        g2 = g2_ref[0]

        def cond_rows(cg, k):
            out = mods[0, k:k + 1]
            for c in range(1, n_cond):
                out = jnp.where(cg == float(c), mods[c, k:k + 1], out)
            return out

        for r in range(tm // LANES):
            rows = slice(r * LANES, (r + 1) * LANES)
            cg = cm_ref[rows, 4:5]
            h2 = _rms(xbuf[slot, rows, :], D) * g2 * (1.0 + cond_rows(cg, 4)) + cond_rows(cg, 3)
            h_scr[rows, :] = h2.astype(BF16)

        hb = h_scr[...]
        y = None
        per = tm // EXPERTS_PER_GROUP
        for e in range(EXPERTS_PER_GROUP):
            for r in range(e * per, (e + 1) * per):
                gather_copy(1 - slot, r, gn_ref[0, 0, r]).start(priority=r % 2)
            hid = _silu(_dot(hb, wg[0, e])) * _dot(hb, wu[0, e]) * cm_ref[:, e:e + 1]
            part = _dot(hid.astype(BF16), wd[0, e])
            y = part if y is None else y + part
        for r in range(tm // LANES):
            rows = slice(r * LANES, (r + 1) * LANES)
            cg = cm_ref[rows, 4:5]
            obuf[slot, rows, :] = xbuf[slot, rows, :] + cond_rows(cg, 5) * y[rows]
        start_rows(functools.partial(scatter_copy, slot), si_ref)

    @pl.when(jnp.logical_and(used(t), jnp.logical_not(has_next)))
    def _():
        wait_gather(1 - slot)

    @pl.when(jnp.logical_and(t == nt - 1, used(t - 1)))
    def _():
        wait_scatter(1 - slot)

    @pl.when(jnp.logical_and(t == nt - 1, used(t)))
    def _():
        wait_scatter(slot)


def _moe_plan(route, dims, tm):
    T = route.shape[1]
    nt = T // tm + N_GROUPS
    grp = route[4].astype(I32)
    _, order, *comb_sorted = lax.sort(
        (grp, jnp.arange(T, dtype=I32), route[0], route[1], route[2], route[3]), num_keys=1, is_stable=True)
    gids = jnp.arange(N_GROUPS, dtype=I32)
    counts = jnp.sum(grp[None, :] == gids[:, None], axis=1).astype(I32)
    tiles = (counts + tm - 1) // tm
    tile_end = jnp.cumsum(tiles)
    tile_start = tile_end - tiles
    tok_start = jnp.cumsum(counts) - counts
    tidx = jnp.arange(nt, dtype=I32)
    tg = jnp.minimum(jnp.sum(tidx[:, None] >= tile_end[None, :], axis=1), N_GROUPS - 1).astype(I32)
    onehot = (tg[:, None] == gids[None, :]).astype(I32)
    used = tidx < tile_end[-1]
    in_group = (tidx - jnp.sum(onehot * tile_start[None, :], axis=1)) * tm
    nvalid = jnp.where(used, jnp.clip(jnp.sum(onehot * counts[None, :], axis=1) - in_group, 0, tm), 0)
    nvalid = nvalid.astype(I32)
    slot = jnp.arange(tm, dtype=I32)[None, :]
    valid = slot < nvalid[:, None]
    shift = tile_start * tm - tok_start
    pad = jnp.zeros((nt * tm - T,), I32)
    ints = jnp.concatenate([order, pad])
    flts = jnp.concatenate([jnp.stack(comb_sorted, axis=0), jnp.zeros((4, nt * tm - T), F32)], axis=1)
    slot_g = jnp.broadcast_to(tg[:, None], (nt, tm)).reshape(nt * tm)
    src = jnp.zeros((nt * tm,), I32)
    comb = jnp.zeros((4, nt * tm), F32)
    for g in range(N_GROUPS):
        src = jnp.where(slot_g == g, jnp.roll(ints, shift[g]), src)
        comb = jnp.where((slot_g == g)[None, :], jnp.roll(flts, shift[g], axis=1), comb)
    src = jnp.where(valid, src.reshape(nt, tm), 0)
    comb = jnp.where(valid[None], comb.reshape(4, nt, tm), 0.0)
    dst = jnp.where(valid, src, T + tg[:, None] * tm + slot).astype(I32)
    cond = jnp.where(src < dims["t_ctx"], 0, 1 + (src - dims["t_ctx"]) // dims["l_lat"]).astype(F32)
    cm = jnp.concatenate([comb, cond[None], jnp.zeros((3,) + cond.shape, F32)], axis=0)
    cm = cm.transpose(1, 2, 0).reshape(nt * tm, 8)
    return tg, nvalid, src.reshape(nt, 1, tm), dst.reshape(nt, 1, tm), cm


def _moe_tile(dims):
    return 512 if dims["t"] % 512 == 0 else 256


def _moe(x, route, mods, w, l, dims):
    T = dims["t"]
    tm = _moe_tile(dims)
    nt = T // tm + N_GROUPS
    tg, nvalid, src, dst, cm = _moe_plan(route, dims, tm)
    ff = EXPERT_FF

    def wspec(shape):
        return pl.BlockSpec((1, EXPERTS_PER_GROUP) + shape, lambda t, tg_, nv: (l, tg_[t], 0, 0),
                            pipeline_mode=pl.Buffered(1))

    grid_spec = pltpu.PrefetchScalarGridSpec(
        num_scalar_prefetch=2, grid=(nt,),
        in_specs=[
            pl.BlockSpec((1, 1, tm), lambda t, tg_, nv: (t, 0, 0), memory_space=pltpu.SMEM),
            pl.BlockSpec((1, 1, tm), lambda t, tg_, nv: (jnp.minimum(t + 1, nt - 1), 0, 0),
                         memory_space=pltpu.SMEM),
            pl.BlockSpec((1, 1, tm), lambda t, tg_, nv: (t, 0, 0), memory_space=pltpu.SMEM),
            pl.BlockSpec(memory_space=pl.ANY),
            pl.BlockSpec((tm, 8), lambda t, tg_, nv: (t, 0)),
            pl.BlockSpec((1, 8, 6, D), lambda t, tg_, nv: (l, 0, 0, 0)),
            pl.BlockSpec((1, 1, D), lambda t, tg_, nv: (l, 0, 0)),
            wspec((D, ff)), wspec((D, ff)), wspec((ff, D)),
        ],
        out_specs=pl.BlockSpec(memory_space=pl.ANY),
        scratch_shapes=[pltpu.VMEM((2, tm, D), F32), pltpu.VMEM((2, tm, D), F32), pltpu.VMEM((tm, D), BF16),
                        pltpu.SemaphoreType.DMA((2,)), pltpu.SemaphoreType.DMA((2,))])
    return pl.pallas_call(
        functools.partial(_moe_kernel, tm=tm, nt=nt, n_cond=1 + dims["b_lat"]),
        grid_spec=grid_spec,
        out_shape=jax.ShapeDtypeStruct((T + N_GROUPS * tm, D), F32),
        compiler_params=_cparams(("arbitrary",)),
        name="moe_group_experts",
    )(tg, nvalid, src, src, dst, x, cm, mods, w["g_norm2"], w["w_exp_gate"], w["w_exp_up"], w["w_exp_down"])


_SRC_OFF = [0] + [int(v) for v in np.cumsum(SPLIT_SIZES)]
_PACK_MOVES = tuple(zip(_SRC_OFF[:-1], SPLIT_SIZES,
                        (C_Q, C_K, C_V, C_GR, C_GLOW, C_PIN, C_QA, C_KVA, C_KR, C_GL)))


def _pack_w_in_kernel(w_ref, o_ref):
    rows = o_ref.shape[1]
    for dst in (C_GLOW, C_KR):
        o_ref[0, :, dst:dst + LANES] = jnp.zeros((rows, LANES), BF16)
    step = 512
    for src, width, dst in _PACK_MOVES:
        for c in range(0, width, step):
            n = min(step, width - c)
            o_ref[0, :, dst + c:dst + c + n] = w_ref[0, :, src + c:src + c + n].astype(BF16)


def _pack_w_in(w_in):
    d_in = w_in.shape[-1]
    tk = 128
    return pl.pallas_call(
        _pack_w_in_kernel,
        grid=(DEPTH, D // tk),
        in_specs=[pl.BlockSpec((1, tk, d_in), lambda l, i: (l, i, 0))],
        out_specs=pl.BlockSpec((1, tk, NP), lambda l, i: (l, i, 0)),
        out_shape=jax.ShapeDtypeStruct((DEPTH, D, NP), BF16),
        compiler_params=_cparams(("parallel", "parallel"), 48 * 1024 * 1024),
        name="pack_w_in",
    )(w_in)


def _pack_params(w_in, w_gla_dec, b_gla_dec, w_mla_q_up, g_q_rope, g_k_rope,
                 w_group_router, b_group_router, w_expert_router, b_expert_router):
    wp = _pack_w_in(w_in)
    wdec = jnp.zeros((DEPTH, 2, LANES, GLA_HEADS * GLA_DK), F32)
    for d in range(2):
        wdec = wdec.at[:, d, d * GLA_RANK:(d + 1) * GLA_RANK, :].set(w_gla_dec[:, d])
    wdec = wdec.astype(BF16)
    bdec = b_gla_dec.reshape(DEPTH, 2, 1, GLA_HEADS * GLA_DK)
    wq = w_mla_q_up.reshape(DEPTH, MLA_Q_LORA, MLA_HEADS, MLA_NOPE + MLA_ROPE)
    wq = jnp.pad(wq, ((0, 0), (0, 0), (0, 0), (0, 2 * LANES - MLA_NOPE - MLA_ROPE)))
    wq = wq.reshape(DEPTH, MLA_Q_LORA, MLA_HEADS * 2 * LANES).astype(BF16)
    g_qr = jnp.pad(g_q_rope, ((0, 0), (0, LANES - MLA_ROPE))).reshape(DEPTH, 1, LANES)
    g_kr = jnp.pad(g_k_rope, ((0, 0), (0, LANES - MLA_ROPE))).reshape(DEPTH, 1, LANES)
    wr = jnp.concatenate([w_group_router, w_expert_router], axis=-1).transpose(0, 2, 1)
    wr = jnp.pad(wr, ((0, 0), (0, 32 - wr.shape[1]), (0, 0)))
    hi = wr.astype(BF16)
    lo = (wr - hi.astype(F32)).astype(BF16)
    router_w = jnp.stack([hi, lo], axis=1)
    rb = jnp.concatenate([b_group_router, b_expert_router], axis=-1)
    rb = jnp.pad(rb, ((0, 0), (0, 32 - rb.shape[1])))
    router_b = jnp.broadcast_to(rb[:, :, None], (DEPTH, 32, LANES))
    return wp, wdec, bdec, wq, g_qr, g_kr, router_w, router_b


def _rope_tables(dims):
    ll = dims["l_lat"]
    t = jnp.arange(ll)
    row = (t // GRID_W).astype(F32)
    col = (t % GRID_W).astype(F32)
    n_freq = MLA_ROPE // 4
    inv = ROPE_THETA ** (-jnp.arange(n_freq, dtype=F32) / n_freq)
    ang = jnp.stack([row[:, None] * inv, col[:, None] * inv], axis=1)
    cos, sin = jnp.cos(ang), jnp.sin(ang)
    cos64 = jnp.concatenate([cos, cos], axis=-1).reshape(ll, MLA_ROPE)
    sin64 = jnp.concatenate([-sin, sin], axis=-1).reshape(ll, MLA_ROPE)
    pad = jnp.zeros((ll, LANES - MLA_ROPE), F32)
    cos_l = jnp.tile(jnp.concatenate([cos64, pad], axis=-1), (dims["b_lat"], 1))
    sin_l = jnp.tile(jnp.concatenate([sin64, pad], axis=-1), (dims["b_lat"], 1))
    cos_c = jnp.concatenate([jnp.ones((dims["t_ctx"], MLA_ROPE), F32),
                             jnp.zeros((dims["t_ctx"], LANES - MLA_ROPE), F32)], axis=-1)
    sin_c = jnp.zeros((dims["t_ctx"], LANES), F32)
    return jnp.concatenate([cos_c, cos_l], axis=0), jnp.concatenate([sin_c, sin_l], axis=0)


def kernel(x_prompt, x_sample, c, cache_mla_ckv, cache_mla_krope, state_gla, c_ctx, w_ada, b_ada, g_norm1, g_norm2, w_in, w_gla_dec, b_gla_dec, g_gla, w_pool, pool_scale, g_mla_qa, w_mla_q_up, g_mla_kva, w_mla_kv_up, g_q_nope, g_q_rope, g_k_nope, g_k_rope, w_br_gla, w_br_pool, w_br_mla, w_o, w_group_router, b_group_router, w_expert_router, b_expert_router, w_exp_gate, w_exp_up, w_exp_down):
    b_ctx, l_ctx, _ = x_prompt.shape
    b_lat, l_lat, _ = x_sample.shape
    dims = dict(b_ctx=b_ctx, l_ctx=l_ctx, b_lat=b_lat, l_lat=l_lat,
                t_ctx=b_ctx * l_ctx, t_lat=b_lat * l_lat, t=b_ctx * l_ctx + b_lat * l_lat)
    t_ctx = dims["t_ctx"]
    assert l_ctx % SLAB == 0 and l_lat % SLAB == 0 and t_ctx % l_lat == 0 and 1 + b_lat <= 8

    wp, wdec, bdec, wq, g_qr, g_kr, router_w, router_b = _pack_params(
        w_in, w_gla_dec, b_gla_dec, w_mla_q_up, g_q_rope, g_k_rope,
        w_group_router, b_group_router, w_expert_router, b_expert_router)
    wkv = w_mla_kv_up.astype(BF16)
    r3 = lambda a: a.reshape(DEPTH, 1, a.shape[-1])
    gains = (r3(g_mla_qa), r3(g_mla_kva), r3(g_q_nope), g_qr, r3(g_k_nope), g_kr)
    w = dict(w_br_gla=w_br_gla.astype(BF16), w_br_pool=w_br_pool.astype(BF16),
             w_br_mla=w_br_mla.astype(BF16), w_o=w_o.astype(BF16), g_gla=r3(g_gla),
             g_norm2=r3(g_norm2), router_w=router_w, router_b=router_b,
             w_exp_gate=w_exp_gate.astype(BF16), w_exp_up=w_exp_up.astype(BF16),
             w_exp_down=w_exp_down.astype(BF16))
    w_pool_b = w_pool.astype(BF16)
    pool_scale3 = r3(pool_scale)
    g1 = r3(g_norm1)

    cond = jnp.concatenate([c_ctx[None, :], c, jnp.zeros((8 - 1 - b_lat, D), F32)], axis=0)
    mods = _modulation(cond, w_ada, b_ada)

    cos, sin = _rope_tables(dims)
    cache_kr = jnp.pad(cache_mla_krope, ((0, 0), (0, 0), (0, 0), (0, LANES - MLA_ROPE)))
    kc, vc = _cache_kv(cache_mla_ckv, cache_kr, wkv, r3(g_k_nope))
    gla_tables = _gla_tables(dims)

    x = jnp.concatenate([x_prompt.reshape(t_ctx, D), x_sample.reshape(dims["t_lat"], D)], axis=0)
    ckv_l, kr_l, st_l = [], [], []
    for l in range(DEPTH):
        p = _proj_in(x, mods, g1, wp, l, dims)
        s0 = jnp.concatenate([jnp.zeros((1, 2, GLA_HEADS, GLA_DV, GLA_DK), F32),
                              jnp.swapaxes(state_gla[:, l], -1, -2)], axis=0)
        o_f, o_b, s_fin = _gla(p, wdec[l], bdec[l], s0, gla_tables, dims)
        pool_out = _pool(p, w_pool_b, pool_scale3, l, dims)
        qr, qu, k, v, ckv, kro = _mla_prep(p, cos, sin, wq, wkv, gains, l, dims)
        attn = jnp.concatenate([_attn_ctx(qr, k, v, dims),
                                _attn_lat(qr, qu, k, v, kc, vc, l, dims)], axis=0)
        merged = _branch_merge(o_f, o_b, p, pool_out, attn, w, l, dims)
        x_mid, route = _merge(x, mods, merged, w, l, dims)
        x = _moe(x_mid, route, mods, w, l, dims)
        ckv_l.append(ckv[:t_ctx].reshape(b_ctx, l_ctx, MLA_KV_LORA))
        kr_l.append(kro[:t_ctx, :MLA_ROPE].reshape(b_ctx, l_ctx, MLA_ROPE))
        st_l.append(jnp.swapaxes(s_fin[:b_ctx], -1, -2))

    y_prompt = x[:t_ctx].reshape(b_ctx, l_ctx, D)
    y_sample = x[t_ctx:dims["t"]].reshape(b_lat, l_lat, D)
    return (y_prompt, y_sample, jnp.stack(ckv_l, axis=1), jnp.stack(kr_l, axis=1),
            jnp.stack(st_l, axis=1))
```

```python
import functools
import math

import numpy as np
import jax
import jax.numpy as jnp
from jax import lax
from jax.experimental import pallas as pl
from jax.experimental.pallas import tpu as pltpu

F32 = jnp.float32
BF16 = jnp.bfloat16
I32 = jnp.int32

D = 2048
DEPTH = 4
EPS = 1e-6
GRID_W = 64
GLA_HEADS, GLA_DK, GLA_DV, GLA_RANK, GLA_TAU, GLA_CHUNK = 4, 128, 256, 16, 16.0, 64
POOL_WINDOWS = (2, 4, 8, 16)
POOL_GROUP_DIM = 256
MLA_HEADS, MLA_Q_LORA, MLA_KV_LORA, MLA_NOPE, MLA_ROPE, MLA_V = 8, 768, 512, 128, 64, 128
MLA_SCALE = 1.0 / math.sqrt(MLA_NOPE + MLA_ROPE)
ROPE_THETA = 10000.0
N_GROUPS, EXPERTS_PER_GROUP, EXPERT_FF = 4, 4, 512
SPLIT_SIZES = (512, 512, 1024, 1024, 32, 1024, 768, 512, 64, 6144)

LANES = 128
SLAB = 256
VMEM_LIMIT_BYTES = 60000 * 1024

C_GL = 0
C_Q = 6144
C_K = 6656
C_V = 7168
C_GR = 8192
C_PIN = 9216
C_KVA = 10240
C_QA = 10752
C_GLOW = 11520
C_KR = 11648
NP = 11776


def _dot(a, b):
    return jnp.dot(a, b, preferred_element_type=F32)


def _dot_nt(a, b):
    return lax.dot_general(a, b, (((1,), (1,)), ((), ())), preferred_element_type=F32)


def _dot_tn(a, b):
    return lax.dot_general(a, b, (((0,), (0,)), ((), ())), preferred_element_type=F32)


def _split3(x):
    hi = x.astype(BF16)
    r1 = x - hi.astype(F32)
    mid = r1.astype(BF16)
    lo = (r1 - mid.astype(F32)).astype(BF16)
    return hi, mid, lo


def _sigmoid(x):
    return 0.5 * jnp.tanh(0.5 * x) + 0.5


def _silu(x):
    return x * _sigmoid(x)


def _rms(x, n):
    ms = jnp.sum(x * x, axis=-1, keepdims=True) * (1.0 / n)
    return x * lax.rsqrt(ms + EPS)


def _cparams(sem, vmem=VMEM_LIMIT_BYTES):
    return pltpu.CompilerParams(dimension_semantics=sem, vmem_limit_bytes=vmem)


def _pick_tile(cands, *extents):
    for c in cands:
        if all(e % c == 0 for e in extents):
            return c
    raise ValueError(f"no tile in {cands} divides {extents}")


def _mod_kernel(c_ref, w_ref, b_ref, o_ref):
    a = _silu(c_ref[...]).astype(BF16)
    o_ref[0] = _dot(a, w_ref[0].astype(BF16)) + b_ref[0]


def _modulation(cond, w_ada, b_ada):
    tn = 1024
    out = pl.pallas_call(
        _mod_kernel,
        grid=(DEPTH, 6 * D // tn),
        in_specs=[
            pl.BlockSpec((8, D), lambda l, j: (0, 0)),
            pl.BlockSpec((1, D, tn), lambda l, j: (l, 0, j)),
            pl.BlockSpec((1, 1, tn), lambda l, j: (l, 0, j)),
        ],
        out_specs=pl.BlockSpec((1, 8, tn), lambda l, j: (l, 0, j)),
        out_shape=jax.ShapeDtypeStruct((DEPTH, 8, 6 * D), F32),
        compiler_params=_cparams(("parallel", "parallel"), 40 * 1024 * 1024),
        name="adaln_modulation",
    )(cond, w_ada, b_ada.reshape(DEPTH, 1, 6 * D))
    return out.reshape(DEPTH, 8, 6, D)


def _proj_in_kernel(x_ref, mod_ref, g_ref, w_ref, o_ref, h_scr, *, tm):
    @pl.when(pl.program_id(1) == 0)
    def _():
        m = mod_ref[0, 0]
        g = g_ref[0]

        def body(r, carry):
            rows = pl.ds(pl.multiple_of(r * LANES, LANES), LANES)
            y = _rms(x_ref[rows, :], D) * g
            h_scr[rows, :] = (y * (1.0 + m[1:2]) + m[0:1]).astype(BF16)
            return carry

        lax.fori_loop(0, tm // LANES, body, 0)

    o_ref[...] = _dot_nt(h_scr[...], w_ref[0]).astype(BF16)


def _proj_in(x, mods, g1, wp, l, dims):
    T = dims["t"]
    tm = _pick_tile((1024, 512, 256), dims["t_ctx"], dims["l_lat"])
    tn = 512
    gmap = _group_map(dims, tm)
    return pl.pallas_call(
        functools.partial(_proj_in_kernel, tm=tm),
        grid=(T // tm, NP // tn),
        in_specs=[
            pl.BlockSpec((tm, D), lambda i, j: (i, 0)),
            pl.BlockSpec((1, 1, 6, D), lambda i, j: (l, gmap(i), 0, 0)),
            pl.BlockSpec((1, 1, D), lambda i, j: (l, 0, 0)),
            pl.BlockSpec((1, tn, D), lambda i, j: (l, j, 0)),
        ],
        out_specs=pl.BlockSpec((tm, tn), lambda i, j: (i, j)),
        out_shape=jax.ShapeDtypeStruct((T, NP), BF16),
        scratch_shapes=[pltpu.VMEM((tm, D), BF16)],
        compiler_params=_cparams(("parallel", "arbitrary")),
        name="norm1_proj_in",
    )(x, mods, g1, wp)


def _group_map(dims, tm):
    n_ctx_tiles = dims["t_ctx"] // tm
    per_lat = dims["l_lat"] // tm

    def gmap(i):
        return jnp.where(i < n_ctx_tiles, 0, 1 + (i - n_ctx_tiles) // per_lat)

    return gmap


def _gla_kernel(fblk, bblk, first, last, sidx, unit,
                qf, kf, vf, gf, qb, kb, vb, gb, wdec, bdec, s0,
                of, ob, sout, s_scr):
    del fblk, bblk, sidx, unit
    s = pl.program_id(0)

    @pl.when(first[s] == 1)
    def _():
        s_scr[...] = s0[0]

    n_chunks = SLAB // GLA_CHUNK
    r = lax.broadcasted_iota(I32, (SLAB, SLAB), 0)
    c = lax.broadcasted_iota(I32, (SLAB, SLAB), 1)
    same = (r // GLA_CHUNK) == (c // GLA_CHUNK)

    def rows(x, ci):
        return x[ci * GLA_CHUNK:(ci + 1) * GLA_CHUNK]

    dirs = ((qf, kf, vf, gf, of), (qb, kb, vb, gb, ob))
    for d, (q_ref, k_ref, v_ref, g_ref, o_ref) in enumerate(dirs):
        tri = jnp.logical_and(same, (c <= r) if d == 0 else (c >= r))
        tri_b = tri.astype(BF16)
        dec = _dot(g_ref[...], wdec[d]) + bdec[d]
        la = (jnp.minimum(dec, 0.0) - jnp.log1p(jnp.exp(-jnp.abs(dec)))) * (1.0 / GLA_TAU)
        hi, mid, lo = _split3(la)
        b_all = _dot(tri_b, hi) + _dot(tri_b, mid) + _dot(tri_b, lo)
        edge = GLA_CHUNK - 1 if d == 0 else 0
        tot = [b_all[ci * GLA_CHUNK + edge:ci * GLA_CHUNK + edge + 1] for ci in range(n_chunks)]
        bl_all = jnp.concatenate([jnp.broadcast_to(t, (GLA_CHUNK, t.shape[1])) for t in tot], axis=0)
        order = range(n_chunks) if d == 0 else range(n_chunks - 1, -1, -1)
        for h in range(GLA_HEADS):
            kc = slice(h * GLA_DK, (h + 1) * GLA_DK)
            vc = slice(h * GLA_DV, (h + 1) * GLA_DV)
            b = b_all[:, kc]
            bl = bl_all[:, kc]
            q = q_ref[:, kc].astype(F32) * (GLA_DK ** -0.5)
            k = k_ref[:, kc].astype(F32)
            v = v_ref[:, vc]
            qt = (q * jnp.exp(b)).astype(BF16)
            kt = (k * jnp.exp(-b)).astype(BF16)
            ke = (k * jnp.exp(bl - b)).astype(BF16)
            a = jnp.where(tri, _dot_nt(qt, kt), 0.0).astype(BF16)
            o_intra = _dot(a, v)
            ds_t = [_dot_tn(rows(v, ci), rows(ke, ci)) for ci in range(n_chunks)]
            decay = [jnp.exp(tot[ci][:, kc]) for ci in range(n_chunks)]
            sd = s_scr[d, h]
            o_inter = [None] * n_chunks
            for ci in order:
                o_inter[ci] = _dot_nt(rows(qt, ci), sd.astype(BF16))
                sd = sd * decay[ci] + ds_t[ci]
            o_ref[:, vc] = (o_intra + jnp.concatenate(o_inter, axis=0)).astype(BF16)
            s_scr[d, h] = sd

            @pl.when(last[s] == 1)
            def _(sd=sd, d=d, h=h):
                sout[0, d, h] = sd


def _gla_tables(dims):
    ctx_slabs = dims["l_ctx"] // SLAB
    lat_slabs = dims["l_lat"] // SLAB
    fblk, bblk, first, last, sidx, unit = [], [], [], [], [], []
    base = 0
    for u in range(dims["b_ctx"] + dims["b_lat"]):
        is_ctx = u < dims["b_ctx"]
        n = ctx_slabs if is_ctx else lat_slabs
        for j in range(n):
            fblk.append(base + j)
            bblk.append(base + n - 1 - j)
            first.append(int(j == 0))
            last.append(int(j == n - 1))
            sidx.append(0 if is_ctx else 1 + u - dims["b_ctx"])
            unit.append(u)
        base += n
    return [jnp.asarray(np.asarray(t, np.int32)) for t in (fblk, bblk, first, last, sidx, unit)]


def _gla(p, wdec, bdec, s0, tables, dims):
    T = p.shape[0]
    n_steps = T // SLAB
    n_units = dims["b_ctx"] + dims["b_lat"]

    hk, hv = GLA_HEADS * GLA_DK, GLA_HEADS * GLA_DV
    state = (2, GLA_HEADS, GLA_DV, GLA_DK)

    def pspec(width, col0, which):
        cb = col0 // width
        if which == 0:
            return pl.BlockSpec((SLAB, width), lambda s, fb, bb, fi, la, si, un: (fb[s], cb))
        return pl.BlockSpec((SLAB, width), lambda s, fb, bb, fi, la, si, un: (bb[s], cb))

    in_specs = []
    for which in (0, 1):
        in_specs += [pspec(hk, C_Q, which), pspec(hk, C_K, which), pspec(hv, C_V, which),
                     pspec(LANES, C_GLOW, which)]
    in_specs += [
        pl.BlockSpec((2, LANES, hk), lambda s, *_: (0, 0, 0)),
        pl.BlockSpec((2, 1, hk), lambda s, *_: (0, 0, 0)),
        pl.BlockSpec((1,) + state, lambda s, fb, bb, fi, la, si, un: (si[s], 0, 0, 0, 0)),
    ]
    out_specs = [
        pl.BlockSpec((SLAB, hv), lambda s, fb, bb, fi, la, si, un: (fb[s], 0)),
        pl.BlockSpec((SLAB, hv), lambda s, fb, bb, fi, la, si, un: (bb[s], 0)),
        pl.BlockSpec((1,) + state, lambda s, fb, bb, fi, la, si, un: (un[s], 0, 0, 0, 0)),
    ]
    grid_spec = pltpu.PrefetchScalarGridSpec(
        num_scalar_prefetch=6, grid=(n_steps,),
        in_specs=in_specs, out_specs=out_specs,
        scratch_shapes=[pltpu.VMEM(state, F32)])
    return pl.pallas_call(
        _gla_kernel,
        grid_spec=grid_spec,
        out_shape=[jax.ShapeDtypeStruct((T, hv), BF16),
                   jax.ShapeDtypeStruct((T, hv), BF16),
                   jax.ShapeDtypeStruct((n_units,) + state, F32)],
        compiler_params=_cparams(("arbitrary",), 32 * 1024 * 1024),
        name="gla_bidirectional",
    )(*tables, p, p, p, p, p, p, p, p, wdec, bdec, s0)


def _pool_kernel(cur, prv, nxt, wp, sc, o_ref, *, n_ctx_slabs, ctx_slabs, lat_slabs):
    i = pl.program_id(0)
    is_ctx = i < n_ctx_slabs
    seq_slabs = jnp.where(is_ctx, ctx_slabs, lat_slabs)
    j = jnp.where(is_ctx, i % ctx_slabs, (i - n_ctx_slabs) % lat_slabs)
    has_prev = j > 0
    has_next = j < seq_slabs - 1
    seq_len = seq_slabs * SLAB
    r = lax.broadcasted_iota(I32, (SLAB, SLAB), 0)
    c = lax.broadcasted_iota(I32, (SLAB, SLAB), 1)
    t = j * SLAB + lax.broadcasted_iota(I32, (SLAB, 1), 0)
    gd = POOL_GROUP_DIM
    for g, w in enumerate(POOL_WINDOWS):
        lo_off, hi_off = w // 2, w - w // 2
        cols = slice(g * gd, (g + 1) * gd)
        u = cur[:, cols]
        b_cur = jnp.logical_and(c >= r - lo_off, c < r + hi_off)
        b_prv = jnp.logical_and(c - SLAB >= r - lo_off, has_prev)
        b_nxt = jnp.logical_and(c + SLAB < r + hi_off, has_next)
        ssum = (_dot(b_cur.astype(BF16), u) + _dot(b_prv.astype(BF16), prv[:, cols])
                + _dot(b_nxt.astype(BF16), nxt[:, cols]))
        cnt = (jnp.minimum(t + hi_off, seq_len) - jnp.maximum(t - lo_off, 0)).astype(F32)
        pooled = ssum / cnt - u.astype(F32)
        o_ref[:, cols] = (_dot(pooled.astype(BF16), wp[0, g]) * sc[0, :, cols]).astype(BF16)


def _pool(p, w_pool, pool_scale, l, dims):
    T = p.shape[0]
    n = T // SLAB
    cb = C_PIN // 1024
    kern = functools.partial(_pool_kernel, n_ctx_slabs=dims["t_ctx"] // SLAB,
                             ctx_slabs=dims["l_ctx"] // SLAB, lat_slabs=dims["l_lat"] // SLAB)
    return pl.pallas_call(
        kern,
        grid=(n,),
        in_specs=[
            pl.BlockSpec((SLAB, 1024), lambda i: (i, cb)),
            pl.BlockSpec((SLAB, 1024), lambda i: (jnp.maximum(i - 1, 0), cb)),
            pl.BlockSpec((SLAB, 1024), lambda i: (jnp.minimum(i + 1, n - 1), cb)),
            pl.BlockSpec((1, 4, 256, 256), lambda i: (l, 0, 0, 0)),
            pl.BlockSpec((1, 1, 1024), lambda i: (l, 0, 0)),
        ],
        out_specs=pl.BlockSpec((SLAB, 1024), lambda i: (i, 0)),
        out_shape=jax.ShapeDtypeStruct((T, 1024), BF16),
        compiler_params=_cparams(("parallel",), 32 * 1024 * 1024),
        name="pool_mixer",
    )(p, p, p, w_pool, pool_scale)


def _swap16(x):
    lane = lax.broadcasted_iota(I32, x.shape, x.ndim - 1)
    n = x.shape[-1]
    fwd = pltpu.roll(x, n - 16, x.ndim - 1)
    bwd = pltpu.roll(x, 16, x.ndim - 1)
    return jnp.where((lane % 32) < 16, fwd, bwd)


def _mla_prep_kernel(kva_ref, qa_ref, kr_ref, cos_ref, sin_ref, wq, wkv,
                     g_qa, g_kva, g_qn, g_qr, g_kn, g_kr,
                     qr_ref, qu_ref, k_ref, v_ref, ckv_ref, kro_ref):
    ckv = _rms(kva_ref[...].astype(F32), MLA_KV_LORA) * g_kva[0]
    ckv_ref[...] = ckv
    kv = _dot(ckv.astype(BF16), wkv[0])
    qn = _rms(qa_ref[...].astype(F32), MLA_Q_LORA) * g_qa[0]
    q = _dot(qn.astype(BF16), wq[0])
    cos = cos_ref[...]
    sin = sin_ref[...]
    kr = _rms(kr_ref[...].astype(F32), MLA_ROPE) * g_kr[0]
    kro_ref[...] = kr
    kr_rot = (kr * cos + _swap16(kr) * sin).astype(BF16)
    for h in range(MLA_HEADS):
        c0 = 2 * LANES * h
        q_nope = _rms(q[:, c0:c0 + LANES], MLA_NOPE) * g_qn[0] * MLA_SCALE
        q_rope = _rms(q[:, c0 + LANES:c0 + 2 * LANES], MLA_ROPE) * g_qr[0]
        q_rot = q_rope * cos + _swap16(q_rope) * sin
        qr_ref[:, c0:c0 + LANES] = q_nope.astype(BF16)
        qu_ref[:, c0:c0 + LANES] = q_nope.astype(BF16)
        qr_ref[:, c0 + LANES:c0 + 2 * LANES] = (q_rot * MLA_SCALE).astype(BF16)
        qu_ref[:, c0 + LANES:c0 + 2 * LANES] = (q_rope * MLA_SCALE).astype(BF16)
        k_nope = _rms(kv[:, c0:c0 + LANES], MLA_NOPE) * g_kn[0]
        k_ref[:, c0:c0 + LANES] = k_nope.astype(BF16)
        k_ref[:, c0 + LANES:c0 + 2 * LANES] = kr_rot
        v_ref[:, LANES * h:LANES * (h + 1)] = kv[:, c0 + LANES:c0 + 2 * LANES].astype(BF16)


def _mla_prep(p, cos, sin, wq, wkv, gains, l, dims):
    T = p.shape[0]
    tm = _pick_tile((512, 256), dims["t_ctx"], dims["l_lat"])
    hw = MLA_HEADS * 2 * LANES

    def gspec(n):
        return pl.BlockSpec((1, 1, n), lambda i: (l, 0, 0))

    return pl.pallas_call(
        _mla_prep_kernel,
        grid=(T // tm,),
        in_specs=[
            pl.BlockSpec((tm, MLA_KV_LORA), lambda i: (i, C_KVA // MLA_KV_LORA)),
            pl.BlockSpec((tm, MLA_Q_LORA), lambda i: (i, C_QA // MLA_Q_LORA)),
            pl.BlockSpec((tm, LANES), lambda i: (i, C_KR // LANES)),
            pl.BlockSpec((tm, LANES), lambda i: (i, 0)),
            pl.BlockSpec((tm, LANES), lambda i: (i, 0)),
            pl.BlockSpec((1, MLA_Q_LORA, hw), lambda i: (l, 0, 0)),
            pl.BlockSpec((1, MLA_KV_LORA, hw), lambda i: (l, 0, 0)),
            gspec(MLA_Q_LORA), gspec(MLA_KV_LORA), gspec(LANES), gspec(LANES), gspec(LANES), gspec(LANES),
        ],
        out_specs=[
            pl.BlockSpec((tm, hw), lambda i: (i, 0)),
            pl.BlockSpec((tm, hw), lambda i: (i, 0)),
            pl.BlockSpec((tm, hw), lambda i: (i, 0)),
            pl.BlockSpec((tm, MLA_HEADS * MLA_V), lambda i: (i, 0)),
            pl.BlockSpec((tm, MLA_KV_LORA), lambda i: (i, 0)),
            pl.BlockSpec((tm, LANES), lambda i: (i, 0)),
        ],
        out_shape=[
            jax.ShapeDtypeStruct((T, hw), BF16),
            jax.ShapeDtypeStruct((T, hw), BF16),
            jax.ShapeDtypeStruct((T, hw), BF16),
            jax.ShapeDtypeStruct((T, MLA_HEADS * MLA_V), BF16),
            jax.ShapeDtypeStruct((T, MLA_KV_LORA), F32),
            jax.ShapeDtypeStruct((T, LANES), F32),
        ],
        compiler_params=_cparams(("parallel",), 48 * 1024 * 1024),
        name="mla_prep",
    )(p, p, p, cos, sin, wq, wkv, *gains)


def _cache_kv_kernel(ckv_ref, kr_ref, wkv, g_kn, k_ref, v_ref):
    kv = _dot(ckv_ref[0, 0].astype(BF16), wkv[0])
    kr = kr_ref[0, 0].astype(BF16)
    for h in range(MLA_HEADS):
        c0 = 2 * LANES * h
        k_nope = _rms(kv[:, c0:c0 + LANES], MLA_NOPE) * g_kn[0]
        k_ref[0, 0, :, c0:c0 + LANES] = k_nope.astype(BF16)
        k_ref[0, 0, :, c0 + LANES:c0 + 2 * LANES] = kr
        v_ref[0, 0, :, LANES * h:LANES * (h + 1)] = kv[:, c0 + LANES:c0 + 2 * LANES].astype(BF16)


def _cache_kv(cache_ckv, cache_kr, wkv, g_kn):
    b_lat, _, past, _ = cache_ckv.shape
    hw = MLA_HEADS * 2 * LANES
    return pl.pallas_call(
        _cache_kv_kernel,
        grid=(DEPTH, b_lat),
        in_specs=[
            pl.BlockSpec((1, 1, past, MLA_KV_LORA), lambda l, b: (b, l, 0, 0)),
            pl.BlockSpec((1, 1, past, LANES), lambda l, b: (b, l, 0, 0)),
            pl.BlockSpec((1, MLA_KV_LORA, hw), lambda l, b: (l, 0, 0)),
            pl.BlockSpec((1, 1, LANES), lambda l, b: (l, 0, 0)),
        ],
        out_specs=[
            pl.BlockSpec((1, 1, past, hw), lambda l, b: (l, b, 0, 0)),
            pl.BlockSpec((1, 1, past, MLA_HEADS * MLA_V), lambda l, b: (l, b, 0, 0)),
        ],
        out_shape=[
            jax.ShapeDtypeStruct((DEPTH, b_lat, past, hw), BF16),
            jax.ShapeDtypeStruct((DEPTH, b_lat, past, MLA_HEADS * MLA_V), BF16),
        ],
        compiler_params=_cparams(("parallel", "parallel"), 32 * 1024 * 1024),
        name="mla_cache_decompress",
    )(cache_ckv, cache_kr, wkv, g_kn)


def _attn_ctx_kernel(q_ref, k_ref, v_ref, o_ref):
    for h in range(MLA_HEADS):
        qk = slice(h * 2 * LANES, (h + 1) * 2 * LANES)
        vs = slice(h * MLA_V, (h + 1) * MLA_V)
        s = _dot_nt(q_ref[:, qk], k_ref[:, qk])
        m = jnp.max(s, axis=-1, keepdims=True)
        pr = jnp.exp(s - m)
        den = jnp.sum(pr, axis=-1, keepdims=True)
        o_ref[:, vs] = (_dot(pr.astype(BF16), v_ref[:, vs]) / den).astype(BF16)


def _attn_ctx(q, k, v, dims):
    lc = dims["l_ctx"]
    t_ctx = dims["t_ctx"]
    hw = MLA_HEADS * 2 * LANES
    return pl.pallas_call(
        _attn_ctx_kernel,
        grid=(dims["b_ctx"],),
        in_specs=[
            pl.BlockSpec((lc, hw), lambda s: (s, 0)),
            pl.BlockSpec((lc, hw), lambda s: (s, 0)),
            pl.BlockSpec((lc, MLA_HEADS * MLA_V), lambda s: (s, 0)),
        ],
        out_specs=pl.BlockSpec((lc, MLA_HEADS * MLA_V), lambda s: (s, 0)),
        out_shape=jax.ShapeDtypeStruct((t_ctx, MLA_HEADS * MLA_V), BF16),
        compiler_params=_cparams(("parallel",), 32 * 1024 * 1024),
        name="mla_attention_context",
    )(q, k, v)


ATTN_HEADS_PER_STEP = 2


def _attn_lat_kernel(qr_ref, qu_ref, k_ref, v_ref, kc_ref, vc_ref, o_ref):
    for h in range(ATTN_HEADS_PER_STEP):
        qk = slice(h * 2 * LANES, (h + 1) * 2 * LANES)
        vs = slice(h * MLA_V, (h + 1) * MLA_V)
        s1 = _dot_nt(qr_ref[:, qk], k_ref[:, qk])
        s2 = _dot_nt(qu_ref[:, qk], kc_ref[0, 0, :, qk])
        m = jnp.maximum(jnp.max(s1, axis=-1, keepdims=True), jnp.max(s2, axis=-1, keepdims=True))
        p1 = jnp.exp(s1 - m)
        p2 = jnp.exp(s2 - m)
        den = jnp.sum(p1, axis=-1, keepdims=True) + jnp.sum(p2, axis=-1, keepdims=True)
        o = _dot(p1.astype(BF16), v_ref[:, vs]) + _dot(p2.astype(BF16), vc_ref[0, 0, :, vs])
        o_ref[:, vs] = (o / den).astype(BF16)


def _attn_lat(qr, qu, k, v, kc, vc, l, dims):
    ll = dims["l_lat"]
    past = kc.shape[2]
    tq = _pick_tile((256,), ll)
    q0 = dims["t_ctx"] // tq
    k0 = dims["t_ctx"] // ll
    nq = ll // tq
    hp = ATTN_HEADS_PER_STEP
    qw, vw = hp * 2 * LANES, hp * MLA_V
    return pl.pallas_call(
        _attn_lat_kernel,
        grid=(dims["b_lat"], MLA_HEADS // hp, nq),
        in_specs=[
            pl.BlockSpec((tq, qw), lambda b, h, i: (q0 + b * nq + i, h)),
            pl.BlockSpec((tq, qw), lambda b, h, i: (q0 + b * nq + i, h)),
            pl.BlockSpec((ll, qw), lambda b, h, i: (k0 + b, h)),
            pl.BlockSpec((ll, vw), lambda b, h, i: (k0 + b, h)),
            pl.BlockSpec((1, 1, past, qw), lambda b, h, i: (l, b, 0, h)),
            pl.BlockSpec((1, 1, past, vw), lambda b, h, i: (l, b, 0, h)),
        ],
        out_specs=pl.BlockSpec((tq, vw), lambda b, h, i: (b * nq + i, h)),
        out_shape=jax.ShapeDtypeStruct((dims["b_lat"] * ll, MLA_HEADS * MLA_V), BF16),
        compiler_params=_cparams(("parallel", "parallel", "arbitrary")),
        name="mla_attention_latent",
    )(qr, qu, k, v, kc, vc)


def _route(logits_t):
    ng, ne = N_GROUPS, EXPERTS_PER_GROUP
    lg = [logits_t[g:g + 1] for g in range(ng)]
    mg = functools.reduce(jnp.maximum, lg)
    zg = functools.reduce(lambda a, b: a + b, [jnp.exp(x - mg) for x in lg])
    pg_top = 1.0 / zg
    grp = jnp.full_like(mg, float(ng))
    for g in range(ng - 1, -1, -1):
        grp = jnp.where(lg[g] == mg, float(g), grp)
    el = []
    for j in range(ne):
        acc = jnp.zeros_like(mg)
        for g in range(ng):
            row = ng + g * ne + j
            acc = jnp.where(grp == float(g), logits_t[row:row + 1], acc)
        el.append(acc)
    m1 = functools.reduce(jnp.maximum, el)
    i1 = jnp.full_like(mg, float(ne))
    for j in range(ne - 1, -1, -1):
        i1 = jnp.where(el[j] == m1, float(j), i1)
    neg = jnp.full_like(mg, -jnp.inf)
    rest = [jnp.where(i1 == float(j), neg, el[j]) for j in range(ne)]
    m2 = functools.reduce(jnp.maximum, rest)
    i2 = jnp.full_like(mg, float(ne))
    for j in range(ne - 1, -1, -1):
        i2 = jnp.where(rest[j] == m2, float(j), i2)
    e2 = jnp.exp(m2 - m1)
    w1 = 1.0 / (1.0 + e2)
    w2 = e2 / (1.0 + e2)
    rows = [pg_top * (jnp.where(i1 == float(j), w1, 0.0) + jnp.where(i2 == float(j), w2, 0.0))
            for j in range(ne)]
    rows += [grp, jnp.zeros_like(mg), jnp.zeros_like(mg), jnp.zeros_like(mg)]
    return jnp.concatenate(rows, axis=0)


def _branch_kernel(of_ref, ob_ref, gr_ref, pool_ref, actx_ref, alat_ref, gl0, gl1, gl2,
                   wbg, wbp, wbm, g_gla, o_ref, a_scr, attn_ref, *, n_ctx_tiles):
    gg = g_gla[0]

    @pl.when(pl.program_id(0) < n_ctx_tiles)
    def _():
        attn_ref[...] = actx_ref[...]

    @pl.when(pl.program_id(0) >= n_ctx_tiles)
    def _():
        attn_ref[...] = alat_ref[...]

    for h in range(GLA_HEADS):
        cols = slice(h * GLA_DV, (h + 1) * GLA_DV)
        o = of_ref[:, cols].astype(F32) + ob_ref[:, cols].astype(F32)
        a_scr[:, cols] = (_rms(o, GLA_DV) * gg * _silu(gr_ref[:, cols].astype(F32))).astype(BF16)
    nb = 512
    for n in range(D // nb):
        cols = slice(n * nb, (n + 1) * nb)
        acc = _sigmoid(gl0[:, cols].astype(F32)) * _dot(a_scr[...], wbg[0, :, cols])
        acc += _sigmoid(gl1[:, cols].astype(F32)) * _dot(pool_ref[...], wbp[0, :, cols])
        acc += _sigmoid(gl2[:, cols].astype(F32)) * _dot(attn_ref[...], wbm[0, :, cols])
        o_ref[:, cols] = acc.astype(BF16)


def _branch_merge(o_f, o_b, p, pool_out, attn_ctx, attn_lat, w, l, dims):
    T = dims["t"]
    tm = _pick_tile((512, 256), dims["t_ctx"], dims["l_lat"])
    n_ctx_tiles = dims["t_ctx"] // tm

    def wspec(shape):
        return pl.BlockSpec((1,) + shape, lambda i: (l,) + (0,) * len(shape), pipeline_mode=pl.Buffered(1))

    return pl.pallas_call(
        functools.partial(_branch_kernel, n_ctx_tiles=n_ctx_tiles),
        grid=(T // tm,),
        in_specs=[
            pl.BlockSpec((tm, 1024), lambda i: (i, 0)),
            pl.BlockSpec((tm, 1024), lambda i: (i, 0)),
            pl.BlockSpec((tm, 1024), lambda i: (i, C_GR // 1024)),
            pl.BlockSpec((tm, 1024), lambda i: (i, 0)),
            pl.BlockSpec((tm, 1024), lambda i: (jnp.minimum(i, n_ctx_tiles - 1), 0)),
            pl.BlockSpec((tm, 1024), lambda i: (jnp.maximum(i - n_ctx_tiles, 0), 0)),
            pl.BlockSpec((tm, D), lambda i: (i, 0)),
            pl.BlockSpec((tm, D), lambda i: (i, 1)),
            pl.BlockSpec((tm, D), lambda i: (i, 2)),
            wspec((1024, D)), wspec((1024, D)), wspec((1024, D)),
            pl.BlockSpec((1, 1, GLA_DV), lambda i: (l, 0, 0)),
        ],
        out_specs=pl.BlockSpec((tm, D), lambda i: (i, 0)),
        out_shape=jax.ShapeDtypeStruct((T, D), BF16),
        scratch_shapes=[pltpu.VMEM((tm, 1024), BF16), pltpu.VMEM((tm, 1024), BF16)],
        compiler_params=_cparams(("parallel",)),
        name="branch_merge",
    )(o_f, o_b, p, pool_out, attn_ctx, attn_lat, p, p, p,
      w["w_br_gla"], w["w_br_pool"], w["w_br_mla"], w["g_gla"])


def _merge_kernel(x_ref, mod_ref, mg_ref, wo, g2_ref, wr, br, xo_ref, r_ref, *, tm):
    m = mod_ref[0, 0]
    xm = x_ref[...] + m[2:3] * _dot(mg_ref[...], wo[0])
    xo_ref[...] = xm
    g2 = g2_ref[0]
    for ci in range(tm // LANES):
        rows = slice(ci * LANES, (ci + 1) * LANES)
        h2 = _rms(xm[rows], D) * g2 * (1.0 + m[4:5]) + m[3:4]
        h_hi = h2.astype(BF16)
        hs = (h_hi, (h2 - h_hi.astype(F32)).astype(BF16))
        lt = br[0]
        for ia, ib in ((0, 0), (0, 1), (1, 0)):
            lt = lt + _dot_nt(wr[0, ia], hs[ib])
        r_ref[:, rows] = _route(lt)


def _merge(x, mods, merged, w, l, dims):
    T = dims["t"]
    tm = _pick_tile((512, 256), dims["t_ctx"], dims["l_lat"])
    gmap = _group_map(dims, tm)
    return pl.pallas_call(
        functools.partial(_merge_kernel, tm=tm),
        grid=(T // tm,),
        in_specs=[
            pl.BlockSpec((tm, D), lambda i: (i, 0)),
            pl.BlockSpec((1, 1, 6, D), lambda i: (l, gmap(i), 0, 0)),
            pl.BlockSpec((tm, D), lambda i: (i, 0)),
            pl.BlockSpec((1, D, D), lambda i: (l, 0, 0), pipeline_mode=pl.Buffered(1)),
            pl.BlockSpec((1, 1, D), lambda i: (l, 0, 0)),
            pl.BlockSpec((1, 2, 32, D), lambda i: (l, 0, 0, 0)),
            pl.BlockSpec((1, 32, LANES), lambda i: (l, 0, 0)),
        ],
        out_specs=[
            pl.BlockSpec((tm, D), lambda i: (i, 0)),
            pl.BlockSpec((8, tm), lambda i: (0, i)),
        ],
        out_shape=[jax.ShapeDtypeStruct((T, D), F32), jax.ShapeDtypeStruct((8, T), F32)],
        compiler_params=_cparams(("parallel",)),
        name="merge_out_route",
    )(x, mods, merged, w["w_o"], w["g_norm2"], w["router_w"], w["router_b"])


def _moe_kernel(tg, nvalid, gi_ref, gn_ref, si_ref, x_hbm, cm_ref, mod_ref, g2_ref, wg, wu, wd,
                o_hbm, xbuf, obuf, h_scr, gsem, ssem, *, tm, nt, n_cond):
    del tg
    t = pl.program_id(0)
    slot = t % 2

    def used(i):
        return nvalid[jnp.clip(i, 0, nt - 1)] > 0

    def gather_copy(sl, r, row):
        return pltpu.make_async_copy(x_hbm.at[pl.ds(row, 1)], xbuf.at[sl, pl.ds(r, 1)], gsem.at[sl])

    def scatter_copy(sl, r, row):
        return pltpu.make_async_copy(obuf.at[sl, pl.ds(r, 1)], o_hbm.at[pl.ds(row, 1)], ssem.at[sl])

    def start_rows(make, idx_ref):
        for r in range(tm):
            make(r, idx_ref[0, 0, r]).start(priority=r % 2)

    def wait_gather(sl):
        pltpu.make_async_copy(x_hbm.at[pl.ds(0, tm)], xbuf.at[sl], gsem.at[sl]).wait()

    def wait_scatter(sl):
        pltpu.make_async_copy(obuf.at[sl], o_hbm.at[pl.ds(0, tm)], ssem.at[sl]).wait()

    @pl.when(t == 0)
    def _():
        obuf[1] = jnp.zeros((tm, D), F32)
        n_tok = o_hbm.shape[0] - N_GROUPS * tm
        for g in range(N_GROUPS):
            pltpu.make_async_copy(obuf.at[1], o_hbm.at[pl.ds(n_tok + g * tm, tm)], ssem.at[1]).start()
        for g in range(N_GROUPS):
            wait_scatter(1)

    @pl.when(jnp.logical_and(t == 0, used(0)))
    def _():
        start_rows(functools.partial(gather_copy, 0), gi_ref)

    has_next = jnp.logical_and(t + 1 < nt, used(t + 1))

    @pl.when(jnp.logical_and(t >= 2, used(t - 2)))
    def _():
        wait_scatter(slot)

    @pl.when(used(t))
    def _():
        wait_gather(slot)
        mods = mod_ref[0]
        g2 = g2_ref[0]

        def cond_rows(cg, k):
            out = mods[0, k:k + 1]
            for c in range(1, n_cond):
                out = jnp.where(cg == float(c), mods[c, k:k + 1], out)
            return out

        for r in range(tm // LANES):
            rows = slice(r * LANES, (r + 1) * LANES)
            cg = cm_ref[rows, 4:5]
            h2 = _rms(xbuf[slot, rows, :], D) * g2 * (1.0 + cond_rows(cg, 4)) + cond_rows(cg, 3)
            h_scr[rows, :] = h2.astype(BF16)

        hb = h_scr[...]
        y = None
        per = tm // EXPERTS_PER_GROUP
        for e in range(EXPERTS_PER_GROUP):
            for r in range(e * per, (e + 1) * per):
                gather_copy(1 - slot, r, gn_ref[0, 0, r]).start(priority=r % 2)
            hid = _silu(_dot(hb, wg[0, e])) * _dot(hb, wu[0, e]) * cm_ref[:, e:e + 1]
            part = _dot(hid.astype(BF16), wd[0, e])
            y = part if y is None else y + part
        for r in range(tm // LANES):
            rows = slice(r * LANES, (r + 1) * LANES)
            cg = cm_ref[rows, 4:5]
            obuf[slot, rows, :] = xbuf[slot, rows, :] + cond_rows(cg, 5) * y[rows]
        start_rows(functools.partial(scatter_copy, slot), si_ref)

    @pl.when(jnp.logical_and(used(t), jnp.logical_not(has_next)))
    def _():
        wait_gather(1 - slot)

    @pl.when(jnp.logical_and(t == nt - 1, used(t - 1)))
    def _():
        wait_scatter(1 - slot)

    @pl.when(jnp.logical_and(t == nt - 1, used(t)))
    def _():
        wait_scatter(slot)


def _moe_plan(route, dims, tm):
    T = route.shape[1]
    nt = T // tm + N_GROUPS
    grp = route[4].astype(I32)
    _, order, *comb_sorted = lax.sort(
        (grp, jnp.arange(T, dtype=I32), route[0], route[1], route[2], route[3]), num_keys=1, is_stable=True)
    gids = jnp.arange(N_GROUPS, dtype=I32)
    counts = jnp.sum(grp[None, :] == gids[:, None], axis=1).astype(I32)
    tiles = (counts + tm - 1) // tm
    tile_end = jnp.cumsum(tiles)
    tile_start = tile_end - tiles
    tok_start = jnp.cumsum(counts) - counts
    tidx = jnp.arange(nt, dtype=I32)
    tg = jnp.minimum(jnp.sum(tidx[:, None] >= tile_end[None, :], axis=1), N_GROUPS - 1).astype(I32)
    onehot = (tg[:, None] == gids[None, :]).astype(I32)
    used = tidx < tile_end[-1]
    in_group = (tidx - jnp.sum(onehot * tile_start[None, :], axis=1)) * tm
    nvalid = jnp.where(used, jnp.clip(jnp.sum(onehot * counts[None, :], axis=1) - in_group, 0, tm), 0)
    nvalid = nvalid.astype(I32)
    slot = jnp.arange(tm, dtype=I32)[None, :]
    valid = slot < nvalid[:, None]
    shift = tile_start * tm - tok_start
    pad = jnp.zeros((nt * tm - T,), I32)
    ints = jnp.concatenate([order, pad])
    flts = jnp.concatenate([jnp.stack(comb_sorted, axis=0), jnp.zeros((4, nt * tm - T), F32)], axis=1)
    slot_g = jnp.broadcast_to(tg[:, None], (nt, tm)).reshape(nt * tm)
    src = jnp.zeros((nt * tm,), I32)
    comb = jnp.zeros((4, nt * tm), F32)
    for g in range(N_GROUPS):
        src = jnp.where(slot_g == g, jnp.roll(ints, shift[g]), src)
        comb = jnp.where((slot_g == g)[None, :], jnp.roll(flts, shift[g], axis=1), comb)
    src = jnp.where(valid, src.reshape(nt, tm), 0)
    comb = jnp.where(valid[None], comb.reshape(4, nt, tm), 0.0)
    dst = jnp.where(valid, src, T + tg[:, None] * tm + slot).astype(I32)
    cond = jnp.where(src < dims["t_ctx"], 0, 1 + (src - dims["t_ctx"]) // dims["l_lat"]).astype(F32)
    cm = jnp.concatenate([comb, cond[None], jnp.zeros((3,) + cond.shape, F32)], axis=0)
    cm = cm.transpose(1, 2, 0).reshape(nt * tm, 8)
    return tg, nvalid, src.reshape(nt, 1, tm), dst.reshape(nt, 1, tm), cm


def _moe_tile(dims):
    return 512 if dims["t"] % 512 == 0 else 256


def _moe(x, route, mods, w, l, dims):
    T = dims["t"]
    tm = _moe_tile(dims)
    nt = T // tm + N_GROUPS
    tg, nvalid, src, dst, cm = _moe_plan(route, dims, tm)
    ff = EXPERT_FF

    def wspec(shape):
        return pl.BlockSpec((1, EXPERTS_PER_GROUP) + shape, lambda t, tg_, nv: (l, tg_[t], 0, 0),
                            pipeline_mode=pl.Buffered(1))

    grid_spec = pltpu.PrefetchScalarGridSpec(
        num_scalar_prefetch=2, grid=(nt,),
        in_specs=[
            pl.BlockSpec((1, 1, tm), lambda t, tg_, nv: (t, 0, 0), memory_space=pltpu.SMEM),
            pl.BlockSpec((1, 1, tm), lambda t, tg_, nv: (jnp.minimum(t + 1, nt - 1), 0, 0),
                         memory_space=pltpu.SMEM),
            pl.BlockSpec((1, 1, tm), lambda t, tg_, nv: (t, 0, 0), memory_space=pltpu.SMEM),
            pl.BlockSpec(memory_space=pl.ANY),
            pl.BlockSpec((tm, 8), lambda t, tg_, nv: (t, 0)),
            pl.BlockSpec((1, 8, 6, D), lambda t, tg_, nv: (l, 0, 0, 0)),
            pl.BlockSpec((1, 1, D), lambda t, tg_, nv: (l, 0, 0)),
            wspec((D, ff)), wspec((D, ff)), wspec((ff, D)),
        ],
        out_specs=pl.BlockSpec(memory_space=pl.ANY),
        scratch_shapes=[pltpu.VMEM((2, tm, D), F32), pltpu.VMEM((2, tm, D), F32), pltpu.VMEM((tm, D), BF16),
                        pltpu.SemaphoreType.DMA((2,)), pltpu.SemaphoreType.DMA((2,))])
    return pl.pallas_call(
        functools.partial(_moe_kernel, tm=tm, nt=nt, n_cond=1 + dims["b_lat"]),
        grid_spec=grid_spec,
        out_shape=jax.ShapeDtypeStruct((T + N_GROUPS * tm, D), F32),
        compiler_params=_cparams(("arbitrary",)),
        name="moe_group_experts",
    )(tg, nvalid, src, src, dst, x, cm, mods, w["g_norm2"], w["w_exp_gate"], w["w_exp_up"], w["w_exp_down"])


def _pack_params(w_in, w_gla_dec, b_gla_dec, w_mla_q_up, g_q_rope, g_k_rope,
                 w_group_router, b_group_router, w_expert_router, b_expert_router):
    wt = jnp.swapaxes(w_in, 1, 2)
    gq, gk, gv, gr, glow, pin, qa, kva, kr, gl = jnp.split(
        wt, [int(v) for v in np.cumsum(SPLIT_SIZES)[:-1]], axis=1)

    def padr(a, n):
        return jnp.pad(a, ((0, 0), (0, n - a.shape[1]), (0, 0)))

    pieces = [gl, gq, gk, gv, gr, pin, kva, qa, padr(glow, LANES), padr(kr, LANES)]
    wp = jnp.concatenate([a.astype(BF16) for a in pieces], axis=1)
    wdec = jnp.zeros((DEPTH, 2, LANES, GLA_HEADS * GLA_DK), F32)
    for d in range(2):
        wdec = wdec.at[:, d, d * GLA_RANK:(d + 1) * GLA_RANK, :].set(w_gla_dec[:, d])
    wdec = wdec.astype(BF16)
    bdec = b_gla_dec.reshape(DEPTH, 2, 1, GLA_HEADS * GLA_DK)
    wq = w_mla_q_up.reshape(DEPTH, MLA_Q_LORA, MLA_HEADS, MLA_NOPE + MLA_ROPE)
    wq = jnp.pad(wq, ((0, 0), (0, 0), (0, 0), (0, 2 * LANES - MLA_NOPE - MLA_ROPE)))
    wq = wq.reshape(DEPTH, MLA_Q_LORA, MLA_HEADS * 2 * LANES).astype(BF16)
    g_qr = jnp.pad(g_q_rope, ((0, 0), (0, LANES - MLA_ROPE))).reshape(DEPTH, 1, LANES)
    g_kr = jnp.pad(g_k_rope, ((0, 0), (0, LANES - MLA_ROPE))).reshape(DEPTH, 1, LANES)
    wr = jnp.concatenate([w_group_router, w_expert_router], axis=-1).transpose(0, 2, 1)
    wr = jnp.pad(wr, ((0, 0), (0, 32 - wr.shape[1]), (0, 0)))
    hi = wr.astype(BF16)
    lo = (wr - hi.astype(F32)).astype(BF16)
    router_w = jnp.stack([hi, lo], axis=1)
    rb = jnp.concatenate([b_group_router, b_expert_router], axis=-1)
    rb = jnp.pad(rb, ((0, 0), (0, 32 - rb.shape[1])))
    router_b = jnp.broadcast_to(rb[:, :, None], (DEPTH, 32, LANES))
    return wp, wdec, bdec, wq, g_qr, g_kr, router_w, router_b


def _rope_tables(dims):
    ll = dims["l_lat"]
    t = jnp.arange(ll)
    row = (t // GRID_W).astype(F32)
    col = (t % GRID_W).astype(F32)
    n_freq = MLA_ROPE // 4
    inv = ROPE_THETA ** (-jnp.arange(n_freq, dtype=F32) / n_freq)
    ang = jnp.stack([row[:, None] * inv, col[:, None] * inv], axis=1)
    cos, sin = jnp.cos(ang), jnp.sin(ang)
    cos64 = jnp.concatenate([cos, cos], axis=-1).reshape(ll, MLA_ROPE)
    sin64 = jnp.concatenate([-sin, sin], axis=-1).reshape(ll, MLA_ROPE)
    pad = jnp.zeros((ll, LANES - MLA_ROPE), F32)
    cos_l = jnp.tile(jnp.concatenate([cos64, pad], axis=-1), (dims["b_lat"], 1))
    sin_l = jnp.tile(jnp.concatenate([sin64, pad], axis=-1), (dims["b_lat"], 1))
    cos_c = jnp.concatenate([jnp.ones((dims["t_ctx"], MLA_ROPE), F32),
                             jnp.zeros((dims["t_ctx"], LANES - MLA_ROPE), F32)], axis=-1)
    sin_c = jnp.zeros((dims["t_ctx"], LANES), F32)
    return jnp.concatenate([cos_c, cos_l], axis=0), jnp.concatenate([sin_c, sin_l], axis=0)


def kernel(x_prompt, x_sample, c, cache_mla_ckv, cache_mla_krope, state_gla, c_ctx, w_ada, b_ada, g_norm1, g_norm2, w_in, w_gla_dec, b_gla_dec, g_gla, w_pool, pool_scale, g_mla_qa, w_mla_q_up, g_mla_kva, w_mla_kv_up, g_q_nope, g_q_rope, g_k_nope, g_k_rope, w_br_gla, w_br_pool, w_br_mla, w_o, w_group_router, b_group_router, w_expert_router, b_expert_router, w_exp_gate, w_exp_up, w_exp_down):
    b_ctx, l_ctx, _ = x_prompt.shape
    b_lat, l_lat, _ = x_sample.shape
    dims = dict(b_ctx=b_ctx, l_ctx=l_ctx, b_lat=b_lat, l_lat=l_lat,
                t_ctx=b_ctx * l_ctx, t_lat=b_lat * l_lat, t=b_ctx * l_ctx + b_lat * l_lat)
    t_ctx = dims["t_ctx"]
    assert l_ctx % SLAB == 0 and l_lat % SLAB == 0 and t_ctx % l_lat == 0 and 1 + b_lat <= 8

    wp, wdec, bdec, wq, g_qr, g_kr, router_w, router_b = _pack_params(
        w_in, w_gla_dec, b_gla_dec, w_mla_q_up, g_q_rope, g_k_rope,
        w_group_router, b_group_router, w_expert_router, b_expert_router)
    wkv = w_mla_kv_up.astype(BF16)
    r3 = lambda a: a.reshape(DEPTH, 1, a.shape[-1])
    gains = (r3(g_mla_qa), r3(g_mla_kva), r3(g_q_nope), g_qr, r3(g_k_nope), g_kr)
    w = dict(w_br_gla=w_br_gla.astype(BF16), w_br_pool=w_br_pool.astype(BF16),
             w_br_mla=w_br_mla.astype(BF16), w_o=w_o.astype(BF16), g_gla=r3(g_gla),
             g_norm2=r3(g_norm2), router_w=router_w, router_b=router_b,
             w_exp_gate=w_exp_gate.astype(BF16), w_exp_up=w_exp_up.astype(BF16),
             w_exp_down=w_exp_down.astype(BF16))
    w_pool_b = w_pool.astype(BF16)
    pool_scale3 = r3(pool_scale)
    g1 = r3(g_norm1)

    cond = jnp.concatenate([c_ctx[None, :], c, jnp.zeros((8 - 1 - b_lat, D), F32)], axis=0)
    mods = _modulation(cond, w_ada, b_ada)

    cos, sin = _rope_tables(dims)
    cache_kr = jnp.pad(cache_mla_krope, ((0, 0), (0, 0), (0, 0), (0, LANES - MLA_ROPE)))
    kc, vc = _cache_kv(cache_mla_ckv, cache_kr, wkv, r3(g_k_nope))
    gla_tables = _gla_tables(dims)

    x = jnp.concatenate([x_prompt.reshape(t_ctx, D), x_sample.reshape(dims["t_lat"], D)], axis=0)
    ckv_l, kr_l, st_l = [], [], []
    for l in range(DEPTH):
        p = _proj_in(x, mods, g1, wp, l, dims)
        s0 = jnp.concatenate([jnp.zeros((1, 2, GLA_HEADS, GLA_DV, GLA_DK), F32),
                              jnp.swapaxes(state_gla[:, l], -1, -2)], axis=0)
        o_f, o_b, s_fin = _gla(p, wdec[l], bdec[l], s0, gla_tables, dims)
        pool_out = _pool(p, w_pool_b, pool_scale3, l, dims)
        qr, qu, k, v, ckv, kro = _mla_prep(p, cos, sin, wq, wkv, gains, l, dims)
        attn_ctx = _attn_ctx(qr, k, v, dims)
        attn_lat = _attn_lat(qr, qu, k, v, kc, vc, l, dims)
        merged = _branch_merge(o_f, o_b, p, pool_out, attn_ctx, attn_lat, w, l, dims)
        x_mid, route = _merge(x, mods, merged, w, l, dims)
        x = _moe(x_mid, route, mods, w, l, dims)
        ckv_l.append(ckv[:t_ctx].reshape(b_ctx, l_ctx, MLA_KV_LORA))
        kr_l.append(kro[:t_ctx, :MLA_ROPE].reshape(b_ctx, l_ctx, MLA_ROPE))
        st_l.append(jnp.swapaxes(s_fin[:b_ctx], -1, -2))

    y_prompt = x[:t_ctx].reshape(b_ctx, l_ctx, D)
    y_sample = x[t_ctx:dims["t"]].reshape(b_lat, l_lat, D)
    return (y_prompt, y_sample, jnp.stack(ckv_l, axis=1), jnp.stack(kr_l, axis=1),
            jnp.stack(st_l, axis=1))
```

```python
import functools
import math

import numpy as np
import jax
import jax.numpy as jnp
from jax import lax
from jax.experimental import pallas as pl
from jax.experimental.pallas import tpu as pltpu

F32 = jnp.float32
BF16 = jnp.bfloat16
I32 = jnp.int32

D = 2048
DEPTH = 4
EPS = 1e-6
GRID_W = 64
GLA_HEADS, GLA_DK, GLA_DV, GLA_RANK, GLA_TAU, GLA_CHUNK = 4, 128, 256, 16, 16.0, 64
POOL_WINDOWS = (2, 4, 8, 16)
POOL_GROUP_DIM = 256
MLA_HEADS, MLA_Q_LORA, MLA_KV_LORA, MLA_NOPE, MLA_ROPE, MLA_V = 8, 768, 512, 128, 64, 128
MLA_SCALE = 1.0 / math.sqrt(MLA_NOPE + MLA_ROPE)
ROPE_THETA = 10000.0
N_GROUPS, EXPERTS_PER_GROUP, EXPERT_FF = 4, 4, 512
SPLIT_SIZES = (512, 512, 1024, 1024, 32, 1024, 768, 512, 64, 6144)

LANES = 128
SLAB = 256
VMEM_LIMIT_BYTES = 60000 * 1024

C_GL = 0
C_Q = 6144
C_K = 6656
C_V = 7168
C_GR = 8192
C_PIN = 9216
C_KVA = 10240
C_QA = 10752
C_GLOW = 11520
C_KR = 11648
NP = 11776


def _dot(a, b):
    return jnp.dot(a, b, preferred_element_type=F32)


def _dot_nt(a, b):
    return lax.dot_general(a, b, (((1,), (1,)), ((), ())), preferred_element_type=F32)


def _dot_tn(a, b):
    return lax.dot_general(a, b, (((0,), (0,)), ((), ())), preferred_element_type=F32)


def _split3(x):
    hi = x.astype(BF16)
    r1 = x - hi.astype(F32)
    mid = r1.astype(BF16)
    lo = (r1 - mid.astype(F32)).astype(BF16)
    return hi, mid, lo


def _sigmoid(x):
    return 0.5 * jnp.tanh(0.5 * x) + 0.5


def _silu(x):
    return x * _sigmoid(x)


def _rms(x, n):
    ms = jnp.sum(x * x, axis=-1, keepdims=True) * (1.0 / n)
    return x * lax.rsqrt(ms + EPS)


def _cparams(sem, vmem=VMEM_LIMIT_BYTES):
    return pltpu.CompilerParams(dimension_semantics=sem, vmem_limit_bytes=vmem)


def _pick_tile(cands, *extents):
    for c in cands:
        if all(e % c == 0 for e in extents):
            return c
    raise ValueError(f"no tile in {cands} divides {extents}")


def _mod_kernel(c_ref, w_ref, b_ref, o_ref):
    a = _silu(c_ref[...]).astype(BF16)
    o_ref[0] = _dot(a, w_ref[0].astype(BF16)) + b_ref[0]


def _modulation(cond, w_ada, b_ada):
    tn = 1024
    out = pl.pallas_call(
        _mod_kernel,
        grid=(DEPTH, 6 * D // tn),
        in_specs=[
            pl.BlockSpec((8, D), lambda l, j: (0, 0)),
            pl.BlockSpec((1, D, tn), lambda l, j: (l, 0, j)),
            pl.BlockSpec((1, 1, tn), lambda l, j: (l, 0, j)),
        ],
        out_specs=pl.BlockSpec((1, 8, tn), lambda l, j: (l, 0, j)),
        out_shape=jax.ShapeDtypeStruct((DEPTH, 8, 6 * D), F32),
        compiler_params=_cparams(("parallel", "parallel"), 40 * 1024 * 1024),
        name="adaln_modulation",
    )(cond, w_ada, b_ada.reshape(DEPTH, 1, 6 * D))
    return out.reshape(DEPTH, 8, 6, D)


def _proj_in_kernel(xa_ref, xb_ref, mod_ref, g_ref, w_ref, o_ref, h_scr, *, tm, n_ctx_tiles):
    def fill(x_ref):
        m = mod_ref[0, 0]
        g = g_ref[0]

        def body(r, carry):
            rows = pl.ds(pl.multiple_of(r * LANES, LANES), LANES)
            y = _rms(x_ref[rows, :], D) * g
            h_scr[rows, :] = (y * (1.0 + m[1:2]) + m[0:1]).astype(BF16)
            return carry

        lax.fori_loop(0, tm // LANES, body, 0)

    first = pl.program_id(1) == 0
    is_ctx = pl.program_id(0) < n_ctx_tiles

    @pl.when(jnp.logical_and(first, is_ctx))
    def _():
        fill(xa_ref)

    @pl.when(jnp.logical_and(first, jnp.logical_not(is_ctx)))
    def _():
        fill(xb_ref)

    o_ref[...] = _dot_nt(h_scr[...], w_ref[0]).astype(BF16)


def _x_specs(xs, tm, dims):
    xa, xb, lat_off = xs
    nct = dims["t_ctx"] // tm
    off = lat_off // tm
    return [xa, xb], [
        pl.BlockSpec((tm, D), lambda i, *_: (jnp.minimum(i, nct - 1), 0)),
        pl.BlockSpec((tm, D), lambda i, *_: (jnp.maximum(i - nct, 0) + off, 0)),
    ], nct


def _proj_in(xs, mods, g1, wp, l, dims):
    T = dims["t"]
    tm = _pick_tile((1024, 512, 256), dims["t_ctx"], dims["l_lat"])
    tn = 512
    gmap = _group_map(dims, tm)
    x_args, x_specs, nct = _x_specs(xs, tm, dims)
    return pl.pallas_call(
        functools.partial(_proj_in_kernel, tm=tm, n_ctx_tiles=nct),
        grid=(T // tm, NP // tn),
        in_specs=x_specs + [
            pl.BlockSpec((1, 1, 6, D), lambda i, j: (l, gmap(i), 0, 0)),
            pl.BlockSpec((1, 1, D), lambda i, j: (l, 0, 0)),
            pl.BlockSpec((1, tn, D), lambda i, j: (l, j, 0)),
        ],
        out_specs=pl.BlockSpec((tm, tn), lambda i, j: (i, j)),
        out_shape=jax.ShapeDtypeStruct((T, NP), BF16),
        scratch_shapes=[pltpu.VMEM((tm, D), BF16)],
        compiler_params=_cparams(("parallel", "arbitrary")),
        name="norm1_proj_in",
    )(*x_args, mods, g1, wp)


def _group_map(dims, tm):
    n_ctx_tiles = dims["t_ctx"] // tm
    per_lat = dims["l_lat"] // tm

    def gmap(i):
        return jnp.where(i < n_ctx_tiles, 0, 1 + (i - n_ctx_tiles) // per_lat)

    return gmap


def _gla_kernel(fblk, bblk, first, last, sidx, unit,
                qf, kf, vf, gf, qb, kb, vb, gb, wdec, bdec, s0,
                of, ob, sout, s_scr):
    del fblk, bblk, sidx, unit
    s = pl.program_id(0)

    @pl.when(first[s] == 1)
    def _():
        s_scr[...] = s0[0]

    n_chunks = SLAB // GLA_CHUNK
    r = lax.broadcasted_iota(I32, (SLAB, SLAB), 0)
    c = lax.broadcasted_iota(I32, (SLAB, SLAB), 1)
    same = (r // GLA_CHUNK) == (c // GLA_CHUNK)

    def rows(x, ci):
        return x[ci * GLA_CHUNK:(ci + 1) * GLA_CHUNK]

    dirs = ((qf, kf, vf, gf, of), (qb, kb, vb, gb, ob))
    for d, (q_ref, k_ref, v_ref, g_ref, o_ref) in enumerate(dirs):
        tri = jnp.logical_and(same, (c <= r) if d == 0 else (c >= r))
        tri_b = tri.astype(BF16)
        dec = _dot(g_ref[...], wdec[d]) + bdec[d]
        la = (jnp.minimum(dec, 0.0) - jnp.log1p(jnp.exp(-jnp.abs(dec)))) * (1.0 / GLA_TAU)
        hi, mid, lo = _split3(la)
        b_all = _dot(tri_b, hi) + _dot(tri_b, mid) + _dot(tri_b, lo)
        edge = GLA_CHUNK - 1 if d == 0 else 0
        tot = [b_all[ci * GLA_CHUNK + edge:ci * GLA_CHUNK + edge + 1] for ci in range(n_chunks)]
        bl_all = jnp.concatenate([jnp.broadcast_to(t, (GLA_CHUNK, t.shape[1])) for t in tot], axis=0)
        order = range(n_chunks) if d == 0 else range(n_chunks - 1, -1, -1)
        for h in range(GLA_HEADS):
            kc = slice(h * GLA_DK, (h + 1) * GLA_DK)
            vc = slice(h * GLA_DV, (h + 1) * GLA_DV)
            b = b_all[:, kc]
            bl = bl_all[:, kc]
            q = q_ref[:, kc].astype(F32) * (GLA_DK ** -0.5)
            k = k_ref[:, kc].astype(F32)
            v = v_ref[:, vc]
            qt = (q * jnp.exp(b)).astype(BF16)
            kt = (k * jnp.exp(-b)).astype(BF16)
            ke = (k * jnp.exp(bl - b)).astype(BF16)
            a = jnp.where(tri, _dot_nt(qt, kt), 0.0).astype(BF16)
            o_intra = _dot(a, v)
            ds_t = [_dot_tn(rows(v, ci), rows(ke, ci)) for ci in range(n_chunks)]
            decay = [jnp.exp(tot[ci][:, kc]) for ci in range(n_chunks)]
            sd = s_scr[d, h]
            o_inter = [None] * n_chunks
            for ci in order:
                o_inter[ci] = _dot_nt(rows(qt, ci), sd.astype(BF16))
                sd = sd * decay[ci] + ds_t[ci]
            o_ref[:, vc] = (o_intra + jnp.concatenate(o_inter, axis=0)).astype(BF16)
            s_scr[d, h] = sd

            @pl.when(last[s] == 1)
            def _(sd=sd, d=d, h=h):
                sout[0, d, h] = sd.T


def _gla_tables(dims):
    ctx_slabs = dims["l_ctx"] // SLAB
    lat_slabs = dims["l_lat"] // SLAB
    fblk, bblk, first, last, sidx, unit = [], [], [], [], [], []
    base = 0
    for u in range(dims["b_ctx"] + dims["b_lat"]):
        is_ctx = u < dims["b_ctx"]
        n = ctx_slabs if is_ctx else lat_slabs
        for j in range(n):
            fblk.append(base + j)
            bblk.append(base + n - 1 - j)
            first.append(int(j == 0))
            last.append(int(j == n - 1))
            sidx.append(0 if is_ctx else 1 + u - dims["b_ctx"])
            unit.append(u)
        base += n
    return [jnp.asarray(np.asarray(t, np.int32)) for t in (fblk, bblk, first, last, sidx, unit)]


def _gla(p, wdec, bdec, s0, tables, dims):
    T = p.shape[0]
    n_steps = T // SLAB
    n_units = dims["b_ctx"] + dims["b_lat"]

    hk, hv = GLA_HEADS * GLA_DK, GLA_HEADS * GLA_DV
    state = (2, GLA_HEADS, GLA_DV, GLA_DK)
    state_out = (2, GLA_HEADS, GLA_DK, GLA_DV)

    def pspec(width, col0, which):
        cb = col0 // width
        if which == 0:
            return pl.BlockSpec((SLAB, width), lambda s, fb, bb, fi, la, si, un: (fb[s], cb))
        return pl.BlockSpec((SLAB, width), lambda s, fb, bb, fi, la, si, un: (bb[s], cb))

    in_specs = []
    for which in (0, 1):
        in_specs += [pspec(hk, C_Q, which), pspec(hk, C_K, which), pspec(hv, C_V, which),
                     pspec(LANES, C_GLOW, which)]
    in_specs += [
        pl.BlockSpec((2, LANES, hk), lambda s, *_: (0, 0, 0)),
        pl.BlockSpec((2, 1, hk), lambda s, *_: (0, 0, 0)),
        pl.BlockSpec((1,) + state, lambda s, fb, bb, fi, la, si, un: (si[s], 0, 0, 0, 0)),
    ]
    out_specs = [
        pl.BlockSpec((SLAB, hv), lambda s, fb, bb, fi, la, si, un: (fb[s], 0)),
        pl.BlockSpec((SLAB, hv), lambda s, fb, bb, fi, la, si, un: (bb[s], 0)),
        pl.BlockSpec((1,) + state_out, lambda s, fb, bb, fi, la, si, un: (un[s], 0, 0, 0, 0)),
    ]
    grid_spec = pltpu.PrefetchScalarGridSpec(
        num_scalar_prefetch=6, grid=(n_steps,),
        in_specs=in_specs, out_specs=out_specs,
        scratch_shapes=[pltpu.VMEM(state, F32)])
    return pl.pallas_call(
        _gla_kernel,
        grid_spec=grid_spec,
        out_shape=[jax.ShapeDtypeStruct((T, hv), BF16),
                   jax.ShapeDtypeStruct((T, hv), BF16),
                   jax.ShapeDtypeStruct((n_units,) + state_out, F32)],
        compiler_params=_cparams(("arbitrary",), 32 * 1024 * 1024),
        name="gla_bidirectional",
    )(*tables, p, p, p, p, p, p, p, p, wdec, bdec, s0)


def _pool_kernel(cur, prv, nxt, wp, sc, o_ref, *, n_ctx_slabs, ctx_slabs, lat_slabs):
    i = pl.program_id(0)
    is_ctx = i < n_ctx_slabs
    seq_slabs = jnp.where(is_ctx, ctx_slabs, lat_slabs)
    j = jnp.where(is_ctx, i % ctx_slabs, (i - n_ctx_slabs) % lat_slabs)
    has_prev = j > 0
    has_next = j < seq_slabs - 1
    seq_len = seq_slabs * SLAB
    r = lax.broadcasted_iota(I32, (SLAB, SLAB), 0)
    c = lax.broadcasted_iota(I32, (SLAB, SLAB), 1)
    t = j * SLAB + lax.broadcasted_iota(I32, (SLAB, 1), 0)
    gd = POOL_GROUP_DIM
    for g, w in enumerate(POOL_WINDOWS):
        lo_off, hi_off = w // 2, w - w // 2
        cols = slice(g * gd, (g + 1) * gd)
        u = cur[:, cols]
        b_cur = jnp.logical_and(c >= r - lo_off, c < r + hi_off)
        b_prv = jnp.logical_and(c - SLAB >= r - lo_off, has_prev)
        b_nxt = jnp.logical_and(c + SLAB < r + hi_off, has_next)
        ssum = (_dot(b_cur.astype(BF16), u) + _dot(b_prv.astype(BF16), prv[:, cols])
                + _dot(b_nxt.astype(BF16), nxt[:, cols]))
        cnt = (jnp.minimum(t + hi_off, seq_len) - jnp.maximum(t - lo_off, 0)).astype(F32)
        pooled = ssum / cnt - u.astype(F32)
        o_ref[:, cols] = (_dot(pooled.astype(BF16), wp[0, g]) * sc[0, :, cols]).astype(BF16)


def _pool(p, w_pool, pool_scale, l, dims):
    T = p.shape[0]
    n = T // SLAB
    cb = C_PIN // 1024
    kern = functools.partial(_pool_kernel, n_ctx_slabs=dims["t_ctx"] // SLAB,
                             ctx_slabs=dims["l_ctx"] // SLAB, lat_slabs=dims["l_lat"] // SLAB)
    return pl.pallas_call(
        kern,
        grid=(n,),
        in_specs=[
            pl.BlockSpec((SLAB, 1024), lambda i: (i, cb)),
            pl.BlockSpec((SLAB, 1024), lambda i: (jnp.maximum(i - 1, 0), cb)),
            pl.BlockSpec((SLAB, 1024), lambda i: (jnp.minimum(i + 1, n - 1), cb)),
            pl.BlockSpec((1, 4, 256, 256), lambda i: (l, 0, 0, 0)),
            pl.BlockSpec((1, 1, 1024), lambda i: (l, 0, 0)),
        ],
        out_specs=pl.BlockSpec((SLAB, 1024), lambda i: (i, 0)),
        out_shape=jax.ShapeDtypeStruct((T, 1024), BF16),
        compiler_params=_cparams(("parallel",), 32 * 1024 * 1024),
        name="pool_mixer",
    )(p, p, p, w_pool, pool_scale)


def _swap16(x):
    lane = lax.broadcasted_iota(I32, x.shape, x.ndim - 1)
    n = x.shape[-1]
    fwd = pltpu.roll(x, n - 16, x.ndim - 1)
    bwd = pltpu.roll(x, 16, x.ndim - 1)
    return jnp.where((lane % 32) < 16, fwd, bwd)


def _mla_prep_kernel(kva_ref, qa_ref, kr_ref, cos_ref, sin_ref, wq, wkv,
                     g_qa, g_kva, g_qn, g_qr, g_kn, g_kr,
                     qr_ref, qu_ref, k_ref, v_ref, ckv_ref, kro_ref):
    ckv = _rms(kva_ref[...].astype(F32), MLA_KV_LORA) * g_kva[0]
    ckv_ref[...] = ckv
    kv = _dot(ckv.astype(BF16), wkv[0])
    qn = _rms(qa_ref[...].astype(F32), MLA_Q_LORA) * g_qa[0]
    q = _dot(qn.astype(BF16), wq[0])
    cos = cos_ref[...]
    sin = sin_ref[...]
    kr = _rms(kr_ref[...].astype(F32), MLA_ROPE) * g_kr[0]
    kro_ref[...] = kr
    kr_rot = (kr * cos + _swap16(kr) * sin).astype(BF16)
    for h in range(MLA_HEADS):
        c0 = 2 * LANES * h
        q_nope = _rms(q[:, c0:c0 + LANES], MLA_NOPE) * g_qn[0] * MLA_SCALE
        q_rope = _rms(q[:, c0 + LANES:c0 + 2 * LANES], MLA_ROPE) * g_qr[0]
        q_rot = q_rope * cos + _swap16(q_rope) * sin
        qr_ref[:, c0:c0 + LANES] = q_nope.astype(BF16)
        qu_ref[:, c0:c0 + LANES] = q_nope.astype(BF16)
        qr_ref[:, c0 + LANES:c0 + 2 * LANES] = (q_rot * MLA_SCALE).astype(BF16)
        qu_ref[:, c0 + LANES:c0 + 2 * LANES] = (q_rope * MLA_SCALE).astype(BF16)
        k_nope = _rms(kv[:, c0:c0 + LANES], MLA_NOPE) * g_kn[0]
        k_ref[:, c0:c0 + LANES] = k_nope.astype(BF16)
        k_ref[:, c0 + LANES:c0 + 2 * LANES] = kr_rot
        v_ref[:, LANES * h:LANES * (h + 1)] = kv[:, c0 + LANES:c0 + 2 * LANES].astype(BF16)


def _mla_prep(p, cos, sin, wq, wkv, gains, l, dims):
    T = p.shape[0]
    tm = _pick_tile((512, 256), dims["t_ctx"], dims["l_lat"])
    hw = MLA_HEADS * 2 * LANES

    def gspec(n):
        return pl.BlockSpec((1, 1, n), lambda i: (l, 0, 0))

    return pl.pallas_call(
        _mla_prep_kernel,
        grid=(T // tm,),
        in_specs=[
            pl.BlockSpec((tm, MLA_KV_LORA), lambda i: (i, C_KVA // MLA_KV_LORA)),
            pl.BlockSpec((tm, MLA_Q_LORA), lambda i: (i, C_QA // MLA_Q_LORA)),
            pl.BlockSpec((tm, LANES), lambda i: (i, C_KR // LANES)),
            pl.BlockSpec((tm, LANES), lambda i: (i, 0)),
            pl.BlockSpec((tm, LANES), lambda i: (i, 0)),
            pl.BlockSpec((1, MLA_Q_LORA, hw), lambda i: (l, 0, 0)),
            pl.BlockSpec((1, MLA_KV_LORA, hw), lambda i: (l, 0, 0)),
            gspec(MLA_Q_LORA), gspec(MLA_KV_LORA), gspec(LANES), gspec(LANES), gspec(LANES), gspec(LANES),
        ],
        out_specs=[
            pl.BlockSpec((tm, hw), lambda i: (i, 0)),
            pl.BlockSpec((tm, hw), lambda i: (i, 0)),
            pl.BlockSpec((tm, hw), lambda i: (i, 0)),
            pl.BlockSpec((tm, MLA_HEADS * MLA_V), lambda i: (i, 0)),
            pl.BlockSpec((tm, MLA_KV_LORA), lambda i: (i, 0)),
            pl.BlockSpec((tm, LANES), lambda i: (i, 0)),
        ],
        out_shape=[
            jax.ShapeDtypeStruct((T, hw), BF16),
            jax.ShapeDtypeStruct((T, hw), BF16),
            jax.ShapeDtypeStruct((T, hw), BF16),
            jax.ShapeDtypeStruct((T, MLA_HEADS * MLA_V), BF16),
            jax.ShapeDtypeStruct((T, MLA_KV_LORA), F32),
            jax.ShapeDtypeStruct((T, LANES), F32),
        ],
        compiler_params=_cparams(("parallel",), 48 * 1024 * 1024),
        name="mla_prep",
    )(p, p, p, cos, sin, wq, wkv, *gains)


def _cache_kv_kernel(ckv_ref, kr_ref, wkv, g_kn, k_ref, v_ref):
    kv = _dot(ckv_ref[0, 0].astype(BF16), wkv[0])
    kr = kr_ref[0, 0].astype(BF16)
    for h in range(MLA_HEADS):
        c0 = 2 * LANES * h
        k_nope = _rms(kv[:, c0:c0 + LANES], MLA_NOPE) * g_kn[0]
        k_ref[0, 0, :, c0:c0 + LANES] = k_nope.astype(BF16)
        k_ref[0, 0, :, c0 + LANES:c0 + 2 * LANES] = kr
        v_ref[0, 0, :, LANES * h:LANES * (h + 1)] = kv[:, c0 + LANES:c0 + 2 * LANES].astype(BF16)


def _cache_kv(cache_ckv, cache_kr, wkv, g_kn):
    b_lat, _, past, _ = cache_ckv.shape
    hw = MLA_HEADS * 2 * LANES
    return pl.pallas_call(
        _cache_kv_kernel,
        grid=(DEPTH, b_lat),
        in_specs=[
            pl.BlockSpec((1, 1, past, MLA_KV_LORA), lambda l, b: (b, l, 0, 0)),
            pl.BlockSpec((1, 1, past, LANES), lambda l, b: (b, l, 0, 0)),
            pl.BlockSpec((1, MLA_KV_LORA, hw), lambda l, b: (l, 0, 0)),
            pl.BlockSpec((1, 1, LANES), lambda l, b: (l, 0, 0)),
        ],
        out_specs=[
            pl.BlockSpec((1, 1, past, hw), lambda l, b: (l, b, 0, 0)),
            pl.BlockSpec((1, 1, past, MLA_HEADS * MLA_V), lambda l, b: (l, b, 0, 0)),
        ],
        out_shape=[
            jax.ShapeDtypeStruct((DEPTH, b_lat, past, hw), BF16),
            jax.ShapeDtypeStruct((DEPTH, b_lat, past, MLA_HEADS * MLA_V), BF16),
        ],
        compiler_params=_cparams(("parallel", "parallel"), 32 * 1024 * 1024),
        name="mla_cache_decompress",
    )(cache_ckv, cache_kr, wkv, g_kn)


def _attn_ctx_kernel(q_ref, k_ref, v_ref, o_ref):
    for h in range(MLA_HEADS):
        qk = slice(h * 2 * LANES, (h + 1) * 2 * LANES)
        vs = slice(h * MLA_V, (h + 1) * MLA_V)
        s = _dot_nt(q_ref[:, qk], k_ref[:, qk])
        m = jnp.max(s, axis=-1, keepdims=True)
        pr = jnp.exp(s - m)
        den = jnp.sum(pr, axis=-1, keepdims=True)
        o_ref[:, vs] = (_dot(pr.astype(BF16), v_ref[:, vs]) / den).astype(BF16)


def _attn_ctx(q, k, v, dims):
    lc = dims["l_ctx"]
    t_ctx = dims["t_ctx"]
    hw = MLA_HEADS * 2 * LANES
    return pl.pallas_call(
        _attn_ctx_kernel,
        grid=(dims["b_ctx"],),
        in_specs=[
            pl.BlockSpec((lc, hw), lambda s: (s, 0)),
            pl.BlockSpec((lc, hw), lambda s: (s, 0)),
            pl.BlockSpec((lc, MLA_HEADS * MLA_V), lambda s: (s, 0)),
        ],
        out_specs=pl.BlockSpec((lc, MLA_HEADS * MLA_V), lambda s: (s, 0)),
        out_shape=jax.ShapeDtypeStruct((t_ctx, MLA_HEADS * MLA_V), BF16),
        compiler_params=_cparams(("parallel",), 32 * 1024 * 1024),
        name="mla_attention_context",
    )(q, k, v)


ATTN_HEADS_PER_STEP = 2
ATTN_KEY_PARTS = 2


def _attn_lat_kernel(qr_ref, qu_ref, k_ref, v_ref, kc_ref, vc_ref, o_ref):
    for h in range(ATTN_HEADS_PER_STEP):
        qk = slice(h * 2 * LANES, (h + 1) * 2 * LANES)
        vs = slice(h * MLA_V, (h + 1) * MLA_V)
        n_lat = k_ref.shape[0]
        part = n_lat // ATTN_KEY_PARTS
        qr = qr_ref[:, qk]
        stats = []
        for c in range(ATTN_KEY_PARTS):
            ks = slice(c * part, (c + 1) * part)
            s = _dot_nt(qr, k_ref[ks, qk])
            mc = jnp.max(s, axis=-1, keepdims=True)
            if c == ATTN_KEY_PARTS - 1:
                s2 = _dot_nt(qu_ref[:, qk], kc_ref[0, 0, :, qk])
                mc = jnp.maximum(mc, jnp.max(s2, axis=-1, keepdims=True))
            pr = jnp.exp(s - mc)
            lc = jnp.sum(pr, axis=-1, keepdims=True)
            oc = _dot(pr.astype(BF16), v_ref[ks, vs])
            if c == ATTN_KEY_PARTS - 1:
                p2 = jnp.exp(s2 - mc)
                lc = lc + jnp.sum(p2, axis=-1, keepdims=True)
                oc = oc + _dot(p2.astype(BF16), vc_ref[0, 0, :, vs])
            stats.append((mc, lc, oc))
        m = functools.reduce(jnp.maximum, [st[0] for st in stats])
        num = den = None
        for mc, lc, oc in stats:
            wc = jnp.exp(mc - m)
            num = oc * wc if num is None else num + oc * wc
            den = lc * wc if den is None else den + lc * wc
        o_ref[:, vs] = (num / den).astype(BF16)


def _attn_lat(qr, qu, k, v, kc, vc, l, dims):
    ll = dims["l_lat"]
    past = kc.shape[2]
    tq = _pick_tile((256,), ll)
    q0 = dims["t_ctx"] // tq
    k0 = dims["t_ctx"] // ll
    nq = ll // tq
    hp = ATTN_HEADS_PER_STEP
    qw, vw = hp * 2 * LANES, hp * MLA_V
    return pl.pallas_call(
        _attn_lat_kernel,
        grid=(dims["b_lat"], MLA_HEADS // hp, nq),
        in_specs=[
            pl.BlockSpec((tq, qw), lambda b, h, i: (q0 + b * nq + i, h)),
            pl.BlockSpec((tq, qw), lambda b, h, i: (q0 + b * nq + i, h)),
            pl.BlockSpec((ll, qw), lambda b, h, i: (k0 + b, h)),
            pl.BlockSpec((ll, vw), lambda b, h, i: (k0 + b, h)),
            pl.BlockSpec((1, 1, past, qw), lambda b, h, i: (l, b, 0, h)),
            pl.BlockSpec((1, 1, past, vw), lambda b, h, i: (l, b, 0, h)),
        ],
        out_specs=pl.BlockSpec((tq, vw), lambda b, h, i: (b * nq + i, h)),
        out_shape=jax.ShapeDtypeStruct((dims["b_lat"] * ll, MLA_HEADS * MLA_V), BF16),
        compiler_params=_cparams(("parallel", "parallel", "arbitrary")),
        name="mla_attention_latent",
    )(qr, qu, k, v, kc, vc)


def _route(logits_t):
    ng, ne = N_GROUPS, EXPERTS_PER_GROUP
    lg = [logits_t[g:g + 1] for g in range(ng)]
    mg = functools.reduce(jnp.maximum, lg)
    zg = functools.reduce(lambda a, b: a + b, [jnp.exp(x - mg) for x in lg])
    pg_top = 1.0 / zg
    grp = jnp.full_like(mg, float(ng))
    for g in range(ng - 1, -1, -1):
        grp = jnp.where(lg[g] == mg, float(g), grp)
    el = []
    for j in range(ne):
        acc = jnp.zeros_like(mg)
        for g in range(ng):
            row = ng + g * ne + j
            acc = jnp.where(grp == float(g), logits_t[row:row + 1], acc)
        el.append(acc)
    m1 = functools.reduce(jnp.maximum, el)
    i1 = jnp.full_like(mg, float(ne))
    for j in range(ne - 1, -1, -1):
        i1 = jnp.where(el[j] == m1, float(j), i1)
    neg = jnp.full_like(mg, -jnp.inf)
    rest = [jnp.where(i1 == float(j), neg, el[j]) for j in range(ne)]
    m2 = functools.reduce(jnp.maximum, rest)
    i2 = jnp.full_like(mg, float(ne))
    for j in range(ne - 1, -1, -1):
        i2 = jnp.where(rest[j] == m2, float(j), i2)
    e2 = jnp.exp(m2 - m1)
    w1 = 1.0 / (1.0 + e2)
    w2 = e2 / (1.0 + e2)
    rows = [pg_top * (jnp.where(i1 == float(j), w1, 0.0) + jnp.where(i2 == float(j), w2, 0.0))
            for j in range(ne)]
    rows += [grp, jnp.zeros_like(mg), jnp.zeros_like(mg), jnp.zeros_like(mg)]
    return jnp.concatenate(rows, axis=0)


def _branch_kernel(of_ref, ob_ref, gr_ref, pool_ref, actx_ref, alat_ref, gl0, gl1, gl2,
                   wbg, wbp, wbm, g_gla, o_ref, a_scr, attn_ref, *, n_ctx_tiles):
    gg = g_gla[0]

    @pl.when(pl.program_id(0) < n_ctx_tiles)
    def _():
        attn_ref[...] = actx_ref[...]

    @pl.when(pl.program_id(0) >= n_ctx_tiles)
    def _():
        attn_ref[...] = alat_ref[...]

    for h in range(GLA_HEADS):
        cols = slice(h * GLA_DV, (h + 1) * GLA_DV)
        o = of_ref[:, cols].astype(F32) + ob_ref[:, cols].astype(F32)
        a_scr[:, cols] = (_rms(o, GLA_DV) * gg * _silu(gr_ref[:, cols].astype(F32))).astype(BF16)
    nb = 512
    for n in range(D // nb):
        cols = slice(n * nb, (n + 1) * nb)
        acc = _sigmoid(gl0[:, cols].astype(F32)) * _dot(a_scr[...], wbg[0, :, cols])
        acc += _sigmoid(gl1[:, cols].astype(F32)) * _dot(pool_ref[...], wbp[0, :, cols])
        acc += _sigmoid(gl2[:, cols].astype(F32)) * _dot(attn_ref[...], wbm[0, :, cols])
        o_ref[:, cols] = acc.astype(BF16)


def _branch_merge(o_f, o_b, p, pool_out, attn_ctx, attn_lat, w, l, dims):
    T = dims["t"]
    tm = _pick_tile((512, 256), dims["t_ctx"], dims["l_lat"])
    n_ctx_tiles = dims["t_ctx"] // tm

    def wspec(shape):
        return pl.BlockSpec((1,) + shape, lambda i: (l,) + (0,) * len(shape), pipeline_mode=pl.Buffered(1))

    return pl.pallas_call(
        functools.partial(_branch_kernel, n_ctx_tiles=n_ctx_tiles),
        grid=(T // tm,),
        in_specs=[
            pl.BlockSpec((tm, 1024), lambda i: (i, 0)),
            pl.BlockSpec((tm, 1024), lambda i: (i, 0)),
            pl.BlockSpec((tm, 1024), lambda i: (i, C_GR // 1024)),
            pl.BlockSpec((tm, 1024), lambda i: (i, 0)),
            pl.BlockSpec((tm, 1024), lambda i: (jnp.minimum(i, n_ctx_tiles - 1), 0)),
            pl.BlockSpec((tm, 1024), lambda i: (jnp.maximum(i - n_ctx_tiles, 0), 0)),
            pl.BlockSpec((tm, D), lambda i: (i, 0)),
            pl.BlockSpec((tm, D), lambda i: (i, 1)),
            pl.BlockSpec((tm, D), lambda i: (i, 2)),
            wspec((1024, D)), wspec((1024, D)), wspec((1024, D)),
            pl.BlockSpec((1, 1, GLA_DV), lambda i: (l, 0, 0)),
        ],
        out_specs=pl.BlockSpec((tm, D), lambda i: (i, 0)),
        out_shape=jax.ShapeDtypeStruct((T, D), BF16),
        scratch_shapes=[pltpu.VMEM((tm, 1024), BF16), pltpu.VMEM((tm, 1024), BF16)],
        compiler_params=_cparams(("parallel",)),
        name="branch_merge",
    )(o_f, o_b, p, pool_out, attn_ctx, attn_lat, p, p, p,
      w["w_br_gla"], w["w_br_pool"], w["w_br_mla"], w["g_gla"])


def _merge_kernel(xa_ref, xb_ref, mod_ref, mg_ref, wo, g2_ref, wr, br, xo_ref, r_ref, *, tm, n_ctx_tiles):
    m = mod_ref[0, 0]
    delta = m[2:3] * _dot(mg_ref[...], wo[0])

    @pl.when(pl.program_id(0) < n_ctx_tiles)
    def _():
        xo_ref[...] = xa_ref[...] + delta

    @pl.when(pl.program_id(0) >= n_ctx_tiles)
    def _():
        xo_ref[...] = xb_ref[...] + delta

    xm = xo_ref[...]
    g2 = g2_ref[0]
    for ci in range(tm // LANES):
        rows = slice(ci * LANES, (ci + 1) * LANES)
        h2 = _rms(xm[rows], D) * g2 * (1.0 + m[4:5]) + m[3:4]
        h_hi = h2.astype(BF16)
        hs = (h_hi, (h2 - h_hi.astype(F32)).astype(BF16))
        lt = br[0]
        for ia, ib in ((0, 0), (0, 1), (1, 0)):
            lt = lt + _dot_nt(wr[0, ia], hs[ib])
        r_ref[:, rows] = _route(lt)


def _merge(xs, mods, merged, w, l, dims):
    T = dims["t"]
    tm = _pick_tile((512, 256), dims["t_ctx"], dims["l_lat"])
    gmap = _group_map(dims, tm)
    x_args, x_specs, nct = _x_specs(xs, tm, dims)
    return pl.pallas_call(
        functools.partial(_merge_kernel, tm=tm, n_ctx_tiles=nct),
        grid=(T // tm,),
        in_specs=x_specs + [
            pl.BlockSpec((1, 1, 6, D), lambda i: (l, gmap(i), 0, 0)),
            pl.BlockSpec((tm, D), lambda i: (i, 0)),
            pl.BlockSpec((1, D, D), lambda i: (l, 0, 0), pipeline_mode=pl.Buffered(1)),
            pl.BlockSpec((1, 1, D), lambda i: (l, 0, 0)),
            pl.BlockSpec((1, 2, 32, D), lambda i: (l, 0, 0, 0)),
            pl.BlockSpec((1, 32, LANES), lambda i: (l, 0, 0)),
        ],
        out_specs=[
            pl.BlockSpec((tm, D), lambda i: (i, 0)),
            pl.BlockSpec((8, tm), lambda i: (0, i)),
        ],
        out_shape=[jax.ShapeDtypeStruct((T, D), F32), jax.ShapeDtypeStruct((8, T), F32)],
        compiler_params=_cparams(("parallel",)),
        name="merge_out_route",
    )(*x_args, mods, merged, w["w_o"], w["g_norm2"], w["router_w"], w["router_b"])


def _moe_kernel(tg, nvalid, gi_ref, gn_ref, si_ref, x_hbm, cm_ref, mod_ref, g2_ref, wg, wu, wd,
                o_hbm, xbuf, obuf, h_scr, gsem, ssem, *, tm, nt, n_cond):
    del tg
    t = pl.program_id(0)
    slot = t % 2

    def used(i):
        return nvalid[jnp.clip(i, 0, nt - 1)] > 0

    def gather_copy(sl, r, row):
        return pltpu.make_async_copy(x_hbm.at[pl.ds(row, 1)], xbuf.at[sl, pl.ds(r, 1)], gsem.at[sl])

    def scatter_copy(sl, r, row):
        return pltpu.make_async_copy(obuf.at[sl, pl.ds(r, 1)], o_hbm.at[pl.ds(row, 1)], ssem.at[sl])

    def start_rows(make, idx_ref):
        for r in range(tm):
            make(r, idx_ref[0, 0, r]).start(priority=r % 2)

    def wait_gather(sl):
        pltpu.make_async_copy(x_hbm.at[pl.ds(0, tm)], xbuf.at[sl], gsem.at[sl]).wait()

    def wait_scatter(sl):
        pltpu.make_async_copy(obuf.at[sl], o_hbm.at[pl.ds(0, tm)], ssem.at[sl]).wait()

    @pl.when(t == 0)
    def _():
        obuf[1] = jnp.zeros((tm, D), F32)
        n_tok = o_hbm.shape[0] - N_GROUPS * tm
        for g in range(N_GROUPS):
            pltpu.make_async_copy(obuf.at[1], o_hbm.at[pl.ds(n_tok + g * tm, tm)], ssem.at[1]).start()
        for g in range(N_GROUPS):
            wait_scatter(1)

    @pl.when(jnp.logical_and(t == 0, used(0)))
    def _():
        start_rows(functools.partial(gather_copy, 0), gi_ref)

    has_next = jnp.logical_and(t + 1 < nt, used(t + 1))

    @pl.when(jnp.logical_and(t >= 2, used(t - 2)))
    def _():
        wait_scatter(slot)

    @pl.when(used(t))
    def _():
        wait_gather(slot)
        mods = mod_ref[0]
        g2 = g2_ref[0]

        def cond_rows(cg, k):
            out = mods[0, k:k + 1]
            for c in range(1, n_cond):
                out = jnp.where(cg == float(c), mods[c, k:k + 1], out)
            return out

        for r in range(tm // LANES):
            rows = slice(r * LANES, (r + 1) * LANES)
            cg = cm_ref[rows, 4:5]
            h2 = _rms(xbuf[slot, rows, :], D) * g2 * (1.0 + cond_rows(cg, 4)) + cond_rows(cg, 3)
            h_scr[rows, :] = h2.astype(BF16)

        hb = h_scr[...]
        y = None
        per = tm // EXPERTS_PER_GROUP
        for e in range(EXPERTS_PER_GROUP):
            for r in range(e * per, (e + 1) * per):
                gather_copy(1 - slot, r, gn_ref[0, 0, r]).start(priority=r % 2)
            hid = _silu(_dot(hb, wg[0, e])) * _dot(hb, wu[0, e]) * cm_ref[:, e:e + 1]
            part = _dot(hid.astype(BF16), wd[0, e])
            y = part if y is None else y + part
        for r in range(tm // LANES):
            rows = slice(r * LANES, (r + 1) * LANES)
            cg = cm_ref[rows, 4:5]
            obuf[slot, rows, :] = xbuf[slot, rows, :] + cond_rows(cg, 5) * y[rows]
        start_rows(functools.partial(scatter_copy, slot), si_ref)

    @pl.when(jnp.logical_and(used(t), jnp.logical_not(has_next)))
    def _():
        wait_gather(1 - slot)

    @pl.when(jnp.logical_and(t == nt - 1, used(t - 1)))
    def _():
        wait_scatter(1 - slot)

    @pl.when(jnp.logical_and(t == nt - 1, used(t)))
    def _():
        wait_scatter(slot)


def _moe_plan(route, dims, tm):
    T = route.shape[1]
    nt = T // tm + N_GROUPS
    grp = route[4].astype(I32)
    _, order, *comb_sorted = lax.sort(
        (grp, jnp.arange(T, dtype=I32), route[0], route[1], route[2], route[3]), num_keys=1, is_stable=True)
    gids = jnp.arange(N_GROUPS, dtype=I32)
    counts = jnp.sum(grp[None, :] == gids[:, None], axis=1).astype(I32)
    tiles = (counts + tm - 1) // tm
    tile_end = jnp.cumsum(tiles)
    tile_start = tile_end - tiles
    tok_start = jnp.cumsum(counts) - counts
    tidx = jnp.arange(nt, dtype=I32)
    tg = jnp.minimum(jnp.sum(tidx[:, None] >= tile_end[None, :], axis=1), N_GROUPS - 1).astype(I32)
    onehot = (tg[:, None] == gids[None, :]).astype(I32)
    used = tidx < tile_end[-1]
    in_group = (tidx - jnp.sum(onehot * tile_start[None, :], axis=1)) * tm
    nvalid = jnp.where(used, jnp.clip(jnp.sum(onehot * counts[None, :], axis=1) - in_group, 0, tm), 0)
    nvalid = nvalid.astype(I32)
    slot = jnp.arange(tm, dtype=I32)[None, :]
    valid = slot < nvalid[:, None]
    shift = tile_start * tm - tok_start
    pad = jnp.zeros((nt * tm - T,), I32)
    ints = jnp.concatenate([order, pad])
    flts = jnp.concatenate([jnp.stack(comb_sorted, axis=0), jnp.zeros((4, nt * tm - T), F32)], axis=1)
    slot_g = jnp.broadcast_to(tg[:, None], (nt, tm)).reshape(nt * tm)
    src = jnp.zeros((nt * tm,), I32)
    comb = jnp.zeros((4, nt * tm), F32)
    for g in range(N_GROUPS):
        src = jnp.where(slot_g == g, jnp.roll(ints, shift[g]), src)
        comb = jnp.where((slot_g == g)[None, :], jnp.roll(flts, shift[g], axis=1), comb)
    src = jnp.where(valid, src.reshape(nt, tm), 0)
    comb = jnp.where(valid[None], comb.reshape(4, nt, tm), 0.0)
    dst = jnp.where(valid, src, T + tg[:, None] * tm + slot).astype(I32)
    cond = jnp.where(src < dims["t_ctx"], 0, 1 + (src - dims["t_ctx"]) // dims["l_lat"]).astype(F32)
    cm = jnp.concatenate([comb, cond[None], jnp.zeros((3,) + cond.shape, F32)], axis=0)
    cm = cm.transpose(1, 2, 0).reshape(nt * tm, 8)
    return tg, nvalid, src.reshape(nt, 1, tm), dst.reshape(nt, 1, tm), cm


def _moe_tile(dims):
    return 512 if dims["t"] % 512 == 0 else 256


def _moe(x, route, mods, w, l, dims):
    T = dims["t"]
    tm = _moe_tile(dims)
    nt = T // tm + N_GROUPS
    tg, nvalid, src, dst, cm = _moe_plan(route, dims, tm)
    ff = EXPERT_FF

    def wspec(shape):
        return pl.BlockSpec((1, EXPERTS_PER_GROUP) + shape, lambda t, tg_, nv: (l, tg_[t], 0, 0),
                            pipeline_mode=pl.Buffered(1))

    grid_spec = pltpu.PrefetchScalarGridSpec(
        num_scalar_prefetch=2, grid=(nt,),
        in_specs=[
            pl.BlockSpec((1, 1, tm), lambda t, tg_, nv: (t, 0, 0), memory_space=pltpu.SMEM),
            pl.BlockSpec((1, 1, tm), lambda t, tg_, nv: (jnp.minimum(t + 1, nt - 1), 0, 0),
                         memory_space=pltpu.SMEM),
            pl.BlockSpec((1, 1, tm), lambda t, tg_, nv: (t, 0, 0), memory_space=pltpu.SMEM),
            pl.BlockSpec(memory_space=pl.ANY),
            pl.BlockSpec((tm, 8), lambda t, tg_, nv: (t, 0)),
            pl.BlockSpec((1, 8, 6, D), lambda t, tg_, nv: (l, 0, 0, 0)),
            pl.BlockSpec((1, 1, D), lambda t, tg_, nv: (l, 0, 0)),
            wspec((D, ff)), wspec((D, ff)), wspec((ff, D)),
        ],
        out_specs=pl.BlockSpec(memory_space=pl.ANY),
        scratch_shapes=[pltpu.VMEM((2, tm, D), F32), pltpu.VMEM((2, tm, D), F32), pltpu.VMEM((tm, D), BF16),
                        pltpu.SemaphoreType.DMA((2,)), pltpu.SemaphoreType.DMA((2,))])
    return pl.pallas_call(
        functools.partial(_moe_kernel, tm=tm, nt=nt, n_cond=1 + dims["b_lat"]),
        grid_spec=grid_spec,
        out_shape=jax.ShapeDtypeStruct((T + N_GROUPS * tm, D), F32),
        compiler_params=_cparams(("arbitrary",)),
        name="moe_group_experts",
    )(tg, nvalid, src, src, dst, x, cm, mods, w["g_norm2"], w["w_exp_gate"], w["w_exp_up"], w["w_exp_down"])


def _pack_params(w_in, w_gla_dec, b_gla_dec, w_mla_q_up, g_q_rope, g_k_rope,
                 w_group_router, b_group_router, w_expert_router, b_expert_router):
    wt = jnp.swapaxes(w_in, 1, 2)
    gq, gk, gv, gr, glow, pin, qa, kva, kr, gl = jnp.split(
        wt, [int(v) for v in np.cumsum(SPLIT_SIZES)[:-1]], axis=1)

    def padr(a, n):
        return jnp.pad(a, ((0, 0), (0, n - a.shape[1]), (0, 0)))

    pieces = [gl, gq, gk, gv, gr, pin, kva, qa, padr(glow, LANES), padr(kr, LANES)]
    wp = jnp.concatenate([a.astype(BF16) for a in pieces], axis=1)
    wdec = jnp.zeros((DEPTH, 2, LANES, GLA_HEADS * GLA_DK), F32)
    for d in range(2):
        wdec = wdec.at[:, d, d * GLA_RANK:(d + 1) * GLA_RANK, :].set(w_gla_dec[:, d])
    wdec = wdec.astype(BF16)
    bdec = b_gla_dec.reshape(DEPTH, 2, 1, GLA_HEADS * GLA_DK)
    wq = w_mla_q_up.reshape(DEPTH, MLA_Q_LORA, MLA_HEADS, MLA_NOPE + MLA_ROPE)
    wq = jnp.pad(wq, ((0, 0), (0, 0), (0, 0), (0, 2 * LANES - MLA_NOPE - MLA_ROPE)))
    wq = wq.reshape(DEPTH, MLA_Q_LORA, MLA_HEADS * 2 * LANES).astype(BF16)
    g_qr = jnp.pad(g_q_rope, ((0, 0), (0, LANES - MLA_ROPE))).reshape(DEPTH, 1, LANES)
    g_kr = jnp.pad(g_k_rope, ((0, 0), (0, LANES - MLA_ROPE))).reshape(DEPTH, 1, LANES)
    wr = jnp.concatenate([w_group_router, w_expert_router], axis=-1).transpose(0, 2, 1)
    wr = jnp.pad(wr, ((0, 0), (0, 32 - wr.shape[1]), (0, 0)))
    hi = wr.astype(BF16)
    lo = (wr - hi.astype(F32)).astype(BF16)
    router_w = jnp.stack([hi, lo], axis=1)
    rb = jnp.concatenate([b_group_router, b_expert_router], axis=-1)
    rb = jnp.pad(rb, ((0, 0), (0, 32 - rb.shape[1])))
    router_b = jnp.broadcast_to(rb[:, :, None], (DEPTH, 32, LANES))
    return wp, wdec, bdec, wq, g_qr, g_kr, router_w, router_b


def _rope_tables(dims):
    ll = dims["l_lat"]
    t = jnp.arange(ll)
    row = (t // GRID_W).astype(F32)
    col = (t % GRID_W).astype(F32)
    n_freq = MLA_ROPE // 4
    inv = ROPE_THETA ** (-jnp.arange(n_freq, dtype=F32) / n_freq)
    ang = jnp.stack([row[:, None] * inv, col[:, None] * inv], axis=1)
    cos, sin = jnp.cos(ang), jnp.sin(ang)
    cos64 = jnp.concatenate([cos, cos], axis=-1).reshape(ll, MLA_ROPE)
    sin64 = jnp.concatenate([-sin, sin], axis=-1).reshape(ll, MLA_ROPE)
    pad = jnp.zeros((ll, LANES - MLA_ROPE), F32)
    cos_l = jnp.tile(jnp.concatenate([cos64, pad], axis=-1), (dims["b_lat"], 1))
    sin_l = jnp.tile(jnp.concatenate([sin64, pad], axis=-1), (dims["b_lat"], 1))
    cos_c = jnp.concatenate([jnp.ones((dims["t_ctx"], MLA_ROPE), F32),
                             jnp.zeros((dims["t_ctx"], LANES - MLA_ROPE), F32)], axis=-1)
    sin_c = jnp.zeros((dims["t_ctx"], LANES), F32)
    return jnp.concatenate([cos_c, cos_l], axis=0), jnp.concatenate([sin_c, sin_l], axis=0)


def kernel(x_prompt, x_sample, c, cache_mla_ckv, cache_mla_krope, state_gla, c_ctx, w_ada, b_ada, g_norm1, g_norm2, w_in, w_gla_dec, b_gla_dec, g_gla, w_pool, pool_scale, g_mla_qa, w_mla_q_up, g_mla_kva, w_mla_kv_up, g_q_nope, g_q_rope, g_k_nope, g_k_rope, w_br_gla, w_br_pool, w_br_mla, w_o, w_group_router, b_group_router, w_expert_router, b_expert_router, w_exp_gate, w_exp_up, w_exp_down):
    b_ctx, l_ctx, _ = x_prompt.shape
    b_lat, l_lat, _ = x_sample.shape
    dims = dict(b_ctx=b_ctx, l_ctx=l_ctx, b_lat=b_lat, l_lat=l_lat,
                t_ctx=b_ctx * l_ctx, t_lat=b_lat * l_lat, t=b_ctx * l_ctx + b_lat * l_lat)
    t_ctx = dims["t_ctx"]
    assert l_ctx % SLAB == 0 and l_lat % SLAB == 0 and t_ctx % l_lat == 0 and 1 + b_lat <= 8

    wp, wdec, bdec, wq, g_qr, g_kr, router_w, router_b = _pack_params(
        w_in, w_gla_dec, b_gla_dec, w_mla_q_up, g_q_rope, g_k_rope,
        w_group_router, b_group_router, w_expert_router, b_expert_router)
    wkv = w_mla_kv_up.astype(BF16)
    r3 = lambda a: a.reshape(DEPTH, 1, a.shape[-1])
    gains = (r3(g_mla_qa), r3(g_mla_kva), r3(g_q_nope), g_qr, r3(g_k_nope), g_kr)
    w = dict(w_br_gla=w_br_gla.astype(BF16), w_br_pool=w_br_pool.astype(BF16),
             w_br_mla=w_br_mla.astype(BF16), w_o=w_o.astype(BF16), g_gla=r3(g_gla),
             g_norm2=r3(g_norm2), router_w=router_w, router_b=router_b,
             w_exp_gate=w_exp_gate.astype(BF16), w_exp_up=w_exp_up.astype(BF16),
             w_exp_down=w_exp_down.astype(BF16))
    w_pool_b = w_pool.astype(BF16)
    pool_scale3 = r3(pool_scale)
    g1 = r3(g_norm1)

    cond = jnp.concatenate([c_ctx[None, :], c, jnp.zeros((8 - 1 - b_lat, D), F32)], axis=0)
    mods = _modulation(cond, w_ada, b_ada)

    cos, sin = _rope_tables(dims)
    cache_kr = jnp.pad(cache_mla_krope, ((0, 0), (0, 0), (0, 0), (0, LANES - MLA_ROPE)))
    kc, vc = _cache_kv(cache_mla_ckv, cache_kr, wkv, r3(g_k_nope))
    gla_tables = _gla_tables(dims)

    xs = (x_prompt.reshape(t_ctx, D), x_sample.reshape(dims["t_lat"], D), 0)
    ckv_l, kr_l, st_l = [], [], []
    for l in range(DEPTH):
        p = _proj_in(xs, mods, g1, wp, l, dims)
        s0 = jnp.concatenate([jnp.zeros((1, 2, GLA_HEADS, GLA_DV, GLA_DK), F32),
                              jnp.swapaxes(state_gla[:, l], -1, -2)], axis=0)
        o_f, o_b, s_fin = _gla(p, wdec[l], bdec[l], s0, gla_tables, dims)
        pool_out = _pool(p, w_pool_b, pool_scale3, l, dims)
        qr, qu, k, v, ckv, kro = _mla_prep(p, cos, sin, wq, wkv, gains, l, dims)
        attn_ctx = _attn_ctx(qr, k, v, dims)
        attn_lat = _attn_lat(qr, qu, k, v, kc, vc, l, dims)
        merged = _branch_merge(o_f, o_b, p, pool_out, attn_ctx, attn_lat, w, l, dims)
        x_mid, route = _merge(xs, mods, merged, w, l, dims)
        x = _moe(x_mid, route, mods, w, l, dims)
        xs = (x, x, t_ctx)
        ckv_l.append(ckv[:t_ctx].reshape(b_ctx, l_ctx, MLA_KV_LORA))
        kr_l.append(kro[:t_ctx, :MLA_ROPE].reshape(b_ctx, l_ctx, MLA_ROPE))
        st_l.append(s_fin[:b_ctx])

    y_prompt = x[:t_ctx].reshape(b_ctx, l_ctx, D)
    y_sample = x[t_ctx:dims["t"]].reshape(b_lat, l_lat, D)
    return (y_prompt, y_sample, jnp.stack(ckv_l, axis=1), jnp.stack(kr_l, axis=1),
            jnp.stack(st_l, axis=1))
```

```python
import functools
import math

import numpy as np
import jax
import jax.numpy as jnp
from jax import lax
from jax.experimental import pallas as pl
from jax.experimental.pallas import tpu as pltpu

F32 = jnp.float32
BF16 = jnp.bfloat16
I32 = jnp.int32

D = 2048
DEPTH = 4
EPS = 1e-6
GRID_W = 64
GLA_HEADS, GLA_DK, GLA_DV, GLA_RANK, GLA_TAU, GLA_CHUNK = 4, 128, 256, 16, 16.0, 64
POOL_WINDOWS = (2, 4, 8, 16)
POOL_GROUP_DIM = 256
MLA_HEADS, MLA_Q_LORA, MLA_KV_LORA, MLA_NOPE, MLA_ROPE, MLA_V = 8, 768, 512, 128, 64, 128
MLA_SCALE = 1.0 / math.sqrt(MLA_NOPE + MLA_ROPE)
ROPE_THETA = 10000.0
N_GROUPS, EXPERTS_PER_GROUP, EXPERT_FF = 4, 4, 512
SPLIT_SIZES = (512, 512, 1024, 1024, 32, 1024, 768, 512, 64, 6144)

LANES = 128
SLAB = 256
VMEM_LIMIT_BYTES = 60000 * 1024

C_GL = 0
C_Q = 6144
C_K = 6656
C_V = 7168
C_GR = 8192
C_PIN = 9216
C_KVA = 10240
C_QA = 10752
C_GLOW = 11520
C_KR = 11648
NP = 11776


def _dot(a, b):
    return jnp.dot(a, b, preferred_element_type=F32)


def _dot_nt(a, b):
    return lax.dot_general(a, b, (((1,), (1,)), ((), ())), preferred_element_type=F32)


def _dot_tn(a, b):
    return lax.dot_general(a, b, (((0,), (0,)), ((), ())), preferred_element_type=F32)


def _split3(x):
    hi = x.astype(BF16)
    r1 = x - hi.astype(F32)
    mid = r1.astype(BF16)
    lo = (r1 - mid.astype(F32)).astype(BF16)
    return hi, mid, lo


def _sigmoid(x):
    return 0.5 * jnp.tanh(0.5 * x) + 0.5


def _silu(x):
    return x * _sigmoid(x)


def _rms(x, n):
    ms = jnp.sum(x * x, axis=-1, keepdims=True) * (1.0 / n)
    return x * lax.rsqrt(ms + EPS)


def _cparams(sem, vmem=VMEM_LIMIT_BYTES):
    return pltpu.CompilerParams(dimension_semantics=sem, vmem_limit_bytes=vmem)


def _pick_tile(cands, *extents):
    for c in cands:
        if all(e % c == 0 for e in extents):
            return c
    raise ValueError(f"no tile in {cands} divides {extents}")


def _mod_kernel(c_ref, w_ref, b_ref, o_ref):
    a = _silu(c_ref[...]).astype(BF16)
    o_ref[0] = _dot(a, w_ref[0].astype(BF16)) + b_ref[0]


def _modulation(cond, w_ada, b_ada):
    tn = 1024
    out = pl.pallas_call(
        _mod_kernel,
        grid=(DEPTH, 6 * D // tn),
        in_specs=[
            pl.BlockSpec((8, D), lambda l, j: (0, 0)),
            pl.BlockSpec((1, D, tn), lambda l, j: (l, 0, j)),
            pl.BlockSpec((1, 1, tn), lambda l, j: (l, 0, j)),
        ],
        out_specs=pl.BlockSpec((1, 8, tn), lambda l, j: (l, 0, j)),
        out_shape=jax.ShapeDtypeStruct((DEPTH, 8, 6 * D), F32),
        compiler_params=_cparams(("parallel", "parallel"), 40 * 1024 * 1024),
        name="adaln_modulation",
    )(cond, w_ada, b_ada.reshape(DEPTH, 1, 6 * D))
    return out.reshape(DEPTH, 8, 6, D)


def _proj_in_kernel(x_ref, mod_ref, g_ref, w_ref, o_ref, h_scr, *, tm):
    @pl.when(pl.program_id(1) == 0)
    def _():
        m = mod_ref[0, 0]
        g = g_ref[0]

        def body(r, carry):
            rows = pl.ds(pl.multiple_of(r * LANES, LANES), LANES)
            y = _rms(x_ref[rows, :], D) * g
            h_scr[rows, :] = (y * (1.0 + m[1:2]) + m[0:1]).astype(BF16)
            return carry

        lax.fori_loop(0, tm // LANES, body, 0)

    o_ref[...] = _dot_nt(h_scr[...], w_ref[0]).astype(BF16)


def _proj_in(x, mods, g1, wp, l, dims):
    T = dims["t"]
    tm = _pick_tile((1024, 512, 256), dims["t_ctx"], dims["l_lat"])
    tn = 512
    gmap = _group_map(dims, tm)
    return pl.pallas_call(
        functools.partial(_proj_in_kernel, tm=tm),
        grid=(T // tm, NP // tn),
        in_specs=[
            pl.BlockSpec((tm, D), lambda i, j: (i, 0)),
            pl.BlockSpec((1, 1, 6, D), lambda i, j: (l, gmap(i), 0, 0)),
            pl.BlockSpec((1, 1, D), lambda i, j: (l, 0, 0)),
            pl.BlockSpec((1, tn, D), lambda i, j: (l, j, 0)),
        ],
        out_specs=pl.BlockSpec((tm, tn), lambda i, j: (i, j)),
        out_shape=jax.ShapeDtypeStruct((T, NP), BF16),
        scratch_shapes=[pltpu.VMEM((tm, D), BF16)],
        compiler_params=_cparams(("parallel", "arbitrary")),
        name="norm1_proj_in",
    )(x, mods, g1, wp)


def _group_map(dims, tm):
    n_ctx_tiles = dims["t_ctx"] // tm
    per_lat = dims["l_lat"] // tm

    def gmap(i):
        return jnp.where(i < n_ctx_tiles, 0, 1 + (i - n_ctx_tiles) // per_lat)

    return gmap


def _gla_kernel(fblk, bblk, first, last, sidx, unit,
                qf, kf, vf, gf, qb, kb, vb, gb, wdec, bdec, s0,
                of, ob, sout, s_scr):
    del fblk, bblk, sidx, unit
    s = pl.program_id(0)

    @pl.when(first[s] == 1)
    def _():
        s_scr[...] = s0[0]

    n_chunks = SLAB // GLA_CHUNK
    r = lax.broadcasted_iota(I32, (SLAB, SLAB), 0)
    c = lax.broadcasted_iota(I32, (SLAB, SLAB), 1)
    same = (r // GLA_CHUNK) == (c // GLA_CHUNK)

    def rows(x, ci):
        return x[ci * GLA_CHUNK:(ci + 1) * GLA_CHUNK]

    dirs = ((qf, kf, vf, gf, of), (qb, kb, vb, gb, ob))
    for d, (q_ref, k_ref, v_ref, g_ref, o_ref) in enumerate(dirs):
        tri = jnp.logical_and(same, (c <= r) if d == 0 else (c >= r))
        tri_b = tri.astype(BF16)
        dec = _dot(g_ref[...], wdec[d]) + bdec[d]
        la = (jnp.minimum(dec, 0.0) - jnp.log1p(jnp.exp(-jnp.abs(dec)))) * (1.0 / GLA_TAU)
        hi, mid, lo = _split3(la)
        b_all = _dot(tri_b, hi) + _dot(tri_b, mid) + _dot(tri_b, lo)
        edge = GLA_CHUNK - 1 if d == 0 else 0
        tot = [b_all[ci * GLA_CHUNK + edge:ci * GLA_CHUNK + edge + 1] for ci in range(n_chunks)]
        bl_all = jnp.concatenate([jnp.broadcast_to(t, (GLA_CHUNK, t.shape[1])) for t in tot], axis=0)
        order = range(n_chunks) if d == 0 else range(n_chunks - 1, -1, -1)
        for h in range(GLA_HEADS):
            kc = slice(h * GLA_DK, (h + 1) * GLA_DK)
            vc = slice(h * GLA_DV, (h + 1) * GLA_DV)
            b = b_all[:, kc]
            bl = bl_all[:, kc]
            q = q_ref[:, kc].astype(F32) * (GLA_DK ** -0.5)
            k = k_ref[:, kc].astype(F32)
            v = v_ref[:, vc]
            qt = (q * jnp.exp(b)).astype(BF16)
            kt = (k * jnp.exp(-b)).astype(BF16)
            ke = (k * jnp.exp(bl - b)).astype(BF16)
            a = jnp.where(tri, _dot_nt(qt, kt), 0.0).astype(BF16)
            o_intra = _dot(a, v)
            ds_t = [_dot_tn(rows(v, ci), rows(ke, ci)) for ci in range(n_chunks)]
            decay = [jnp.exp(tot[ci][:, kc]) for ci in range(n_chunks)]
            sd = s_scr[d, h]
            o_inter = [None] * n_chunks
            for ci in order:
                o_inter[ci] = _dot_nt(rows(qt, ci), sd.astype(BF16))
                sd = sd * decay[ci] + ds_t[ci]
            o_ref[:, vc] = (o_intra + jnp.concatenate(o_inter, axis=0)).astype(BF16)
            s_scr[d, h] = sd

            @pl.when(last[s] == 1)
            def _(sd=sd, d=d, h=h):
                sout[0, d, h] = sd


def _gla_tables(dims):
    ctx_slabs = dims["l_ctx"] // SLAB
    lat_slabs = dims["l_lat"] // SLAB
    fblk, bblk, first, last, sidx, unit = [], [], [], [], [], []
    base = 0
    for u in range(dims["b_ctx"] + dims["b_lat"]):
        is_ctx = u < dims["b_ctx"]
        n = ctx_slabs if is_ctx else lat_slabs
        for j in range(n):
            fblk.append(base + j)
            bblk.append(base + n - 1 - j)
            first.append(int(j == 0))
            last.append(int(j == n - 1))
            sidx.append(0 if is_ctx else 1 + u - dims["b_ctx"])
            unit.append(u)
        base += n
    return [jnp.asarray(np.asarray(t, np.int32)) for t in (fblk, bblk, first, last, sidx, unit)]


def _gla(p, wdec, bdec, s0, tables, dims):
    T = p.shape[0]
    n_steps = T // SLAB
    n_units = dims["b_ctx"] + dims["b_lat"]

    hk, hv = GLA_HEADS * GLA_DK, GLA_HEADS * GLA_DV
    state = (2, GLA_HEADS, GLA_DV, GLA_DK)

    def pspec(width, col0, which):
        cb = col0 // width
        if which == 0:
            return pl.BlockSpec((SLAB, width), lambda s, fb, bb, fi, la, si, un: (fb[s], cb))
        return pl.BlockSpec((SLAB, width), lambda s, fb, bb, fi, la, si, un: (bb[s], cb))

    in_specs = []
    for which in (0, 1):
        in_specs += [pspec(hk, C_Q, which), pspec(hk, C_K, which), pspec(hv, C_V, which),
                     pspec(LANES, C_GLOW, which)]
    in_specs += [
        pl.BlockSpec((2, LANES, hk), lambda s, *_: (0, 0, 0)),
        pl.BlockSpec((2, 1, hk), lambda s, *_: (0, 0, 0)),
        pl.BlockSpec((1,) + state, lambda s, fb, bb, fi, la, si, un: (si[s], 0, 0, 0, 0)),
    ]
    out_specs = [
        pl.BlockSpec((SLAB, hv), lambda s, fb, bb, fi, la, si, un: (fb[s], 0)),
        pl.BlockSpec((SLAB, hv), lambda s, fb, bb, fi, la, si, un: (bb[s], 0)),
        pl.BlockSpec((1,) + state, lambda s, fb, bb, fi, la, si, un: (un[s], 0, 0, 0, 0)),
    ]
    grid_spec = pltpu.PrefetchScalarGridSpec(
        num_scalar_prefetch=6, grid=(n_steps,),
        in_specs=in_specs, out_specs=out_specs,
        scratch_shapes=[pltpu.VMEM(state, F32)])
    return pl.pallas_call(
        _gla_kernel,
        grid_spec=grid_spec,
        out_shape=[jax.ShapeDtypeStruct((T, hv), BF16),
                   jax.ShapeDtypeStruct((T, hv), BF16),
                   jax.ShapeDtypeStruct((n_units,) + state, F32)],
        compiler_params=_cparams(("arbitrary",), 32 * 1024 * 1024),
        name="gla_bidirectional",
    )(*tables, p, p, p, p, p, p, p, p, wdec, bdec, s0)


def _pool_kernel(cur, prv, nxt, wp, sc, o_ref, *, n_ctx_slabs, ctx_slabs, lat_slabs):
    i = pl.program_id(0)
    is_ctx = i < n_ctx_slabs
    seq_slabs = jnp.where(is_ctx, ctx_slabs, lat_slabs)
    j = jnp.where(is_ctx, i % ctx_slabs, (i - n_ctx_slabs) % lat_slabs)
    has_prev = j > 0
    has_next = j < seq_slabs - 1
    seq_len = seq_slabs * SLAB
    r = lax.broadcasted_iota(I32, (SLAB, SLAB), 0)
    c = lax.broadcasted_iota(I32, (SLAB, SLAB), 1)
    t = j * SLAB + lax.broadcasted_iota(I32, (SLAB, 1), 0)
    gd = POOL_GROUP_DIM
    for g, w in enumerate(POOL_WINDOWS):
        lo_off, hi_off = w // 2, w - w // 2
        cols = slice(g * gd, (g + 1) * gd)
        u = cur[:, cols]
        b_cur = jnp.logical_and(c >= r - lo_off, c < r + hi_off)
        b_prv = jnp.logical_and(c - SLAB >= r - lo_off, has_prev)
        b_nxt = jnp.logical_and(c + SLAB < r + hi_off, has_next)
        ssum = (_dot(b_cur.astype(BF16), u) + _dot(b_prv.astype(BF16), prv[:, cols])
                + _dot(b_nxt.astype(BF16), nxt[:, cols]))
        cnt = (jnp.minimum(t + hi_off, seq_len) - jnp.maximum(t - lo_off, 0)).astype(F32)
        pooled = ssum / cnt - u.astype(F32)
        o_ref[:, cols] = (_dot(pooled.astype(BF16), wp[0, g]) * sc[0, :, cols]).astype(BF16)


def _pool(p, w_pool, pool_scale, l, dims):
    T = p.shape[0]
    n = T // SLAB
    cb = C_PIN // 1024
    kern = functools.partial(_pool_kernel, n_ctx_slabs=dims["t_ctx"] // SLAB,
                             ctx_slabs=dims["l_ctx"] // SLAB, lat_slabs=dims["l_lat"] // SLAB)
    return pl.pallas_call(
        kern,
        grid=(n,),
        in_specs=[
            pl.BlockSpec((SLAB, 1024), lambda i: (i, cb)),
            pl.BlockSpec((SLAB, 1024), lambda i: (jnp.maximum(i - 1, 0), cb)),
            pl.BlockSpec((SLAB, 1024), lambda i: (jnp.minimum(i + 1, n - 1), cb)),
            pl.BlockSpec((1, 4, 256, 256), lambda i: (l, 0, 0, 0)),
            pl.BlockSpec((1, 1, 1024), lambda i: (l, 0, 0)),
        ],
        out_specs=pl.BlockSpec((SLAB, 1024), lambda i: (i, 0)),
        out_shape=jax.ShapeDtypeStruct((T, 1024), BF16),
        compiler_params=_cparams(("parallel",), 32 * 1024 * 1024),
        name="pool_mixer",
    )(p, p, p, w_pool, pool_scale)


def _swap16(x):
    lane = lax.broadcasted_iota(I32, x.shape, x.ndim - 1)
    n = x.shape[-1]
    fwd = pltpu.roll(x, n - 16, x.ndim - 1)
    bwd = pltpu.roll(x, 16, x.ndim - 1)
    return jnp.where((lane % 32) < 16, fwd, bwd)


def _mla_prep_kernel(kva_ref, qa_ref, kr_ref, cos_ref, sin_ref, wq, wkv,
                     g_qa, g_kva, g_qn, g_qr, g_kn, g_kr,
                     qr_ref, qu_ref, k_ref, v_ref, ckv_ref, kro_ref):
    ckv = _rms(kva_ref[...].astype(F32), MLA_KV_LORA) * g_kva[0]
    ckv_ref[...] = ckv
    kv = _dot(ckv.astype(BF16), wkv[0])
    qn = _rms(qa_ref[...].astype(F32), MLA_Q_LORA) * g_qa[0]
    q = _dot(qn.astype(BF16), wq[0])
    cos = cos_ref[...]
    sin = sin_ref[...]
    kr = _rms(kr_ref[...].astype(F32), MLA_ROPE) * g_kr[0]
    kro_ref[...] = kr
    kr_rot = (kr * cos + _swap16(kr) * sin).astype(BF16)
    for h in range(MLA_HEADS):
        c0 = 2 * LANES * h
        q_nope = _rms(q[:, c0:c0 + LANES], MLA_NOPE) * g_qn[0] * MLA_SCALE
        q_rope = _rms(q[:, c0 + LANES:c0 + 2 * LANES], MLA_ROPE) * g_qr[0]
        q_rot = q_rope * cos + _swap16(q_rope) * sin
        qr_ref[:, c0:c0 + LANES] = q_nope.astype(BF16)
        qu_ref[:, c0:c0 + LANES] = q_nope.astype(BF16)
        qr_ref[:, c0 + LANES:c0 + 2 * LANES] = (q_rot * MLA_SCALE).astype(BF16)
        qu_ref[:, c0 + LANES:c0 + 2 * LANES] = (q_rope * MLA_SCALE).astype(BF16)
        k_nope = _rms(kv[:, c0:c0 + LANES], MLA_NOPE) * g_kn[0]
        k_ref[:, c0:c0 + LANES] = k_nope.astype(BF16)
        k_ref[:, c0 + LANES:c0 + 2 * LANES] = kr_rot
        v_ref[:, LANES * h:LANES * (h + 1)] = kv[:, c0 + LANES:c0 + 2 * LANES].astype(BF16)


def _mla_prep(p, cos, sin, wq, wkv, gains, l, dims):
    T = p.shape[0]
    tm = _pick_tile((512, 256), dims["t_ctx"], dims["l_lat"])
    hw = MLA_HEADS * 2 * LANES

    def gspec(n):
        return pl.BlockSpec((1, 1, n), lambda i: (l, 0, 0))

    return pl.pallas_call(
        _mla_prep_kernel,
        grid=(T // tm,),
        in_specs=[
            pl.BlockSpec((tm, MLA_KV_LORA), lambda i: (i, C_KVA // MLA_KV_LORA)),
            pl.BlockSpec((tm, MLA_Q_LORA), lambda i: (i, C_QA // MLA_Q_LORA)),
            pl.BlockSpec((tm, LANES), lambda i: (i, C_KR // LANES)),
            pl.BlockSpec((tm, LANES), lambda i: (i, 0)),
            pl.BlockSpec((tm, LANES), lambda i: (i, 0)),
            pl.BlockSpec((1, MLA_Q_LORA, hw), lambda i: (l, 0, 0)),
            pl.BlockSpec((1, MLA_KV_LORA, hw), lambda i: (l, 0, 0)),
            gspec(MLA_Q_LORA), gspec(MLA_KV_LORA), gspec(LANES), gspec(LANES), gspec(LANES), gspec(LANES),
        ],
        out_specs=[
            pl.BlockSpec((tm, hw), lambda i: (i, 0)),
            pl.BlockSpec((tm, hw), lambda i: (i, 0)),
            pl.BlockSpec((tm, hw), lambda i: (i, 0)),
            pl.BlockSpec((tm, MLA_HEADS * MLA_V), lambda i: (i, 0)),
            pl.BlockSpec((tm, MLA_KV_LORA), lambda i: (i, 0)),
            pl.BlockSpec((tm, LANES), lambda i: (i, 0)),
        ],
        out_shape=[
            jax.ShapeDtypeStruct((T, hw), BF16),
            jax.ShapeDtypeStruct((T, hw), BF16),
            jax.ShapeDtypeStruct((T, hw), BF16),
            jax.ShapeDtypeStruct((T, MLA_HEADS * MLA_V), BF16),
            jax.ShapeDtypeStruct((T, MLA_KV_LORA), F32),
            jax.ShapeDtypeStruct((T, LANES), F32),
        ],
        compiler_params=_cparams(("parallel",), 48 * 1024 * 1024),
        name="mla_prep",
    )(p, p, p, cos, sin, wq, wkv, *gains)


def _cache_kv_kernel(ckv_ref, kr_ref, wkv, g_kn, k_ref, v_ref):
    kv = _dot(ckv_ref[0, 0].astype(BF16), wkv[0])
    kr = kr_ref[0, 0].astype(BF16)
    for h in range(MLA_HEADS):
        c0 = 2 * LANES * h
        k_nope = _rms(kv[:, c0:c0 + LANES], MLA_NOPE) * g_kn[0]
        k_ref[0, 0, :, c0:c0 + LANES] = k_nope.astype(BF16)
        k_ref[0, 0, :, c0 + LANES:c0 + 2 * LANES] = kr
        v_ref[0, 0, :, LANES * h:LANES * (h + 1)] = kv[:, c0 + LANES:c0 + 2 * LANES].astype(BF16)


def _cache_kv(cache_ckv, cache_kr, wkv, g_kn):
    b_lat, _, past, _ = cache_ckv.shape
    hw = MLA_HEADS * 2 * LANES
    return pl.pallas_call(
        _cache_kv_kernel,
        grid=(DEPTH, b_lat),
        in_specs=[
            pl.BlockSpec((1, 1, past, MLA_KV_LORA), lambda l, b: (b, l, 0, 0)),
            pl.BlockSpec((1, 1, past, LANES), lambda l, b: (b, l, 0, 0)),
            pl.BlockSpec((1, MLA_KV_LORA, hw), lambda l, b: (l, 0, 0)),
            pl.BlockSpec((1, 1, LANES), lambda l, b: (l, 0, 0)),
        ],
        out_specs=[
            pl.BlockSpec((1, 1, past, hw), lambda l, b: (l, b, 0, 0)),
            pl.BlockSpec((1, 1, past, MLA_HEADS * MLA_V), lambda l, b: (l, b, 0, 0)),
        ],
        out_shape=[
            jax.ShapeDtypeStruct((DEPTH, b_lat, past, hw), BF16),
            jax.ShapeDtypeStruct((DEPTH, b_lat, past, MLA_HEADS * MLA_V), BF16),
        ],
        compiler_params=_cparams(("parallel", "parallel"), 32 * 1024 * 1024),
        name="mla_cache_decompress",
    )(cache_ckv, cache_kr, wkv, g_kn)


def _attn_ctx_kernel(q_ref, k_ref, v_ref, o_ref):
    for h in range(MLA_HEADS):
        qk = slice(h * 2 * LANES, (h + 1) * 2 * LANES)
        vs = slice(h * MLA_V, (h + 1) * MLA_V)
        s = _dot_nt(q_ref[:, qk], k_ref[:, qk])
        m = jnp.max(s, axis=-1, keepdims=True)
        pr = jnp.exp(s - m)
        den = jnp.sum(pr, axis=-1, keepdims=True)
        o_ref[:, vs] = (_dot(pr.astype(BF16), v_ref[:, vs]) / den).astype(BF16)


def _attn_ctx(q, k, v, dims):
    lc = dims["l_ctx"]
    t_ctx = dims["t_ctx"]
    hw = MLA_HEADS * 2 * LANES
    return pl.pallas_call(
        _attn_ctx_kernel,
        grid=(dims["b_ctx"],),
        in_specs=[
            pl.BlockSpec((lc, hw), lambda s: (s, 0)),
            pl.BlockSpec((lc, hw), lambda s: (s, 0)),
            pl.BlockSpec((lc, MLA_HEADS * MLA_V), lambda s: (s, 0)),
        ],
        out_specs=pl.BlockSpec((lc, MLA_HEADS * MLA_V), lambda s: (s, 0)),
        out_shape=jax.ShapeDtypeStruct((t_ctx, MLA_HEADS * MLA_V), BF16),
        compiler_params=_cparams(("parallel",), 32 * 1024 * 1024),
        name="mla_attention_context",
    )(q, k, v)


ATTN_HEADS_PER_STEP = 2
ATTN_KEY_PARTS = 2


def _attn_lat_kernel(qr_ref, qu_ref, k_ref, v_ref, kc_ref, vc_ref, o_ref):
    for h in range(ATTN_HEADS_PER_STEP):
        qk = slice(h * 2 * LANES, (h + 1) * 2 * LANES)
        vs = slice(h * MLA_V, (h + 1) * MLA_V)
        n_lat = k_ref.shape[0]
        part = n_lat // ATTN_KEY_PARTS
        qr = qr_ref[:, qk]
        stats = []
        for c in range(ATTN_KEY_PARTS):
            ks = slice(c * part, (c + 1) * part)
            s = _dot_nt(qr, k_ref[ks, qk])
            mc = jnp.max(s, axis=-1, keepdims=True)
            if c == ATTN_KEY_PARTS - 1:
                s2 = _dot_nt(qu_ref[:, qk], kc_ref[0, 0, :, qk])
                mc = jnp.maximum(mc, jnp.max(s2, axis=-1, keepdims=True))
            pr = jnp.exp(s - mc)
            lc = jnp.sum(pr, axis=-1, keepdims=True)
            oc = _dot(pr.astype(BF16), v_ref[ks, vs])
            if c == ATTN_KEY_PARTS - 1:
                p2 = jnp.exp(s2 - mc)
                lc = lc + jnp.sum(p2, axis=-1, keepdims=True)
                oc = oc + _dot(p2.astype(BF16), vc_ref[0, 0, :, vs])
            stats.append((mc, lc, oc))
        m = functools.reduce(jnp.maximum, [st[0] for st in stats])
        num = den = None
        for mc, lc, oc in stats:
            wc = jnp.exp(mc - m)
            num = oc * wc if num is None else num + oc * wc
            den = lc * wc if den is None else den + lc * wc
        o_ref[:, vs] = (num / den).astype(BF16)


def _attn_lat(qr, qu, k, v, kc, vc, l, dims):
    ll = dims["l_lat"]
    past = kc.shape[2]
    tq = _pick_tile((256,), ll)
    q0 = dims["t_ctx"] // tq
    k0 = dims["t_ctx"] // ll
    nq = ll // tq
    hp = ATTN_HEADS_PER_STEP
    qw, vw = hp * 2 * LANES, hp * MLA_V
    return pl.pallas_call(
        _attn_lat_kernel,
        grid=(dims["b_lat"], MLA_HEADS // hp, nq),
        in_specs=[
            pl.BlockSpec((tq, qw), lambda b, h, i: (q0 + b * nq + i, h)),
            pl.BlockSpec((tq, qw), lambda b, h, i: (q0 + b * nq + i, h)),
            pl.BlockSpec((ll, qw), lambda b, h, i: (k0 + b, h)),
            pl.BlockSpec((ll, vw), lambda b, h, i: (k0 + b, h)),
            pl.BlockSpec((1, 1, past, qw), lambda b, h, i: (l, b, 0, h)),
            pl.BlockSpec((1, 1, past, vw), lambda b, h, i: (l, b, 0, h)),
        ],
        out_specs=pl.BlockSpec((tq, vw), lambda b, h, i: (b * nq + i, h)),
        out_shape=jax.ShapeDtypeStruct((dims["b_lat"] * ll, MLA_HEADS * MLA_V), BF16),
        compiler_params=_cparams(("parallel", "parallel", "arbitrary")),
        name="mla_attention_latent",
    )(qr, qu, k, v, kc, vc)


def _route(logits_t):
    ng, ne = N_GROUPS, EXPERTS_PER_GROUP
    lg = [logits_t[g:g + 1] for g in range(ng)]
    mg = functools.reduce(jnp.maximum, lg)
    zg = functools.reduce(lambda a, b: a + b, [jnp.exp(x - mg) for x in lg])
    pg_top = 1.0 / zg
    grp = jnp.full_like(mg, float(ng))
    for g in range(ng - 1, -1, -1):
        grp = jnp.where(lg[g] == mg, float(g), grp)
    el = []
    for j in range(ne):
        acc = jnp.zeros_like(mg)
        for g in range(ng):
            row = ng + g * ne + j
            acc = jnp.where(grp == float(g), logits_t[row:row + 1], acc)
        el.append(acc)
    m1 = functools.reduce(jnp.maximum, el)
    i1 = jnp.full_like(mg, float(ne))
    for j in range(ne - 1, -1, -1):
        i1 = jnp.where(el[j] == m1, float(j), i1)
    neg = jnp.full_like(mg, -jnp.inf)
    rest = [jnp.where(i1 == float(j), neg, el[j]) for j in range(ne)]
    m2 = functools.reduce(jnp.maximum, rest)
    i2 = jnp.full_like(mg, float(ne))
    for j in range(ne - 1, -1, -1):
        i2 = jnp.where(rest[j] == m2, float(j), i2)
    e2 = jnp.exp(m2 - m1)
    w1 = 1.0 / (1.0 + e2)
    w2 = e2 / (1.0 + e2)
    rows = [pg_top * (jnp.where(i1 == float(j), w1, 0.0) + jnp.where(i2 == float(j), w2, 0.0))
            for j in range(ne)]
    rows += [grp, jnp.zeros_like(mg), jnp.zeros_like(mg), jnp.zeros_like(mg)]
    return jnp.concatenate(rows, axis=0)


def _branch_kernel(of_ref, ob_ref, gr_ref, pool_ref, actx_ref, alat_ref, gl0, gl1, gl2,
                   wbg, wbp, wbm, g_gla, o_ref, a_scr, attn_ref, *, n_ctx_tiles):
    gg = g_gla[0]

    @pl.when(pl.program_id(0) < n_ctx_tiles)
    def _():
        attn_ref[...] = actx_ref[...]

    @pl.when(pl.program_id(0) >= n_ctx_tiles)
    def _():
        attn_ref[...] = alat_ref[...]

    for h in range(GLA_HEADS):
        cols = slice(h * GLA_DV, (h + 1) * GLA_DV)
        o = of_ref[:, cols].astype(F32) + ob_ref[:, cols].astype(F32)
        a_scr[:, cols] = (_rms(o, GLA_DV) * gg * _silu(gr_ref[:, cols].astype(F32))).astype(BF16)
    nb = 512
    for n in range(D // nb):
        cols = slice(n * nb, (n + 1) * nb)
        acc = _sigmoid(gl0[:, cols].astype(F32)) * _dot(a_scr[...], wbg[0, :, cols])
        acc += _sigmoid(gl1[:, cols].astype(F32)) * _dot(pool_ref[...], wbp[0, :, cols])
        acc += _sigmoid(gl2[:, cols].astype(F32)) * _dot(attn_ref[...], wbm[0, :, cols])
        o_ref[:, cols] = acc.astype(BF16)


def _branch_merge(o_f, o_b, p, pool_out, attn_ctx, attn_lat, w, l, dims):
    T = dims["t"]
    tm = _pick_tile((512, 256), dims["t_ctx"], dims["l_lat"])
    n_ctx_tiles = dims["t_ctx"] // tm

    def wspec(shape):
        return pl.BlockSpec((1,) + shape, lambda i: (l,) + (0,) * len(shape), pipeline_mode=pl.Buffered(1))

    return pl.pallas_call(
        functools.partial(_branch_kernel, n_ctx_tiles=n_ctx_tiles),
        grid=(T // tm,),
        in_specs=[
            pl.BlockSpec((tm, 1024), lambda i: (i, 0)),
            pl.BlockSpec((tm, 1024), lambda i: (i, 0)),
            pl.BlockSpec((tm, 1024), lambda i: (i, C_GR // 1024)),
            pl.BlockSpec((tm, 1024), lambda i: (i, 0)),
            pl.BlockSpec((tm, 1024), lambda i: (jnp.minimum(i, n_ctx_tiles - 1), 0)),
            pl.BlockSpec((tm, 1024), lambda i: (jnp.maximum(i - n_ctx_tiles, 0), 0)),
            pl.BlockSpec((tm, D), lambda i: (i, 0)),
            pl.BlockSpec((tm, D), lambda i: (i, 1)),
            pl.BlockSpec((tm, D), lambda i: (i, 2)),
            wspec((1024, D)), wspec((1024, D)), wspec((1024, D)),
            pl.BlockSpec((1, 1, GLA_DV), lambda i: (l, 0, 0)),
        ],
        out_specs=pl.BlockSpec((tm, D), lambda i: (i, 0)),
        out_shape=jax.ShapeDtypeStruct((T, D), BF16),
        scratch_shapes=[pltpu.VMEM((tm, 1024), BF16), pltpu.VMEM((tm, 1024), BF16)],
        compiler_params=_cparams(("parallel",)),
        name="branch_merge",
    )(o_f, o_b, p, pool_out, attn_ctx, attn_lat, p, p, p,
      w["w_br_gla"], w["w_br_pool"], w["w_br_mla"], w["g_gla"])


def _merge_kernel(x_ref, mod_ref, mg_ref, wo, g2_ref, wr, br, xo_ref, r_ref, *, tm):
    m = mod_ref[0, 0]
    xm = x_ref[...] + m[2:3] * _dot(mg_ref[...], wo[0])
    xo_ref[...] = xm
    g2 = g2_ref[0]
    for ci in range(tm // LANES):
        rows = slice(ci * LANES, (ci + 1) * LANES)
        h2 = _rms(xm[rows], D) * g2 * (1.0 + m[4:5]) + m[3:4]
        h_hi = h2.astype(BF16)
        hs = (h_hi, (h2 - h_hi.astype(F32)).astype(BF16))
        lt = br[0]
        for ia, ib in ((0, 0), (0, 1), (1, 0)):
            lt = lt + _dot_nt(wr[0, ia], hs[ib])
        r_ref[:, rows] = _route(lt)


def _merge(x, mods, merged, w, l, dims):
    T = dims["t"]
    tm = _pick_tile((512, 256), dims["t_ctx"], dims["l_lat"])
    gmap = _group_map(dims, tm)
    return pl.pallas_call(
        functools.partial(_merge_kernel, tm=tm),
        grid=(T // tm,),
        in_specs=[
            pl.BlockSpec((tm, D), lambda i: (i, 0)),
            pl.BlockSpec((1, 1, 6, D), lambda i: (l, gmap(i), 0, 0)),
            pl.BlockSpec((tm, D), lambda i: (i, 0)),
            pl.BlockSpec((1, D, D), lambda i: (l, 0, 0), pipeline_mode=pl.Buffered(1)),
            pl.BlockSpec((1, 1, D), lambda i: (l, 0, 0)),
            pl.BlockSpec((1, 2, 32, D), lambda i: (l, 0, 0, 0)),
            pl.BlockSpec((1, 32, LANES), lambda i: (l, 0, 0)),
        ],
        out_specs=[
            pl.BlockSpec((tm, D), lambda i: (i, 0)),
            pl.BlockSpec((8, tm), lambda i: (0, i)),
        ],
        out_shape=[jax.ShapeDtypeStruct((T, D), F32), jax.ShapeDtypeStruct((8, T), F32)],
        compiler_params=_cparams(("parallel",)),
        name="merge_out_route",
    )(x, mods, merged, w["w_o"], w["g_norm2"], w["router_w"], w["router_b"])


def _moe_kernel(tg, nvalid, gi_ref, gn_ref, si_ref, x_hbm, cm_ref, mod_ref, g2_ref, wg, wu, wd,
                o_hbm, xbuf, obuf, h_scr, gsem, ssem, *, tm, nt, n_cond):
    del tg
    t = pl.program_id(0)
    slot = t % 2

    def used(i):
        return nvalid[jnp.clip(i, 0, nt - 1)] > 0

    def gather_copy(sl, r, row):
        return pltpu.make_async_copy(x_hbm.at[pl.ds(row, 1)], xbuf.at[sl, pl.ds(r, 1)], gsem.at[sl])

    def scatter_copy(sl, r, row):
        return pltpu.make_async_copy(obuf.at[sl, pl.ds(r, 1)], o_hbm.at[pl.ds(row, 1)], ssem.at[sl])

    def start_rows(make, idx_ref):
        for r in range(tm):
            make(r, idx_ref[0, 0, r]).start(priority=r % 2)

    def wait_gather(sl):
        pltpu.make_async_copy(x_hbm.at[pl.ds(0, tm)], xbuf.at[sl], gsem.at[sl]).wait()

    def wait_scatter(sl):
        pltpu.make_async_copy(obuf.at[sl], o_hbm.at[pl.ds(0, tm)], ssem.at[sl]).wait()

    @pl.when(t == 0)
    def _():
        obuf[1] = jnp.zeros((tm, D), F32)
        n_tok = o_hbm.shape[0] - N_GROUPS * tm
        for g in range(N_GROUPS):
            pltpu.make_async_copy(obuf.at[1], o_hbm.at[pl.ds(n_tok + g * tm, tm)], ssem.at[1]).start()
        for g in range(N_GROUPS):
            wait_scatter(1)

    @pl.when(jnp.logical_and(t == 0, used(0)))
    def _():
        start_rows(functools.partial(gather_copy, 0), gi_ref)

    has_next = jnp.logical_and(t + 1 < nt, used(t + 1))

    @pl.when(jnp.logical_and(t >= 2, used(t - 2)))
    def _():
        wait_scatter(slot)

    @pl.when(used(t))
    def _():
        wait_gather(slot)
        mods = mod_ref[0]
        g2 = g2_ref[0]

        def cond_rows(cg, k):
            out = mods[0, k:k + 1]
            for c in range(1, n_cond):
                out = jnp.where(cg == float(c), mods[c, k:k + 1], out)
            return out

        for r in range(tm // LANES):
            rows = slice(r * LANES, (r + 1) * LANES)
            cg = cm_ref[rows, 4:5]
            h2 = _rms(xbuf[slot, rows, :], D) * g2 * (1.0 + cond_rows(cg, 4)) + cond_rows(cg, 3)
            h_scr[rows, :] = h2.astype(BF16)

        hb = h_scr[...]
        y = None
        per = tm // EXPERTS_PER_GROUP
        for e in range(EXPERTS_PER_GROUP):
            for r in range(e * per, (e + 1) * per):
                gather_copy(1 - slot, r, gn_ref[0, 0, r]).start(priority=r % 2)
            hid = _silu(_dot(hb, wg[0, e])) * _dot(hb, wu[0, e]) * cm_ref[:, e:e + 1]
            part = _dot(hid.astype(BF16), wd[0, e])
            y = part if y is None else y + part
        for r in range(tm // LANES):
            rows = slice(r * LANES, (r + 1) * LANES)
            cg = cm_ref[rows, 4:5]
            obuf[slot, rows, :] = xbuf[slot, rows, :] + cond_rows(cg, 5) * y[rows]
        start_rows(functools.partial(scatter_copy, slot), si_ref)

    @pl.when(jnp.logical_and(used(t), jnp.logical_not(has_next)))
    def _():
        wait_gather(1 - slot)

    @pl.when(jnp.logical_and(t == nt - 1, used(t - 1)))
    def _():
        wait_scatter(1 - slot)

    @pl.when(jnp.logical_and(t == nt - 1, used(t)))
    def _():
        wait_scatter(slot)


def _moe_plan(route, dims, tm):
    T = route.shape[1]
    nt = T // tm + N_GROUPS
    grp = route[4].astype(I32)
    _, order, *comb_sorted = lax.sort(
        (grp, jnp.arange(T, dtype=I32), route[0], route[1], route[2], route[3]), num_keys=1, is_stable=True)
    gids = jnp.arange(N_GROUPS, dtype=I32)
    counts = jnp.sum(grp[None, :] == gids[:, None], axis=1).astype(I32)
    tiles = (counts + tm - 1) // tm
    tile_end = jnp.cumsum(tiles)
    tile_start = tile_end - tiles
    tok_start = jnp.cumsum(counts) - counts
    tidx = jnp.arange(nt, dtype=I32)
    tg = jnp.minimum(jnp.sum(tidx[:, None] >= tile_end[None, :], axis=1), N_GROUPS - 1).astype(I32)
    onehot = (tg[:, None] == gids[None, :]).astype(I32)
    used = tidx < tile_end[-1]
    in_group = (tidx - jnp.sum(onehot * tile_start[None, :], axis=1)) * tm
    nvalid = jnp.where(used, jnp.clip(jnp.sum(onehot * counts[None, :], axis=1) - in_group, 0, tm), 0)
    nvalid = nvalid.astype(I32)
    slot = jnp.arange(tm, dtype=I32)[None, :]
    valid = slot < nvalid[:, None]
    shift = tile_start * tm - tok_start
    pad = jnp.zeros((nt * tm - T,), I32)
    ints = jnp.concatenate([order, pad])
    flts = jnp.concatenate([jnp.stack(comb_sorted, axis=0), jnp.zeros((4, nt * tm - T), F32)], axis=1)
    slot_g = jnp.broadcast_to(tg[:, None], (nt, tm)).reshape(nt * tm)
    src = jnp.zeros((nt * tm,), I32)
    comb = jnp.zeros((4, nt * tm), F32)
    for g in range(N_GROUPS):
        src = jnp.where(slot_g == g, jnp.roll(ints, shift[g]), src)
        comb = jnp.where((slot_g == g)[None, :], jnp.roll(flts, shift[g], axis=1), comb)
    src = jnp.where(valid, src.reshape(nt, tm), 0)
    comb = jnp.where(valid[None], comb.reshape(4, nt, tm), 0.0)
    dst = jnp.where(valid, src, T + tg[:, None] * tm + slot).astype(I32)
    cond = jnp.where(src < dims["t_ctx"], 0, 1 + (src - dims["t_ctx"]) // dims["l_lat"]).astype(F32)
    cm = jnp.concatenate([comb, cond[None], jnp.zeros((3,) + cond.shape, F32)], axis=0)
    cm = cm.transpose(1, 2, 0).reshape(nt * tm, 8)
    return tg, nvalid, src.reshape(nt, 1, tm), dst.reshape(nt, 1, tm), cm


def _moe_tile(dims):
    return 512 if dims["t"] % 512 == 0 else 256


def _moe(x, route, mods, w, l, dims):
    T = dims["t"]
    tm = _moe_tile(dims)
    nt = T // tm + N_GROUPS
    tg, nvalid, src, dst, cm = _moe_plan(route, dims, tm)
    ff = EXPERT_FF

    def wspec(shape):
        return pl.BlockSpec((1, EXPERTS_PER_GROUP) + shape, lambda t, tg_, nv: (l, tg_[t], 0, 0),
                            pipeline_mode=pl.Buffered(1))

    grid_spec = pltpu.PrefetchScalarGridSpec(
        num_scalar_prefetch=2, grid=(nt,),
        in_specs=[
            pl.BlockSpec((1, 1, tm), lambda t, tg_, nv: (t, 0, 0), memory_space=pltpu.SMEM),
            pl.BlockSpec((1, 1, tm), lambda t, tg_, nv: (jnp.minimum(t + 1, nt - 1), 0, 0),
                         memory_space=pltpu.SMEM),
            pl.BlockSpec((1, 1, tm), lambda t, tg_, nv: (t, 0, 0), memory_space=pltpu.SMEM),
            pl.BlockSpec(memory_space=pl.ANY),
            pl.BlockSpec((tm, 8), lambda t, tg_, nv: (t, 0)),
            pl.BlockSpec((1, 8, 6, D), lambda t, tg_, nv: (l, 0, 0, 0)),
            pl.BlockSpec((1, 1, D), lambda t, tg_, nv: (l, 0, 0)),
            wspec((D, ff)), wspec((D, ff)), wspec((ff, D)),
        ],
        out_specs=pl.BlockSpec(memory_space=pl.ANY),
        scratch_shapes=[pltpu.VMEM((2, tm, D), F32), pltpu.VMEM((2, tm, D), F32), pltpu.VMEM((tm, D), BF16),
                        pltpu.SemaphoreType.DMA((2,)), pltpu.SemaphoreType.DMA((2,))])
    return pl.pallas_call(
        functools.partial(_moe_kernel, tm=tm, nt=nt, n_cond=1 + dims["b_lat"]),
        grid_spec=grid_spec,
        out_shape=jax.ShapeDtypeStruct((T + N_GROUPS * tm, D), F32),
        compiler_params=_cparams(("arbitrary",)),
        name="moe_group_experts",
    )(tg, nvalid, src, src, dst, x, cm, mods, w["g_norm2"], w["w_exp_gate"], w["w_exp_up"], w["w_exp_down"])


def _pack_params(w_in, w_gla_dec, b_gla_dec, w_mla_q_up, g_q_rope, g_k_rope,
                 w_group_router, b_group_router, w_expert_router, b_expert_router):
    wt = jnp.swapaxes(w_in, 1, 2)
    gq, gk, gv, gr, glow, pin, qa, kva, kr, gl = jnp.split(
        wt, [int(v) for v in np.cumsum(SPLIT_SIZES)[:-1]], axis=1)

    def padr(a, n):
        return jnp.pad(a, ((0, 0), (0, n - a.shape[1]), (0, 0)))

    pieces = [gl, gq, gk, gv, gr, pin, kva, qa, padr(glow, LANES), padr(kr, LANES)]
    wp = jnp.concatenate([a.astype(BF16) for a in pieces], axis=1)
    wdec = jnp.zeros((DEPTH, 2, LANES, GLA_HEADS * GLA_DK), F32)
    for d in range(2):
        wdec = wdec.at[:, d, d * GLA_RANK:(d + 1) * GLA_RANK, :].set(w_gla_dec[:, d])
    wdec = wdec.astype(BF16)
    bdec = b_gla_dec.reshape(DEPTH, 2, 1, GLA_HEADS * GLA_DK)
    wq = w_mla_q_up.reshape(DEPTH, MLA_Q_LORA, MLA_HEADS, MLA_NOPE + MLA_ROPE)
    wq = jnp.pad(wq, ((0, 0), (0, 0), (0, 0), (0, 2 * LANES - MLA_NOPE - MLA_ROPE)))
    wq = wq.reshape(DEPTH, MLA_Q_LORA, MLA_HEADS * 2 * LANES).astype(BF16)
    g_qr = jnp.pad(g_q_rope, ((0, 0), (0, LANES - MLA_ROPE))).reshape(DEPTH, 1, LANES)
    g_kr = jnp.pad(g_k_rope, ((0, 0), (0, LANES - MLA_ROPE))).reshape(DEPTH, 1, LANES)
    wr = jnp.concatenate([w_group_router, w_expert_router], axis=-1).transpose(0, 2, 1)
    wr = jnp.pad(wr, ((0, 0), (0, 32 - wr.shape[1]), (0, 0)))
    hi = wr.astype(BF16)
    lo = (wr - hi.astype(F32)).astype(BF16)
    router_w = jnp.stack([hi, lo], axis=1)
    rb = jnp.concatenate([b_group_router, b_expert_router], axis=-1)
    rb = jnp.pad(rb, ((0, 0), (0, 32 - rb.shape[1])))
    router_b = jnp.broadcast_to(rb[:, :, None], (DEPTH, 32, LANES))
    return wp, wdec, bdec, wq, g_qr, g_kr, router_w, router_b


def _rope_tables(dims):
    ll = dims["l_lat"]
    t = jnp.arange(ll)
    row = (t // GRID_W).astype(F32)
    col = (t % GRID_W).astype(F32)
    n_freq = MLA_ROPE // 4
    inv = ROPE_THETA ** (-jnp.arange(n_freq, dtype=F32) / n_freq)
    ang = jnp.stack([row[:, None] * inv, col[:, None] * inv], axis=1)
    cos, sin = jnp.cos(ang), jnp.sin(ang)
    cos64 = jnp.concatenate([cos, cos], axis=-1).reshape(ll, MLA_ROPE)
    sin64 = jnp.concatenate([-sin, sin], axis=-1).reshape(ll, MLA_ROPE)
    pad = jnp.zeros((ll, LANES - MLA_ROPE), F32)
    cos_l = jnp.tile(jnp.concatenate([cos64, pad], axis=-1), (dims["b_lat"], 1))
    sin_l = jnp.tile(jnp.concatenate([sin64, pad], axis=-1), (dims["b_lat"], 1))
    cos_c = jnp.concatenate([jnp.ones((dims["t_ctx"], MLA_ROPE), F32),
                             jnp.zeros((dims["t_ctx"], LANES - MLA_ROPE), F32)], axis=-1)
    sin_c = jnp.zeros((dims["t_ctx"], LANES), F32)
    return jnp.concatenate([cos_c, cos_l], axis=0), jnp.concatenate([sin_c, sin_l], axis=0)


def kernel(x_prompt, x_sample, c, cache_mla_ckv, cache_mla_krope, state_gla, c_ctx, w_ada, b_ada, g_norm1, g_norm2, w_in, w_gla_dec, b_gla_dec, g_gla, w_pool, pool_scale, g_mla_qa, w_mla_q_up, g_mla_kva, w_mla_kv_up, g_q_nope, g_q_rope, g_k_nope, g_k_rope, w_br_gla, w_br_pool, w_br_mla, w_o, w_group_router, b_group_router, w_expert_router, b_expert_router, w_exp_gate, w_exp_up, w_exp_down):
    b_ctx, l_ctx, _ = x_prompt.shape
    b_lat, l_lat, _ = x_sample.shape
    dims = dict(b_ctx=b_ctx, l_ctx=l_ctx, b_lat=b_lat, l_lat=l_lat,
                t_ctx=b_ctx * l_ctx, t_lat=b_lat * l_lat, t=b_ctx * l_ctx + b_lat * l_lat)
    t_ctx = dims["t_ctx"]
    assert l_ctx % SLAB == 0 and l_lat % SLAB == 0 and t_ctx % l_lat == 0 and 1 + b_lat <= 8

    wp, wdec, bdec, wq, g_qr, g_kr, router_w, router_b = _pack_params(
        w_in, w_gla_dec, b_gla_dec, w_mla_q_up, g_q_rope, g_k_rope,
        w_group_router, b_group_router, w_expert_router, b_expert_router)
    wkv = w_mla_kv_up.astype(BF16)
    r3 = lambda a: a.reshape(DEPTH, 1, a.shape[-1])
    gains = (r3(g_mla_qa), r3(g_mla_kva), r3(g_q_nope), g_qr, r3(g_k_nope), g_kr)
    w = dict(w_br_gla=w_br_gla.astype(BF16), w_br_pool=w_br_pool.astype(BF16),
             w_br_mla=w_br_mla.astype(BF16), w_o=w_o.astype(BF16), g_gla=r3(g_gla),
             g_norm2=r3(g_norm2), router_w=router_w, router_b=router_b,
             w_exp_gate=w_exp_gate.astype(BF16), w_exp_up=w_exp_up.astype(BF16),
             w_exp_down=w_exp_down.astype(BF16))
    w_pool_b = w_pool.astype(BF16)
    pool_scale3 = r3(pool_scale)
    g1 = r3(g_norm1)

    cond = jnp.concatenate([c_ctx[None, :], c, jnp.zeros((8 - 1 - b_lat, D), F32)], axis=0)
    mods = _modulation(cond, w_ada, b_ada)

    cos, sin = _rope_tables(dims)
    cache_kr = jnp.pad(cache_mla_krope, ((0, 0), (0, 0), (0, 0), (0, LANES - MLA_ROPE)))
    kc, vc = _cache_kv(cache_mla_ckv, cache_kr, wkv, r3(g_k_nope))
    gla_tables = _gla_tables(dims)

    x = jnp.concatenate([x_prompt.reshape(t_ctx, D), x_sample.reshape(dims["t_lat"], D)], axis=0)
    ckv_l, kr_l, st_l = [], [], []
    for l in range(DEPTH):
        p = _proj_in(x, mods, g1, wp, l, dims)
        s0 = jnp.concatenate([jnp.zeros((1, 2, GLA_HEADS, GLA_DV, GLA_DK), F32),
                              jnp.swapaxes(state_gla[:, l], -1, -2)], axis=0)
        o_f, o_b, s_fin = _gla(p, wdec[l], bdec[l], s0, gla_tables, dims)
        pool_out = _pool(p, w_pool_b, pool_scale3, l, dims)
        qr, qu, k, v, ckv, kro = _mla_prep(p, cos, sin, wq, wkv, gains, l, dims)
        attn_ctx = _attn_ctx(qr, k, v, dims)
        attn_lat = _attn_lat(qr, qu, k, v, kc, vc, l, dims)
        merged = _branch_merge(o_f, o_b, p, pool_out, attn_ctx, attn_lat, w, l, dims)
        x_mid, route = _merge(x, mods, merged, w, l, dims)
        x = _moe(x_mid, route, mods, w, l, dims)
        ckv_l.append(ckv[:t_ctx].reshape(b_ctx, l_ctx, MLA_KV_LORA))
        kr_l.append(kro[:t_ctx, :MLA_ROPE].reshape(b_ctx, l_ctx, MLA_ROPE))
        st_l.append(s_fin[:b_ctx])

    y_prompt = x[:t_ctx].reshape(b_ctx, l_ctx, D)
    y_sample = x[t_ctx:dims["t"]].reshape(b_lat, l_lat, D)
    return (y_prompt, y_sample, jnp.stack(ckv_l, axis=1), jnp.stack(kr_l, axis=1),
            jnp.swapaxes(jnp.stack(st_l, axis=1), -1, -2))
```

```python
import functools
import math

import numpy as np
import jax
import jax.numpy as jnp
from jax import lax
from jax.experimental import pallas as pl
from jax.experimental.pallas import tpu as pltpu

F32 = jnp.float32
BF16 = jnp.bfloat16
I32 = jnp.int32

D = 2048
DEPTH = 4
EPS = 1e-6
GRID_W = 64
GLA_HEADS, GLA_DK, GLA_DV, GLA_RANK, GLA_TAU, GLA_CHUNK = 4, 128, 256, 16, 16.0, 64
POOL_WINDOWS = (2, 4, 8, 16)
POOL_GROUP_DIM = 256
MLA_HEADS, MLA_Q_LORA, MLA_KV_LORA, MLA_NOPE, MLA_ROPE, MLA_V = 8, 768, 512, 128, 64, 128
MLA_SCALE = 1.0 / math.sqrt(MLA_NOPE + MLA_ROPE)
ROPE_THETA = 10000.0
N_GROUPS, EXPERTS_PER_GROUP, EXPERT_FF = 4, 4, 512
SPLIT_SIZES = (512, 512, 1024, 1024, 32, 1024, 768, 512, 64, 6144)

LANES = 128
SLAB = 256
VMEM_LIMIT_BYTES = 60000 * 1024

C_GL = 0
C_Q = 6144
C_K = 6656
C_V = 7168
C_GR = 8192
C_PIN = 9216
C_KVA = 10240
C_QA = 10752
C_GLOW = 11520
C_KR = 11648
NP = 11776


def _dot(a, b):
    return jnp.dot(a, b, preferred_element_type=F32)


def _dot_nt(a, b):
    return lax.dot_general(a, b, (((1,), (1,)), ((), ())), preferred_element_type=F32)


def _dot_tn(a, b):
    return lax.dot_general(a, b, (((0,), (0,)), ((), ())), preferred_element_type=F32)


def _split3(x):
    hi = x.astype(BF16)
    r1 = x - hi.astype(F32)
    mid = r1.astype(BF16)
    lo = (r1 - mid.astype(F32)).astype(BF16)
    return hi, mid, lo


def _sigmoid(x):
    return 0.5 * jnp.tanh(0.5 * x) + 0.5


def _silu(x):
    return x * _sigmoid(x)


def _rms(x, n):
    ms = jnp.sum(x * x, axis=-1, keepdims=True) * (1.0 / n)
    return x * lax.rsqrt(ms + EPS)


def _cparams(sem, vmem=VMEM_LIMIT_BYTES):
    return pltpu.CompilerParams(dimension_semantics=sem, vmem_limit_bytes=vmem)


def _pick_tile(cands, *extents):
    for c in cands:
        if all(e % c == 0 for e in extents):
            return c
    raise ValueError(f"no tile in {cands} divides {extents}")


def _mod_kernel(c_ref, w_ref, b_ref, o_ref):
    a = _silu(c_ref[...]).astype(BF16)
    o_ref[0] = _dot(a, w_ref[0].astype(BF16)) + b_ref[0]


def _modulation(cond, w_ada, b_ada):
    tn = 1024
    out = pl.pallas_call(
        _mod_kernel,
        grid=(DEPTH, 6 * D // tn),
        in_specs=[
            pl.BlockSpec((8, D), lambda l, j: (0, 0)),
            pl.BlockSpec((1, D, tn), lambda l, j: (l, 0, j)),
            pl.BlockSpec((1, 1, tn), lambda l, j: (l, 0, j)),
        ],
        out_specs=pl.BlockSpec((1, 8, tn), lambda l, j: (l, 0, j)),
        out_shape=jax.ShapeDtypeStruct((DEPTH, 8, 6 * D), F32),
        compiler_params=_cparams(("parallel", "parallel"), 40 * 1024 * 1024),
        name="adaln_modulation",
    )(cond, w_ada, b_ada.reshape(DEPTH, 1, 6 * D))
    return out.reshape(DEPTH, 8, 6, D)


def _proj_in_kernel(x_ref, mod_ref, g_ref, w_ref, o_ref, h_scr, *, tm):
    @pl.when(pl.program_id(1) == 0)
    def _():
        m = mod_ref[0, 0]
        g = g_ref[0]

        def body(r, carry):
            rows = pl.ds(pl.multiple_of(r * LANES, LANES), LANES)
            y = _rms(x_ref[rows, :], D) * g
            h_scr[rows, :] = (y * (1.0 + m[1:2]) + m[0:1]).astype(BF16)
            return carry

        lax.fori_loop(0, tm // LANES, body, 0)

    o_ref[...] = _dot_nt(h_scr[...], w_ref[0]).astype(BF16)


def _proj_in(x, mods, g1, wp, l, dims):
    T = dims["t"]
    tm = _pick_tile((1024, 512, 256), dims["t_ctx"], dims["l_lat"])
    tn = 512
    gmap = _group_map(dims, tm)
    return pl.pallas_call(
        functools.partial(_proj_in_kernel, tm=tm),
        grid=(T // tm, NP // tn),
        in_specs=[
            pl.BlockSpec((tm, D), lambda i, j: (i, 0)),
            pl.BlockSpec((1, 1, 6, D), lambda i, j: (l, gmap(i), 0, 0)),
            pl.BlockSpec((1, 1, D), lambda i, j: (l, 0, 0)),
            pl.BlockSpec((1, tn, D), lambda i, j: (l, j, 0)),
        ],
        out_specs=pl.BlockSpec((tm, tn), lambda i, j: (i, j)),
        out_shape=jax.ShapeDtypeStruct((T, NP), BF16),
        scratch_shapes=[pltpu.VMEM((tm, D), BF16)],
        compiler_params=_cparams(("parallel", "arbitrary")),
        name="norm1_proj_in",
    )(x, mods, g1, wp)


def _group_map(dims, tm):
    n_ctx_tiles = dims["t_ctx"] // tm
    per_lat = dims["l_lat"] // tm

    def gmap(i):
        return jnp.where(i < n_ctx_tiles, 0, 1 + (i - n_ctx_tiles) // per_lat)

    return gmap


def _gla_kernel(fblk, bblk, first, last, sidx, unit,
                qf, kf, vf, gf, qb, kb, vb, gb, wdec, bdec, s0,
                of, ob, sout, s_scr):
    del fblk, bblk, sidx, unit
    s = pl.program_id(0)

    @pl.when(first[s] == 1)
    def _():
        s_scr[...] = s0[0]

    n_chunks = SLAB // GLA_CHUNK
    r = lax.broadcasted_iota(I32, (SLAB, SLAB), 0)
    c = lax.broadcasted_iota(I32, (SLAB, SLAB), 1)
    same = (r // GLA_CHUNK) == (c // GLA_CHUNK)

    def rows(x, ci):
        return x[ci * GLA_CHUNK:(ci + 1) * GLA_CHUNK]

    dirs = ((qf, kf, vf, gf, of), (qb, kb, vb, gb, ob))
    for d, (q_ref, k_ref, v_ref, g_ref, o_ref) in enumerate(dirs):
        tri = jnp.logical_and(same, (c <= r) if d == 0 else (c >= r))
        tri_b = tri.astype(BF16)
        dec = _dot(g_ref[...], wdec[d]) + bdec[d]
        la = (jnp.minimum(dec, 0.0) - jnp.log1p(jnp.exp(-jnp.abs(dec)))) * (1.0 / GLA_TAU)
        hi, mid, lo = _split3(la)
        b_all = _dot(tri_b, hi) + _dot(tri_b, mid) + _dot(tri_b, lo)
        edge = GLA_CHUNK - 1 if d == 0 else 0
        tot = [b_all[ci * GLA_CHUNK + edge:ci * GLA_CHUNK + edge + 1] for ci in range(n_chunks)]
        bl_all = jnp.concatenate([jnp.broadcast_to(t, (GLA_CHUNK, t.shape[1])) for t in tot], axis=0)
        order = range(n_chunks) if d == 0 else range(n_chunks - 1, -1, -1)
        for h in range(GLA_HEADS):
            kc = slice(h * GLA_DK, (h + 1) * GLA_DK)
            vc = slice(h * GLA_DV, (h + 1) * GLA_DV)
            b = b_all[:, kc]
            bl = bl_all[:, kc]
            q = q_ref[:, kc].astype(F32) * (GLA_DK ** -0.5)
            k = k_ref[:, kc].astype(F32)
            v = v_ref[:, vc]
            qt = (q * jnp.exp(b)).astype(BF16)
            kt = (k * jnp.exp(-b)).astype(BF16)
            ke = (k * jnp.exp(bl - b)).astype(BF16)
            a = jnp.where(tri, _dot_nt(qt, kt), 0.0).astype(BF16)
            o_intra = _dot(a, v)
            ds_t = [_dot_tn(rows(v, ci), rows(ke, ci)) for ci in range(n_chunks)]
            decay = [jnp.exp(tot[ci][:, kc]) for ci in range(n_chunks)]
            sd = s_scr[d, h]
            o_inter = [None] * n_chunks
            for ci in order:
                o_inter[ci] = _dot_nt(rows(qt, ci), sd.astype(BF16))
                sd = sd * decay[ci] + ds_t[ci]
            o_ref[:, vc] = (o_intra + jnp.concatenate(o_inter, axis=0)).astype(BF16)
            s_scr[d, h] = sd

            @pl.when(last[s] == 1)
            def _(sd=sd, d=d, h=h):
                sout[0, d, h] = sd


def _gla_tables(dims):
    ctx_slabs = dims["l_ctx"] // SLAB
    lat_slabs = dims["l_lat"] // SLAB
    fblk, bblk, first, last, sidx, unit = [], [], [], [], [], []
    base = 0
    for u in range(dims["b_ctx"] + dims["b_lat"]):
        is_ctx = u < dims["b_ctx"]
        n = ctx_slabs if is_ctx else lat_slabs
        for j in range(n):
            fblk.append(base + j)
            bblk.append(base + n - 1 - j)
            first.append(int(j == 0))
            last.append(int(j == n - 1))
            sidx.append(0 if is_ctx else 1 + u - dims["b_ctx"])
            unit.append(u)
        base += n
    return [jnp.asarray(np.asarray(t, np.int32)) for t in (fblk, bblk, first, last, sidx, unit)]


def _gla(p, wdec, bdec, s0, tables, dims):
    T = p.shape[0]
    n_steps = T // SLAB
    n_units = dims["b_ctx"] + dims["b_lat"]

    hk, hv = GLA_HEADS * GLA_DK, GLA_HEADS * GLA_DV
    state = (2, GLA_HEADS, GLA_DV, GLA_DK)

    def pspec(width, col0, which):
        cb = col0 // width
        if which == 0:
            return pl.BlockSpec((SLAB, width), lambda s, fb, bb, fi, la, si, un: (fb[s], cb))
        return pl.BlockSpec((SLAB, width), lambda s, fb, bb, fi, la, si, un: (bb[s], cb))

    in_specs = []
    for which in (0, 1):
        in_specs += [pspec(hk, C_Q, which), pspec(hk, C_K, which), pspec(hv, C_V, which),
                     pspec(LANES, C_GLOW, which)]
    in_specs += [
        pl.BlockSpec((2, LANES, hk), lambda s, *_: (0, 0, 0)),
        pl.BlockSpec((2, 1, hk), lambda s, *_: (0, 0, 0)),
        pl.BlockSpec((1,) + state, lambda s, fb, bb, fi, la, si, un: (si[s], 0, 0, 0, 0)),
    ]
    out_specs = [
        pl.BlockSpec((SLAB, hv), lambda s, fb, bb, fi, la, si, un: (fb[s], 0)),
        pl.BlockSpec((SLAB, hv), lambda s, fb, bb, fi, la, si, un: (bb[s], 0)),
        pl.BlockSpec((1,) + state, lambda s, fb, bb, fi, la, si, un: (un[s], 0, 0, 0, 0)),
    ]
    grid_spec = pltpu.PrefetchScalarGridSpec(
        num_scalar_prefetch=6, grid=(n_steps,),
        in_specs=in_specs, out_specs=out_specs,
        scratch_shapes=[pltpu.VMEM(state, F32)])
    return pl.pallas_call(
        _gla_kernel,
        grid_spec=grid_spec,
        out_shape=[jax.ShapeDtypeStruct((T, hv), BF16),
                   jax.ShapeDtypeStruct((T, hv), BF16),
                   jax.ShapeDtypeStruct((n_units,) + state, F32)],
        compiler_params=_cparams(("arbitrary",), 32 * 1024 * 1024),
        name="gla_bidirectional",
    )(*tables, p, p, p, p, p, p, p, p, wdec, bdec, s0)


def _pool_kernel(cur, prv, nxt, wp, sc, o_ref, *, n_ctx_slabs, ctx_slabs, lat_slabs):
    i = pl.program_id(0)
    is_ctx = i < n_ctx_slabs
    seq_slabs = jnp.where(is_ctx, ctx_slabs, lat_slabs)
    j = jnp.where(is_ctx, i % ctx_slabs, (i - n_ctx_slabs) % lat_slabs)
    has_prev = j > 0
    has_next = j < seq_slabs - 1
    seq_len = seq_slabs * SLAB
    r = lax.broadcasted_iota(I32, (SLAB, SLAB), 0)
    c = lax.broadcasted_iota(I32, (SLAB, SLAB), 1)
    t = j * SLAB + lax.broadcasted_iota(I32, (SLAB, 1), 0)
    gd = POOL_GROUP_DIM
    for g, w in enumerate(POOL_WINDOWS):
        lo_off, hi_off = w // 2, w - w // 2
        cols = slice(g * gd, (g + 1) * gd)
        u = cur[:, cols]
        b_cur = jnp.logical_and(c >= r - lo_off, c < r + hi_off)
        b_prv = jnp.logical_and(c - SLAB >= r - lo_off, has_prev)
        b_nxt = jnp.logical_and(c + SLAB < r + hi_off, has_next)
        ssum = (_dot(b_cur.astype(BF16), u) + _dot(b_prv.astype(BF16), prv[:, cols])
                + _dot(b_nxt.astype(BF16), nxt[:, cols]))
        cnt = (jnp.minimum(t + hi_off, seq_len) - jnp.maximum(t - lo_off, 0)).astype(F32)
        pooled = ssum / cnt - u.astype(F32)
        o_ref[:, cols] = (_dot(pooled.astype(BF16), wp[0, g]) * sc[0, :, cols]).astype(BF16)


def _pool(p, w_pool, pool_scale, l, dims):
    T = p.shape[0]
    n = T // SLAB
    cb = C_PIN // 1024
    kern = functools.partial(_pool_kernel, n_ctx_slabs=dims["t_ctx"] // SLAB,
                             ctx_slabs=dims["l_ctx"] // SLAB, lat_slabs=dims["l_lat"] // SLAB)
    return pl.pallas_call(
        kern,
        grid=(n,),
        in_specs=[
            pl.BlockSpec((SLAB, 1024), lambda i: (i, cb)),
            pl.BlockSpec((SLAB, 1024), lambda i: (jnp.maximum(i - 1, 0), cb)),
            pl.BlockSpec((SLAB, 1024), lambda i: (jnp.minimum(i + 1, n - 1), cb)),
            pl.BlockSpec((1, 4, 256, 256), lambda i: (l, 0, 0, 0)),
            pl.BlockSpec((1, 1, 1024), lambda i: (l, 0, 0)),
        ],
        out_specs=pl.BlockSpec((SLAB, 1024), lambda i: (i, 0)),
        out_shape=jax.ShapeDtypeStruct((T, 1024), BF16),
        compiler_params=_cparams(("parallel",), 32 * 1024 * 1024),
        name="pool_mixer",
    )(p, p, p, w_pool, pool_scale)


def _swap16(x):
    lane = lax.broadcasted_iota(I32, x.shape, x.ndim - 1)
    n = x.shape[-1]
    fwd = pltpu.roll(x, n - 16, x.ndim - 1)
    bwd = pltpu.roll(x, 16, x.ndim - 1)
    return jnp.where((lane % 32) < 16, fwd, bwd)


def _mla_prep_kernel(kva_ref, qa_ref, kr_ref, cos_ref, sin_ref, wq, wkv,
                     g_qa, g_kva, g_qn, g_qr, g_kn, g_kr,
                     qr_ref, qu_ref, k_ref, v_ref, ckv_ref, kro_ref):
    ckv = _rms(kva_ref[...].astype(F32), MLA_KV_LORA) * g_kva[0]
    ckv_ref[...] = ckv
    kv = _dot(ckv.astype(BF16), wkv[0])
    qn = _rms(qa_ref[...].astype(F32), MLA_Q_LORA) * g_qa[0]
    q = _dot(qn.astype(BF16), wq[0])
    cos = cos_ref[...]
    sin = sin_ref[...]
    kr = _rms(kr_ref[...].astype(F32), MLA_ROPE) * g_kr[0]
    kro_ref[...] = kr
    kr_rot = (kr * cos + _swap16(kr) * sin).astype(BF16)
    for h in range(MLA_HEADS):
        c0 = 2 * LANES * h
        q_nope = _rms(q[:, c0:c0 + LANES], MLA_NOPE) * g_qn[0] * MLA_SCALE
        q_rope = _rms(q[:, c0 + LANES:c0 + 2 * LANES], MLA_ROPE) * g_qr[0]
        q_rot = q_rope * cos + _swap16(q_rope) * sin
        qr_ref[:, c0:c0 + LANES] = q_nope.astype(BF16)
        qu_ref[:, c0:c0 + LANES] = q_nope.astype(BF16)
        qr_ref[:, c0 + LANES:c0 + 2 * LANES] = (q_rot * MLA_SCALE).astype(BF16)
        qu_ref[:, c0 + LANES:c0 + 2 * LANES] = (q_rope * MLA_SCALE).astype(BF16)
        k_nope = _rms(kv[:, c0:c0 + LANES], MLA_NOPE) * g_kn[0]
        k_ref[:, c0:c0 + LANES] = k_nope.astype(BF16)
        k_ref[:, c0 + LANES:c0 + 2 * LANES] = kr_rot
        v_ref[:, LANES * h:LANES * (h + 1)] = kv[:, c0 + LANES:c0 + 2 * LANES].astype(BF16)


def _mla_prep(p, cos, sin, wq, wkv, gains, l, dims):
    T = p.shape[0]
    tm = _pick_tile((512, 256), dims["t_ctx"], dims["l_lat"])
    hw = MLA_HEADS * 2 * LANES

    def gspec(n):
        return pl.BlockSpec((1, 1, n), lambda i: (l, 0, 0))

    return pl.pallas_call(
        _mla_prep_kernel,
        grid=(T // tm,),
        in_specs=[
            pl.BlockSpec((tm, MLA_KV_LORA), lambda i: (i, C_KVA // MLA_KV_LORA)),
            pl.BlockSpec((tm, MLA_Q_LORA), lambda i: (i, C_QA // MLA_Q_LORA)),
            pl.BlockSpec((tm, LANES), lambda i: (i, C_KR // LANES)),
            pl.BlockSpec((tm, LANES), lambda i: (i, 0)),
            pl.BlockSpec((tm, LANES), lambda i: (i, 0)),
            pl.BlockSpec((1, MLA_Q_LORA, hw), lambda i: (l, 0, 0)),
            pl.BlockSpec((1, MLA_KV_LORA, hw), lambda i: (l, 0, 0)),
            gspec(MLA_Q_LORA), gspec(MLA_KV_LORA), gspec(LANES), gspec(LANES), gspec(LANES), gspec(LANES),
        ],
        out_specs=[
            pl.BlockSpec((tm, hw), lambda i: (i, 0)),
            pl.BlockSpec((tm, hw), lambda i: (i, 0)),
            pl.BlockSpec((tm, hw), lambda i: (i, 0)),
            pl.BlockSpec((tm, MLA_HEADS * MLA_V), lambda i: (i, 0)),
            pl.BlockSpec((tm, MLA_KV_LORA), lambda i: (i, 0)),
            pl.BlockSpec((tm, LANES), lambda i: (i, 0)),
        ],
        out_shape=[
            jax.ShapeDtypeStruct((T, hw), BF16),
            jax.ShapeDtypeStruct((T, hw), BF16),
            jax.ShapeDtypeStruct((T, hw), BF16),
            jax.ShapeDtypeStruct((T, MLA_HEADS * MLA_V), BF16),
            jax.ShapeDtypeStruct((T, MLA_KV_LORA), F32),
            jax.ShapeDtypeStruct((T, LANES), F32),
        ],
        compiler_params=_cparams(("parallel",), 48 * 1024 * 1024),
        name="mla_prep",
    )(p, p, p, cos, sin, wq, wkv, *gains)


def _cache_kv_kernel(ckv_ref, kr_ref, wkv, g_kn, k_ref, v_ref):
    kv = _dot(ckv_ref[0, 0].astype(BF16), wkv[0])
    kr = kr_ref[0, 0].astype(BF16)
    for h in range(MLA_HEADS):
        c0 = 2 * LANES * h
        k_nope = _rms(kv[:, c0:c0 + LANES], MLA_NOPE) * g_kn[0]
        k_ref[0, 0, :, c0:c0 + LANES] = k_nope.astype(BF16)
        k_ref[0, 0, :, c0 + LANES:c0 + 2 * LANES] = kr
        v_ref[0, 0, :, LANES * h:LANES * (h + 1)] = kv[:, c0 + LANES:c0 + 2 * LANES].astype(BF16)


def _cache_kv(cache_ckv, cache_kr, wkv, g_kn):
    b_lat, _, past, _ = cache_ckv.shape
    hw = MLA_HEADS * 2 * LANES
    return pl.pallas_call(
        _cache_kv_kernel,
        grid=(DEPTH, b_lat),
        in_specs=[
            pl.BlockSpec((1, 1, past, MLA_KV_LORA), lambda l, b: (b, l, 0, 0)),
            pl.BlockSpec((1, 1, past, LANES), lambda l, b: (b, l, 0, 0)),
            pl.BlockSpec((1, MLA_KV_LORA, hw), lambda l, b: (l, 0, 0)),
            pl.BlockSpec((1, 1, LANES), lambda l, b: (l, 0, 0)),
        ],
        out_specs=[
            pl.BlockSpec((1, 1, past, hw), lambda l, b: (l, b, 0, 0)),
            pl.BlockSpec((1, 1, past, MLA_HEADS * MLA_V), lambda l, b: (l, b, 0, 0)),
        ],
        out_shape=[
            jax.ShapeDtypeStruct((DEPTH, b_lat, past, hw), BF16),
            jax.ShapeDtypeStruct((DEPTH, b_lat, past, MLA_HEADS * MLA_V), BF16),
        ],
        compiler_params=_cparams(("parallel", "parallel"), 32 * 1024 * 1024),
        name="mla_cache_decompress",
    )(cache_ckv, cache_kr, wkv, g_kn)


def _attn_ctx_kernel(q_ref, k_ref, v_ref, o_ref):
    for h in range(MLA_HEADS):
        qk = slice(h * 2 * LANES, (h + 1) * 2 * LANES)
        vs = slice(h * MLA_V, (h + 1) * MLA_V)
        s = _dot_nt(q_ref[:, qk], k_ref[:, qk])
        m = jnp.max(s, axis=-1, keepdims=True)
        pr = jnp.exp(s - m)
        den = jnp.sum(pr, axis=-1, keepdims=True)
        o_ref[:, vs] = (_dot(pr.astype(BF16), v_ref[:, vs]) / den).astype(BF16)


def _attn_ctx(q, k, v, dims):
    lc = dims["l_ctx"]
    t_ctx = dims["t_ctx"]
    hw = MLA_HEADS * 2 * LANES
    return pl.pallas_call(
        _attn_ctx_kernel,
        grid=(dims["b_ctx"],),
        in_specs=[
            pl.BlockSpec((lc, hw), lambda s: (s, 0)),
            pl.BlockSpec((lc, hw), lambda s: (s, 0)),
            pl.BlockSpec((lc, MLA_HEADS * MLA_V), lambda s: (s, 0)),
        ],
        out_specs=pl.BlockSpec((lc, MLA_HEADS * MLA_V), lambda s: (s, 0)),
        out_shape=jax.ShapeDtypeStruct((t_ctx, MLA_HEADS * MLA_V), BF16),
        compiler_params=_cparams(("parallel",), 32 * 1024 * 1024),
        name="mla_attention_context",
    )(q, k, v)


ATTN_HEADS_PER_STEP = 2
ATTN_KEY_PARTS = 2


def _attn_lat_kernel(qr_ref, qu_ref, k_ref, v_ref, kc_ref, vc_ref, o_ref):
    for h in range(ATTN_HEADS_PER_STEP):
        qk = slice(h * 2 * LANES, (h + 1) * 2 * LANES)
        vs = slice(h * MLA_V, (h + 1) * MLA_V)
        n_lat = k_ref.shape[0]
        part = n_lat // ATTN_KEY_PARTS
        qr = qr_ref[:, qk]
        stats = []
        for c in range(ATTN_KEY_PARTS):
            ks = slice(c * part, (c + 1) * part)
            s = _dot_nt(qr, k_ref[ks, qk])
            mc = jnp.max(s, axis=-1, keepdims=True)
            if c == ATTN_KEY_PARTS - 1:
                s2 = _dot_nt(qu_ref[:, qk], kc_ref[0, 0, :, qk])
                mc = jnp.maximum(mc, jnp.max(s2, axis=-1, keepdims=True))
            pr = jnp.exp(s - mc)
            lc = jnp.sum(pr, axis=-1, keepdims=True)
            oc = _dot(pr.astype(BF16), v_ref[ks, vs])
            if c == ATTN_KEY_PARTS - 1:
                p2 = jnp.exp(s2 - mc)
                lc = lc + jnp.sum(p2, axis=-1, keepdims=True)
                oc = oc + _dot(p2.astype(BF16), vc_ref[0, 0, :, vs])
            stats.append((mc, lc, oc))
        m = functools.reduce(jnp.maximum, [st[0] for st in stats])
        num = den = None
        for mc, lc, oc in stats:
            wc = jnp.exp(mc - m)
            num = oc * wc if num is None else num + oc * wc
            den = lc * wc if den is None else den + lc * wc
        o_ref[:, vs] = (num / den).astype(BF16)


def _attn_lat(qr, qu, k, v, kc, vc, l, dims):
    ll = dims["l_lat"]
    past = kc.shape[2]
    tq = _pick_tile((256,), ll)
    q0 = dims["t_ctx"] // tq
    k0 = dims["t_ctx"] // ll
    nq = ll // tq
    hp = ATTN_HEADS_PER_STEP
    qw, vw = hp * 2 * LANES, hp * MLA_V
    return pl.pallas_call(
        _attn_lat_kernel,
        grid=(dims["b_lat"], MLA_HEADS // hp, nq),
        in_specs=[
            pl.BlockSpec((tq, qw), lambda b, h, i: (q0 + b * nq + i, h)),
            pl.BlockSpec((tq, qw), lambda b, h, i: (q0 + b * nq + i, h)),
            pl.BlockSpec((ll, qw), lambda b, h, i: (k0 + b, h)),
            pl.BlockSpec((ll, vw), lambda b, h, i: (k0 + b, h)),
            pl.BlockSpec((1, 1, past, qw), lambda b, h, i: (l, b, 0, h)),
            pl.BlockSpec((1, 1, past, vw), lambda b, h, i: (l, b, 0, h)),
        ],
        out_specs=pl.BlockSpec((tq, vw), lambda b, h, i: (b * nq + i, h)),
        out_shape=jax.ShapeDtypeStruct((dims["b_lat"] * ll, MLA_HEADS * MLA_V), BF16),
        compiler_params=_cparams(("parallel", "parallel", "arbitrary")),
        name="mla_attention_latent",
    )(qr, qu, k, v, kc, vc)


def _route(logits_t):
    ng, ne = N_GROUPS, EXPERTS_PER_GROUP
    lg = [logits_t[g:g + 1] for g in range(ng)]
    mg = functools.reduce(jnp.maximum, lg)
    zg = functools.reduce(lambda a, b: a + b, [jnp.exp(x - mg) for x in lg])
    pg_top = 1.0 / zg
    grp = jnp.full_like(mg, float(ng))
    for g in range(ng - 1, -1, -1):
        grp = jnp.where(lg[g] == mg, float(g), grp)
    el = []
    for j in range(ne):
        acc = jnp.zeros_like(mg)
        for g in range(ng):
            row = ng + g * ne + j
            acc = jnp.where(grp == float(g), logits_t[row:row + 1], acc)
        el.append(acc)
    m1 = functools.reduce(jnp.maximum, el)
    i1 = jnp.full_like(mg, float(ne))
    for j in range(ne - 1, -1, -1):
        i1 = jnp.where(el[j] == m1, float(j), i1)
    neg = jnp.full_like(mg, -jnp.inf)
    rest = [jnp.where(i1 == float(j), neg, el[j]) for j in range(ne)]
    m2 = functools.reduce(jnp.maximum, rest)
    i2 = jnp.full_like(mg, float(ne))
    for j in range(ne - 1, -1, -1):
        i2 = jnp.where(rest[j] == m2, float(j), i2)
    e2 = jnp.exp(m2 - m1)
    w1 = 1.0 / (1.0 + e2)
    w2 = e2 / (1.0 + e2)
    rows = [pg_top * (jnp.where(i1 == float(j), w1, 0.0) + jnp.where(i2 == float(j), w2, 0.0))
            for j in range(ne)]
    rows += [grp, jnp.zeros_like(mg), jnp.zeros_like(mg), jnp.zeros_like(mg)]
    return jnp.concatenate(rows, axis=0)


def _branch_kernel(of_ref, ob_ref, gr_ref, pool_ref, actx_ref, alat_ref, gl0, gl1, gl2,
                   wbg, wbp, wbm, g_gla, o_ref, a_scr, attn_ref, *, n_ctx_tiles):
    gg = g_gla[0]

    @pl.when(pl.program_id(0) < n_ctx_tiles)
    def _():
        attn_ref[...] = actx_ref[...]

    @pl.when(pl.program_id(0) >= n_ctx_tiles)
    def _():
        attn_ref[...] = alat_ref[...]

    for h in range(GLA_HEADS):
        cols = slice(h * GLA_DV, (h + 1) * GLA_DV)
        o = of_ref[:, cols].astype(F32) + ob_ref[:, cols].astype(F32)
        a_scr[:, cols] = (_rms(o, GLA_DV) * gg * _silu(gr_ref[:, cols].astype(F32))).astype(BF16)
    nb = 512
    for n in range(D // nb):
        cols = slice(n * nb, (n + 1) * nb)
        acc = _sigmoid(gl0[:, cols].astype(F32)) * _dot(a_scr[...], wbg[0, :, cols])
        acc += _sigmoid(gl1[:, cols].astype(F32)) * _dot(pool_ref[...], wbp[0, :, cols])
        acc += _sigmoid(gl2[:, cols].astype(F32)) * _dot(attn_ref[...], wbm[0, :, cols])
        o_ref[:, cols] = acc.astype(BF16)


def _branch_merge(o_f, o_b, p, pool_out, attn_ctx, attn_lat, w, l, dims):
    T = dims["t"]
    tm = _pick_tile((512, 256), dims["t_ctx"], dims["l_lat"])
    n_ctx_tiles = dims["t_ctx"] // tm

    def wspec(shape):
        return pl.BlockSpec((1,) + shape, lambda i: (l,) + (0,) * len(shape), pipeline_mode=pl.Buffered(1))

    return pl.pallas_call(
        functools.partial(_branch_kernel, n_ctx_tiles=n_ctx_tiles),
        grid=(T // tm,),
        in_specs=[
            pl.BlockSpec((tm, 1024), lambda i: (i, 0)),
            pl.BlockSpec((tm, 1024), lambda i: (i, 0)),
            pl.BlockSpec((tm, 1024), lambda i: (i, C_GR // 1024)),
            pl.BlockSpec((tm, 1024), lambda i: (i, 0)),
            pl.BlockSpec((tm, 1024), lambda i: (jnp.minimum(i, n_ctx_tiles - 1), 0)),
            pl.BlockSpec((tm, 1024), lambda i: (jnp.maximum(i - n_ctx_tiles, 0), 0)),
            pl.BlockSpec((tm, D), lambda i: (i, 0)),
            pl.BlockSpec((tm, D), lambda i: (i, 1)),
            pl.BlockSpec((tm, D), lambda i: (i, 2)),
            wspec((1024, D)), wspec((1024, D)), wspec((1024, D)),
            pl.BlockSpec((1, 1, GLA_DV), lambda i: (l, 0, 0)),
        ],
        out_specs=pl.BlockSpec((tm, D), lambda i: (i, 0)),
        out_shape=jax.ShapeDtypeStruct((T, D), BF16),
        scratch_shapes=[pltpu.VMEM((tm, 1024), BF16), pltpu.VMEM((tm, 1024), BF16)],
        compiler_params=_cparams(("parallel",)),
        name="branch_merge",
    )(o_f, o_b, p, pool_out, attn_ctx, attn_lat, p, p, p,
      w["w_br_gla"], w["w_br_pool"], w["w_br_mla"], w["g_gla"])


def _merge_kernel(x_ref, mod_ref, mg_ref, wo, g2_ref, wr, br, xo_ref, r_ref, *, tm):
    m = mod_ref[0, 0]
    xm = x_ref[...] + m[2:3] * _dot(mg_ref[...], wo[0])
    xo_ref[...] = xm
    g2 = g2_ref[0]
    for ci in range(tm // LANES):
        rows = slice(ci * LANES, (ci + 1) * LANES)
        h2 = _rms(xm[rows], D) * g2 * (1.0 + m[4:5]) + m[3:4]
        h_hi = h2.astype(BF16)
        hs = (h_hi, (h2 - h_hi.astype(F32)).astype(BF16))
        lt = br[0]
        for ia, ib in ((0, 0), (0, 1), (1, 0)):
            lt = lt + _dot_nt(wr[0, ia], hs[ib])
        r_ref[:, rows] = _route(lt)


def _merge(x, mods, merged, w, l, dims):
    T = dims["t"]
    tm = _pick_tile((512, 256), dims["t_ctx"], dims["l_lat"])
    gmap = _group_map(dims, tm)
    return pl.pallas_call(
        functools.partial(_merge_kernel, tm=tm),
        grid=(T // tm,),
        in_specs=[
            pl.BlockSpec((tm, D), lambda i: (i, 0)),
            pl.BlockSpec((1, 1, 6, D), lambda i: (l, gmap(i), 0, 0)),
            pl.BlockSpec((tm, D), lambda i: (i, 0)),
            pl.BlockSpec((1, D, D), lambda i: (l, 0, 0), pipeline_mode=pl.Buffered(1)),
            pl.BlockSpec((1, 1, D), lambda i: (l, 0, 0)),
            pl.BlockSpec((1, 2, 32, D), lambda i: (l, 0, 0, 0)),
            pl.BlockSpec((1, 32, LANES), lambda i: (l, 0, 0)),
        ],
        out_specs=[
            pl.BlockSpec((tm, D), lambda i: (i, 0)),
            pl.BlockSpec((8, tm), lambda i: (0, i)),
        ],
        out_shape=[jax.ShapeDtypeStruct((T, D), F32), jax.ShapeDtypeStruct((8, T), F32)],
        compiler_params=_cparams(("parallel",)),
        name="merge_out_route",
    )(x, mods, merged, w["w_o"], w["g_norm2"], w["router_w"], w["router_b"])


def _moe_kernel(tg, nvalid, gi_ref, gn_ref, si_ref, x_hbm, cm_ref, mod_ref, g2_ref, wg, wu, wd,
                o_hbm, xbuf, obuf, h_scr, gsem, ssem, *, tm, nt, n_cond):
    del tg
    t = pl.program_id(0)
    slot = t % 2

    def used(i):
        return nvalid[jnp.clip(i, 0, nt - 1)] > 0

    def gather_copy(sl, r, row):
        return pltpu.make_async_copy(x_hbm.at[pl.ds(row, 1)], xbuf.at[sl, pl.ds(r, 1)], gsem.at[sl])

    def scatter_copy(sl, r, row):
        return pltpu.make_async_copy(obuf.at[sl, pl.ds(r, 1)], o_hbm.at[pl.ds(row, 1)], ssem.at[sl])

    def start_rows(make, idx_ref):
        for r in range(tm):
            make(r, idx_ref[0, 0, r]).start(priority=r % 2)

    def wait_gather(sl):
        pltpu.make_async_copy(x_hbm.at[pl.ds(0, tm)], xbuf.at[sl], gsem.at[sl]).wait()

    def wait_scatter(sl):
        pltpu.make_async_copy(obuf.at[sl], o_hbm.at[pl.ds(0, tm)], ssem.at[sl]).wait()

    @pl.when(t == 0)
    def _():
        obuf[1] = jnp.zeros((tm, D), F32)
        n_tok = o_hbm.shape[0] - N_GROUPS * tm
        for g in range(N_GROUPS):
            pltpu.make_async_copy(obuf.at[1], o_hbm.at[pl.ds(n_tok + g * tm, tm)], ssem.at[1]).start()
        for g in range(N_GROUPS):
            wait_scatter(1)

    @pl.when(jnp.logical_and(t == 0, used(0)))
    def _():
        start_rows(functools.partial(gather_copy, 0), gi_ref)

    has_next = jnp.logical_and(t + 1 < nt, used(t + 1))

    @pl.when(jnp.logical_and(t >= 2, used(t - 2)))
    def _():
        wait_scatter(slot)

    @pl.when(used(t))
    def _():
        wait_gather(slot)
        mods = mod_ref[0]
        g2 = g2_ref[0]

        def cond_rows(cg, k):
            out = mods[0, k:k + 1]
            for c in range(1, n_cond):
                out = jnp.where(cg == float(c), mods[c, k:k + 1], out)
            return out

        for r in range(tm // LANES):
            rows = slice(r * LANES, (r + 1) * LANES)
            cg = cm_ref[rows, 4:5]
            h2 = _rms(xbuf[slot, rows, :], D) * g2 * (1.0 + cond_rows(cg, 4)) + cond_rows(cg, 3)
            h_scr[rows, :] = h2.astype(BF16)

        hb = h_scr[...]
        y = None
        per = tm // EXPERTS_PER_GROUP
        for e in range(EXPERTS_PER_GROUP):
            for r in range(e * per, (e + 1) * per):
                gather_copy(1 - slot, r, gn_ref[0, 0, r]).start(priority=r % 2)
            hid = _silu(_dot(hb, wg[0, e])) * _dot(hb, wu[0, e]) * cm_ref[:, e:e + 1]
            part = _dot(hid.astype(BF16), wd[0, e])
            y = part if y is None else y + part
        for r in range(tm // LANES):
            rows = slice(r * LANES, (r + 1) * LANES)
            cg = cm_ref[rows, 4:5]
            obuf[slot, rows, :] = xbuf[slot, rows, :] + cond_rows(cg, 5) * y[rows]
            for rr in range(r * LANES, (r + 1) * LANES):
                scatter_copy(slot, rr, si_ref[0, 0, rr]).start(priority=rr % 2)

    @pl.when(jnp.logical_and(used(t), jnp.logical_not(has_next)))
    def _():
        wait_gather(1 - slot)

    @pl.when(jnp.logical_and(t == nt - 1, used(t - 1)))
    def _():
        wait_scatter(1 - slot)

    @pl.when(jnp.logical_and(t == nt - 1, used(t)))
    def _():
        wait_scatter(slot)


def _moe_plan(route, dims, tm):
    T = route.shape[1]
    nt = T // tm + N_GROUPS
    grp = route[4].astype(I32)
    _, order, *comb_sorted = lax.sort(
        (grp, jnp.arange(T, dtype=I32), route[0], route[1], route[2], route[3]), num_keys=1, is_stable=True)
    gids = jnp.arange(N_GROUPS, dtype=I32)
    counts = jnp.sum(grp[None, :] == gids[:, None], axis=1).astype(I32)
    tiles = (counts + tm - 1) // tm
    tile_end = jnp.cumsum(tiles)
    tile_start = tile_end - tiles
    tok_start = jnp.cumsum(counts) - counts
    tidx = jnp.arange(nt, dtype=I32)
    tg = jnp.minimum(jnp.sum(tidx[:, None] >= tile_end[None, :], axis=1), N_GROUPS - 1).astype(I32)
    onehot = (tg[:, None] == gids[None, :]).astype(I32)
    used = tidx < tile_end[-1]
    in_group = (tidx - jnp.sum(onehot * tile_start[None, :], axis=1)) * tm
    nvalid = jnp.where(used, jnp.clip(jnp.sum(onehot * counts[None, :], axis=1) - in_group, 0, tm), 0)
    nvalid = nvalid.astype(I32)
    slot = jnp.arange(tm, dtype=I32)[None, :]
    valid = slot < nvalid[:, None]
    shift = tile_start * tm - tok_start
    pad = jnp.zeros((nt * tm - T,), I32)
    ints = jnp.concatenate([order, pad])
    flts = jnp.concatenate([jnp.stack(comb_sorted, axis=0), jnp.zeros((4, nt * tm - T), F32)], axis=1)
    slot_g = jnp.broadcast_to(tg[:, None], (nt, tm)).reshape(nt * tm)
    src = jnp.zeros((nt * tm,), I32)
    comb = jnp.zeros((4, nt * tm), F32)
    for g in range(N_GROUPS):
        src = jnp.where(slot_g == g, jnp.roll(ints, shift[g]), src)
        comb = jnp.where((slot_g == g)[None, :], jnp.roll(flts, shift[g], axis=1), comb)
    src = jnp.where(valid, src.reshape(nt, tm), 0)
    comb = jnp.where(valid[None], comb.reshape(4, nt, tm), 0.0)
    dst = jnp.where(valid, src, T + tg[:, None] * tm + slot).astype(I32)
    cond = jnp.where(src < dims["t_ctx"], 0, 1 + (src - dims["t_ctx"]) // dims["l_lat"]).astype(F32)
    cm = jnp.concatenate([comb, cond[None], jnp.zeros((3,) + cond.shape, F32)], axis=0)
    cm = cm.transpose(1, 2, 0).reshape(nt * tm, 8)
    return tg, nvalid, src.reshape(nt, 1, tm), dst.reshape(nt, 1, tm), cm


def _moe_tile(dims):
    return 512 if dims["t"] % 512 == 0 else 256


def _moe(x, route, mods, w, l, dims):
    T = dims["t"]
    tm = _moe_tile(dims)
    nt = T // tm + N_GROUPS
    tg, nvalid, src, dst, cm = _moe_plan(route, dims, tm)
    ff = EXPERT_FF

    def wspec(shape):
        return pl.BlockSpec((1, EXPERTS_PER_GROUP) + shape, lambda t, tg_, nv: (l, tg_[t], 0, 0),
                            pipeline_mode=pl.Buffered(1))

    grid_spec = pltpu.PrefetchScalarGridSpec(
        num_scalar_prefetch=2, grid=(nt,),
        in_specs=[
            pl.BlockSpec((1, 1, tm), lambda t, tg_, nv: (t, 0, 0), memory_space=pltpu.SMEM),
            pl.BlockSpec((1, 1, tm), lambda t, tg_, nv: (jnp.minimum(t + 1, nt - 1), 0, 0),
                         memory_space=pltpu.SMEM),
            pl.BlockSpec((1, 1, tm), lambda t, tg_, nv: (t, 0, 0), memory_space=pltpu.SMEM),
            pl.BlockSpec(memory_space=pl.ANY),
            pl.BlockSpec((tm, 8), lambda t, tg_, nv: (t, 0)),
            pl.BlockSpec((1, 8, 6, D), lambda t, tg_, nv: (l, 0, 0, 0)),
            pl.BlockSpec((1, 1, D), lambda t, tg_, nv: (l, 0, 0)),
            wspec((D, ff)), wspec((D, ff)), wspec((ff, D)),
        ],
        out_specs=pl.BlockSpec(memory_space=pl.ANY),
        scratch_shapes=[pltpu.VMEM((2, tm, D), F32), pltpu.VMEM((2, tm, D), F32), pltpu.VMEM((tm, D), BF16),
                        pltpu.SemaphoreType.DMA((2,)), pltpu.SemaphoreType.DMA((2,))])
    return pl.pallas_call(
        functools.partial(_moe_kernel, tm=tm, nt=nt, n_cond=1 + dims["b_lat"]),
        grid_spec=grid_spec,
        out_shape=jax.ShapeDtypeStruct((T + N_GROUPS * tm, D), F32),
        compiler_params=_cparams(("arbitrary",)),
        name="moe_group_experts",
    )(tg, nvalid, src, src, dst, x, cm, mods, w["g_norm2"], w["w_exp_gate"], w["w_exp_up"], w["w_exp_down"])


def _pack_params(w_in, w_gla_dec, b_gla_dec, w_mla_q_up, g_q_rope, g_k_rope,
                 w_group_router, b_group_router, w_expert_router, b_expert_router):
    wt = jnp.swapaxes(w_in, 1, 2)
    gq, gk, gv, gr, glow, pin, qa, kva, kr, gl = jnp.split(
        wt, [int(v) for v in np.cumsum(SPLIT_SIZES)[:-1]], axis=1)

    def padr(a, n):
        return jnp.pad(a, ((0, 0), (0, n - a.shape[1]), (0, 0)))

    pieces = [gl, gq, gk, gv, gr, pin, kva, qa, padr(glow, LANES), padr(kr, LANES)]
    wp = jnp.concatenate([a.astype(BF16) for a in pieces], axis=1)
    wdec = jnp.zeros((DEPTH, 2, LANES, GLA_HEADS * GLA_DK), F32)
    for d in range(2):
        wdec = wdec.at[:, d, d * GLA_RANK:(d + 1) * GLA_RANK, :].set(w_gla_dec[:, d])
    wdec = wdec.astype(BF16)
    bdec = b_gla_dec.reshape(DEPTH, 2, 1, GLA_HEADS * GLA_DK)
    wq = w_mla_q_up.reshape(DEPTH, MLA_Q_LORA, MLA_HEADS, MLA_NOPE + MLA_ROPE)
    wq = jnp.pad(wq, ((0, 0), (0, 0), (0, 0), (0, 2 * LANES - MLA_NOPE - MLA_ROPE)))
    wq = wq.reshape(DEPTH, MLA_Q_LORA, MLA_HEADS * 2 * LANES).astype(BF16)
    g_qr = jnp.pad(g_q_rope, ((0, 0), (0, LANES - MLA_ROPE))).reshape(DEPTH, 1, LANES)
    g_kr = jnp.pad(g_k_rope, ((0, 0), (0, LANES - MLA_ROPE))).reshape(DEPTH, 1, LANES)
    wr = jnp.concatenate([w_group_router, w_expert_router], axis=-1).transpose(0, 2, 1)
    wr = jnp.pad(wr, ((0, 0), (0, 32 - wr.shape[1]), (0, 0)))
    hi = wr.astype(BF16)
    lo = (wr - hi.astype(F32)).astype(BF16)
    router_w = jnp.stack([hi, lo], axis=1)
    rb = jnp.concatenate([b_group_router, b_expert_router], axis=-1)
    rb = jnp.pad(rb, ((0, 0), (0, 32 - rb.shape[1])))
    router_b = jnp.broadcast_to(rb[:, :, None], (DEPTH, 32, LANES))
    return wp, wdec, bdec, wq, g_qr, g_kr, router_w, router_b


def _rope_tables(dims):
    ll = dims["l_lat"]
    t = jnp.arange(ll)
    row = (t // GRID_W).astype(F32)
    col = (t % GRID_W).astype(F32)
    n_freq = MLA_ROPE // 4
    inv = ROPE_THETA ** (-jnp.arange(n_freq, dtype=F32) / n_freq)
    ang = jnp.stack([row[:, None] * inv, col[:, None] * inv], axis=1)
    cos, sin = jnp.cos(ang), jnp.sin(ang)
    cos64 = jnp.concatenate([cos, cos], axis=-1).reshape(ll, MLA_ROPE)
    sin64 = jnp.concatenate([-sin, sin], axis=-1).reshape(ll, MLA_ROPE)
    pad = jnp.zeros((ll, LANES - MLA_ROPE), F32)
    cos_l = jnp.tile(jnp.concatenate([cos64, pad], axis=-1), (dims["b_lat"], 1))
    sin_l = jnp.tile(jnp.concatenate([sin64, pad], axis=-1), (dims["b_lat"], 1))
    cos_c = jnp.concatenate([jnp.ones((dims["t_ctx"], MLA_ROPE), F32),
                             jnp.zeros((dims["t_ctx"], LANES - MLA_ROPE), F32)], axis=-1)
    sin_c = jnp.zeros((dims["t_ctx"], LANES), F32)
    return jnp.concatenate([cos_c, cos_l], axis=0), jnp.concatenate([sin_c, sin_l], axis=0)


def kernel(x_prompt, x_sample, c, cache_mla_ckv, cache_mla_krope, state_gla, c_ctx, w_ada, b_ada, g_norm1, g_norm2, w_in, w_gla_dec, b_gla_dec, g_gla, w_pool, pool_scale, g_mla_qa, w_mla_q_up, g_mla_kva, w_mla_kv_up, g_q_nope, g_q_rope, g_k_nope, g_k_rope, w_br_gla, w_br_pool, w_br_mla, w_o, w_group_router, b_group_router, w_expert_router, b_expert_router, w_exp_gate, w_exp_up, w_exp_down):
    b_ctx, l_ctx, _ = x_prompt.shape
    b_lat, l_lat, _ = x_sample.shape
    dims = dict(b_ctx=b_ctx, l_ctx=l_ctx, b_lat=b_lat, l_lat=l_lat,
                t_ctx=b_ctx * l_ctx, t_lat=b_lat * l_lat, t=b_ctx * l_ctx + b_lat * l_lat)
    t_ctx = dims["t_ctx"]
    assert l_ctx % SLAB == 0 and l_lat % SLAB == 0 and t_ctx % l_lat == 0 and 1 + b_lat <= 8

    wp, wdec, bdec, wq, g_qr, g_kr, router_w, router_b = _pack_params(
        w_in, w_gla_dec, b_gla_dec, w_mla_q_up, g_q_rope, g_k_rope,
        w_group_router, b_group_router, w_expert_router, b_expert_router)
    wkv = w_mla_kv_up.astype(BF16)
    r3 = lambda a: a.reshape(DEPTH, 1, a.shape[-1])
    gains = (r3(g_mla_qa), r3(g_mla_kva), r3(g_q_nope), g_qr, r3(g_k_nope), g_kr)
    w = dict(w_br_gla=w_br_gla.astype(BF16), w_br_pool=w_br_pool.astype(BF16),
             w_br_mla=w_br_mla.astype(BF16), w_o=w_o.astype(BF16), g_gla=r3(g_gla),
             g_norm2=r3(g_norm2), router_w=router_w, router_b=router_b,
             w_exp_gate=w_exp_gate.astype(BF16), w_exp_up=w_exp_up.astype(BF16),
             w_exp_down=w_exp_down.astype(BF16))
    w_pool_b = w_pool.astype(BF16)
    pool_scale3 = r3(pool_scale)
    g1 = r3(g_norm1)

    cond = jnp.concatenate([c_ctx[None, :], c, jnp.zeros((8 - 1 - b_lat, D), F32)], axis=0)
    mods = _modulation(cond, w_ada, b_ada)

    cos, sin = _rope_tables(dims)
    cache_kr = jnp.pad(cache_mla_krope, ((0, 0), (0, 0), (0, 0), (0, LANES - MLA_ROPE)))
    kc, vc = _cache_kv(cache_mla_ckv, cache_kr, wkv, r3(g_k_nope))
    gla_tables = _gla_tables(dims)

    x = jnp.concatenate([x_prompt.reshape(t_ctx, D), x_sample.reshape(dims["t_lat"], D)], axis=0)
    ckv_l, kr_l, st_l = [], [], []
    for l in range(DEPTH):
        p = _proj_in(x, mods, g1, wp, l, dims)
        s0 = jnp.concatenate([jnp.zeros((1, 2, GLA_HEADS, GLA_DV, GLA_DK), F32),
                              jnp.swapaxes(state_gla[:, l], -1, -2)], axis=0)
        o_f, o_b, s_fin = _gla(p, wdec[l], bdec[l], s0, gla_tables, dims)
        pool_out = _pool(p, w_pool_b, pool_scale3, l, dims)
        qr, qu, k, v, ckv, kro = _mla_prep(p, cos, sin, wq, wkv, gains, l, dims)
        attn_ctx = _attn_ctx(qr, k, v, dims)
        attn_lat = _attn_lat(qr, qu, k, v, kc, vc, l, dims)
        merged = _branch_merge(o_f, o_b, p, pool_out, attn_ctx, attn_lat, w, l, dims)
        x_mid, route = _merge(x, mods, merged, w, l, dims)
        x = _moe(x_mid, route, mods, w, l, dims)
        ckv_l.append(ckv[:t_ctx].reshape(b_ctx, l_ctx, MLA_KV_LORA))
        kr_l.append(kro[:t_ctx, :MLA_ROPE].reshape(b_ctx, l_ctx, MLA_ROPE))
        st_l.append(s_fin[:b_ctx])

    y_prompt = x[:t_ctx].reshape(b_ctx, l_ctx, D)
    y_sample = x[t_ctx:dims["t"]].reshape(b_lat, l_lat, D)
    return (y_prompt, y_sample, jnp.stack(ckv_l, axis=1), jnp.stack(kr_l, axis=1),
            jnp.swapaxes(jnp.stack(st_l, axis=1), -1, -2))
```
